```python
import math
import jax
import jax.numpy as jnp
from jax import lax
import numpy as np


D_MODEL = 2048
BATCH = 1
SEQ = 16384
DEPTH = 2

BRANCH_W = D_MODEL // 2
N_BRANCH = 4
NORM_EPS = 1e-6

HGRN_EXPAND = 128
HGRN_HEADS = BRANCH_W // HGRN_EXPAND
HGRN_CHUNK = 64

S5_GROUP = 16
S5_GROUPS = BRANCH_W // S5_GROUP
S5_STATE = 64
S5_DT_MIN = 1e-3
S5_DT_MAX = 1e-1
S5_MAX_RE = -1e-4

RWKV_HEAD = 64
RWKV_HEADS = BRANCH_W // RWKV_HEAD
RWKV_DECAY_LORA = 64
RWKV_A_LORA = 64
RWKV_SHIFT_W = 3 * BRANCH_W + RWKV_DECAY_LORA + RWKV_A_LORA
RWKV_DECAY_SCALE = 0.606531
RWKV_LN_EPS = 64e-5

M2_HEADDIM = 64
M2_HEADS = BRANCH_W // M2_HEADDIM
M2_GROUPS = 4
M2_STATE = 128
M2_CONV = 4
M2_CHUNK = 128
M2_CONV_CH = BRANCH_W + 2 * M2_GROUPS * M2_STATE
M2_NORM_EPS = 1e-5

IN_SPLITS = (BRANCH_W, BRANCH_W, BRANCH_W, BRANCH_W,
             BRANCH_W, BRANCH_W,
             RWKV_SHIFT_W, BRANCH_W,
             M2_CONV_CH, M2_HEADS, BRANCH_W,
             N_BRANCH * D_MODEL)
W_IN = sum(IN_SPLITS)

kernel_name = 'hybrid_hgrn2_s5_rwkv7_ssd_gated_block'

F32 = jnp.float32


def split_sizes(t, sizes):
    out, off = [], 0
    for s in sizes:
        out.append(t[..., off:off + s])
        off += s
    return out


def project_partitioned(h, w, sizes):
    out, off = [], 0
    for s in sizes:
        out.append(jnp.einsum('bld,de->ble', h, w[:, off:off + s].astype(F32)))
        off += s
    return out


def rmsnorm(x, g, eps=NORM_EPS):
    xf = x.astype(F32)
    return xf * lax.rsqrt(jnp.mean(xf * xf, axis=-1, keepdims=True) + eps) * g.astype(F32)


def group_rmsnorm(x, g, n_groups, eps):
    shp = x.shape
    xg = x.reshape(shp[:-1] + (n_groups, shp[-1] // n_groups))
    xg = xg * lax.rsqrt(jnp.mean(xg * xg, axis=-1, keepdims=True) + eps)
    return xg.reshape(shp) * g.astype(F32)


def token_shift(x):
    return jnp.pad(x, ((0, 0), (1, 0), (0, 0)))[:, :-1, :]


def causal_mask(n):
    return jnp.tril(jnp.ones((n, n), dtype=bool))


def gla_chunk_scan(q, k, v, log_f, chunk):
    bsz, seq, nh, dk = q.shape
    dv = v.shape[-1]
    n = seq // chunk

    def to_chunks(t):
        return t.reshape(bsz, n, chunk, nh, t.shape[-1]).transpose(1, 0, 3, 2, 4)

    mask = causal_mask(chunk)[:, :, None]

    def step(state, inp):
        qc, kc, vc, gc = inp
        cum = jnp.cumsum(gc, axis=2)
        o_inter = jnp.einsum('bhtk,bhkv->bhtv', qc * jnp.exp(cum), state)
        diff = cum[:, :, :, None, :] - cum[:, :, None, :, :]
        decay = jnp.exp(jnp.where(mask, diff, -jnp.inf))
        scores = jnp.einsum('bhtsk,bhsk->bhts', qc[:, :, :, None, :] * decay, kc)
        o_intra = jnp.einsum('bhts,bhsv->bhtv', scores, vc)
        last = cum[:, :, -1:, :]
        state = (jnp.exp(last[:, :, 0, :])[..., None] * state
                 + jnp.einsum('bhsk,bhsv->bhkv', kc * jnp.exp(last - cum), vc))
        return state, o_inter + o_intra

    s0 = jnp.zeros((bsz, nh, dk, dv), F32)
    _, o = lax.scan(step, s0, (to_chunks(q), to_chunks(k), to_chunks(v), to_chunks(log_f)))
    return o.transpose(1, 0, 3, 2, 4).reshape(bsz, seq, nh, dv)


def hgrn2_branch(q_raw, f_raw, i_raw, lb, norm_w):
    bsz, seq, w = q_raw.shape
    dv = w // HGRN_HEADS
    log_f = jnp.logaddexp(jnp.log(lb), jnp.log1p(-lb) + jax.nn.log_sigmoid(f_raw))
    k = (1.0 - lb) * jax.nn.sigmoid(-f_raw)
    q = jax.nn.silu(q_raw) * HGRN_EXPAND ** -0.5
    heads = lambda t, d: t.reshape(bsz, seq, HGRN_HEADS, d)
    o = gla_chunk_scan(heads(q, HGRN_EXPAND), heads(k, HGRN_EXPAND), heads(i_raw, dv),
                       heads(log_f, HGRN_EXPAND), HGRN_CHUNK)
    return group_rmsnorm(o.reshape(bsz, seq, w), norm_w, HGRN_HEADS, NORM_EPS)


def s5_branch(u, a_re, a_im, log_dt, b_re, b_im, c_re, c_im, d, w_glu, b_glu):
    bsz, seq, w = u.shape
    uf = u.reshape(bsz, seq, S5_GROUPS, S5_GROUP)
    lam_re = jnp.minimum(a_re.astype(F32), S5_MAX_RE)
    lam_im = a_im.astype(F32)
    dt = jnp.exp(log_dt.astype(F32))[:, None]
    dl_re, dl_im = lam_re * dt, lam_im * dt
    mag = jnp.exp(dl_re)
    num_re = mag * jnp.cos(dl_im) - 1.0
    num_im = mag * jnp.sin(dl_im)
    den = lam_re * lam_re + lam_im * lam_im
    coef_re = (num_re * lam_re + num_im * lam_im) / den
    coef_im = (num_im * lam_re - num_re * lam_im) / den
    b_re = b_re.astype(F32)
    b_im = b_im.astype(F32)
    bbar_re = coef_re[..., None] * b_re - coef_im[..., None] * b_im
    bbar_im = coef_re[..., None] * b_im + coef_im[..., None] * b_re
    bu_re = jnp.einsum('blgh,gph->blgp', uf, bbar_re)
    bu_im = jnp.einsum('blgh,gph->blgp', uf, bbar_im)
    counts = jnp.ones((1, seq, 1, 1), F32)

    def combine(e1, e2):
        n1, r1, i1 = e1
        n2, r2, i2 = e2
        m = jnp.exp(n2 * dl_re)
        ang = n2 * dl_im
        p_re, p_im = m * jnp.cos(ang), m * jnp.sin(ang)
        return (n1 + n2, p_re * r1 - p_im * i1 + r2, p_re * i1 + p_im * r1 + i2)

    _, s_re, s_im = lax.associative_scan(combine, (counts, bu_re, bu_im), axis=1)
    y = (jnp.einsum('blgp,ghp->blgh', s_re, c_re.astype(F32))
         - jnp.einsum('blgp,ghp->blgh', s_im, c_im.astype(F32)))
    y = (y + d.astype(F32).reshape(S5_GROUPS, S5_GROUP) * uf).reshape(bsz, seq, w)
    g = jax.nn.gelu(y)
    return g * jax.nn.sigmoid(g @ w_glu.astype(F32) + b_glu.astype(F32))


def rwkv7_branch(xs, mu, w0, w2, a0, a2, k_k, k_a, r_k, ln_w, ln_b):
    bsz, seq, _ = xs.shape
    nh, hd = RWKV_HEADS, RWKV_HEAD
    xs = xs + mu.astype(F32) * (token_shift(xs) - xs)
    r, k, v, wd, ad = split_sizes(xs, (BRANCH_W, BRANCH_W, BRANCH_W, RWKV_DECAY_LORA, RWKV_A_LORA))
    decay = jnp.exp(-RWKV_DECAY_SCALE * jax.nn.sigmoid(w0.astype(F32) + jnp.tanh(wd) @ w2.astype(F32)))
    iclr = jax.nn.sigmoid(a0.astype(F32) + ad @ a2.astype(F32))
    heads = lambda t: t.reshape(bsz, seq, nh, hd)
    r, k, v, decay, iclr = heads(r), heads(k), heads(v), heads(decay), heads(iclr)
    kk = k * k_k.astype(F32).reshape(nh, hd)
    kk = kk / jnp.maximum(jnp.sqrt(jnp.sum(kk * kk, axis=-1, keepdims=True)), 1e-12)
    k = k * (1.0 + (iclr - 1.0) * k_a.astype(F32).reshape(nh, hd))

    def step(state, inp):
        r_t, w_t, k_t, v_t, kk_t, a_t = inp
        sa = jnp.einsum('bhvk,bhk->bhv', state, -kk_t)
        state = (state * w_t[:, :, None, :]
                 + sa[..., None] * (kk_t * a_t)[:, :, None, :]
                 + v_t[..., None] * k_t[:, :, None, :])
        return state, jnp.einsum('bhvk,bhk->bhv', state, r_t)

    seq_major = lambda t: t.transpose(1, 0, 2, 3)
    s0 = jnp.zeros((bsz, nh, hd, hd), F32)
    _, y = lax.scan(step, s0, (seq_major(r), seq_major(decay), seq_major(k), seq_major(v),
                               seq_major(kk), seq_major(iclr)))
    y = y.transpose(1, 0, 2, 3)
    mean = jnp.mean(y, axis=-1, keepdims=True)
    var = jnp.mean(jnp.square(y - mean), axis=-1, keepdims=True)
    y = ((y - mean) * lax.rsqrt(var + RWKV_LN_EPS)).reshape(bsz, seq, BRANCH_W)
    y = y * ln_w.astype(F32) + ln_b.astype(F32)
    bonus = jnp.sum(r * k * r_k.astype(F32).reshape(nh, hd), axis=-1, keepdims=True) * v
    return y + bonus.reshape(bsz, seq, BRANCH_W)


def ssd_chunked(x, dt, a, bm, cm, chunk):
    bsz, seq, nh, hp = x.shape
    ng, ns = bm.shape[2], bm.shape[3]
    rep = nh // ng
    n = seq // chunk
    xd = (x * dt[..., None]).reshape(bsz, n, chunk, ng, rep, hp)
    a_cs = jnp.cumsum((dt * a).reshape(bsz, n, chunk, ng, rep), axis=2)
    bc = bm.reshape(bsz, n, chunk, ng, ns)
    cc = cm.reshape(bsz, n, chunk, ng, ns)
    seg = a_cs[:, :, :, None] - a_cs[:, :, None, :]
    decay = jnp.exp(jnp.where(causal_mask(chunk)[:, :, None, None], seg, -jnp.inf))
    cb = jnp.einsum('bclgn,bcsgn->bclsg', cc, bc)
    y_diag = jnp.einsum('bclsgr,bcsgrp->bclgrp', cb[..., None] * decay, xd)
    to_end = jnp.exp(a_cs[:, :, -1:] - a_cs)
    states = jnp.einsum('bcsgn,bcsgrp->bcgrpn', bc, xd * to_end[..., None])
    chunk_decay = jnp.exp(a_cs[:, :, -1])

    def step(state, inp):
        st, dec = inp
        return dec[..., None, None] * state + st, state

    s0 = jnp.zeros((bsz, ng, rep, hp, ns), F32)
    _, prev = lax.scan(step, s0, (jnp.moveaxis(states, 1, 0), jnp.moveaxis(chunk_decay, 1, 0)))
    prev = jnp.moveaxis(prev, 0, 1)
    y_off = jnp.einsum('bclgn,bcgrpn->bclgrp', cc, prev) * jnp.exp(a_cs)[..., None]
    return (y_diag + y_off).reshape(bsz, seq, nh, hp)


def mamba2_branch(xbc, dt_raw, z, conv_w, conv_b, dt_bias, a_log, d, norm_w):
    bsz, seq, _ = xbc.shape
    xbc = lax.conv_general_dilated(xbc.astype(F32), conv_w.astype(F32)[:, None, :],
                                   window_strides=(1,), padding=((M2_CONV - 1, 0),),
                                   dimension_numbers=('NWC', 'WIO', 'NWC'),
                                   feature_group_count=M2_CONV_CH)
    xbc = jax.nn.silu(xbc + conv_b.astype(F32))
    xh, bm, cm = split_sizes(xbc, (BRANCH_W, M2_GROUPS * M2_STATE, M2_GROUPS * M2_STATE))
    dt = jax.nn.softplus(dt_raw + dt_bias.astype(F32))
    a = -jnp.exp(a_log.astype(F32))
    xh = xh.reshape(bsz, seq, M2_HEADS, M2_HEADDIM)
    y = ssd_chunked(xh, dt, a, bm.reshape(bsz, seq, M2_GROUPS, M2_STATE),
                    cm.reshape(bsz, seq, M2_GROUPS, M2_STATE), M2_CHUNK)
    y = y + d.astype(F32)[:, None] * xh
    y = y.reshape(bsz, seq, BRANCH_W) * jax.nn.silu(z)
    return group_rmsnorm(y, norm_w, M2_GROUPS, M2_NORM_EPS)


def setup_inputs(seed: int = 0) -> dict:
    key = jax.random.key(seed)
    ks = jax.random.split(key, 40)
    W = BRANCH_W

    def nrm(i, shape, scale):
        return scale * jax.random.normal(ks[i], shape, F32)

    def uni(i, shape, lo, hi):
        return jax.random.uniform(ks[i], shape, F32, lo, hi)

    s5_im_init = jnp.pi * jnp.arange(S5_STATE, dtype=F32)
    dt_m2 = jnp.exp(uni(31, (DEPTH, M2_HEADS), math.log(1e-3), math.log(1e-1)))
    return {
        'x': nrm(0, (BATCH, SEQ, D_MODEL), 1.0),
        'norm_pre': 1.0 + nrm(1, (DEPTH, D_MODEL), 0.02),
        'norm_post': 1.0 + nrm(2, (DEPTH, D_MODEL), 0.02),
        'w_in': nrm(3, (DEPTH, D_MODEL, W_IN), D_MODEL ** -0.5),
        'gate_bias': nrm(4, (DEPTH, N_BRANCH, D_MODEL), 0.01),
        'w_up': nrm(5, (DEPTH, N_BRANCH, W, D_MODEL), W ** -0.5),
        'w_out': nrm(6, (DEPTH, D_MODEL, D_MODEL), D_MODEL ** -0.5),
        'hgrn_lb_logits': nrm(7, (DEPTH, W), 0.1),
        'hgrn_norm': 1.0 + nrm(8, (DEPTH, W), 0.02),
        's5_a_re': -0.5 + nrm(9, (DEPTH, S5_GROUPS, S5_STATE), 0.01),
        's5_a_im': s5_im_init + nrm(10, (DEPTH, S5_GROUPS, S5_STATE), 0.01),
        's5_log_dt': uni(11, (DEPTH, S5_GROUPS), math.log(S5_DT_MIN), math.log(S5_DT_MAX)),
        's5_b_re': nrm(12, (DEPTH, S5_GROUPS, S5_STATE, S5_GROUP), (2 * S5_GROUP) ** -0.5),
        's5_b_im': nrm(13, (DEPTH, S5_GROUPS, S5_STATE, S5_GROUP), (2 * S5_GROUP) ** -0.5),
        's5_c_re': nrm(14, (DEPTH, S5_GROUPS, S5_GROUP, S5_STATE), (2 * S5_STATE) ** -0.5),
        's5_c_im': nrm(15, (DEPTH, S5_GROUPS, S5_GROUP, S5_STATE), (2 * S5_STATE) ** -0.5),
        's5_d': nrm(16, (DEPTH, W), 1.0),
        's5_w_glu': nrm(17, (DEPTH, W, W), W ** -0.5),
        's5_b_glu': nrm(18, (DEPTH, W), 0.01),
        'rwkv_mu': uni(19, (DEPTH, RWKV_SHIFT_W), 0.0, 1.0),
        'rwkv_w0': uni(20, (DEPTH, W), -3.0, 3.0),
        'rwkv_w2': nrm(21, (DEPTH, RWKV_DECAY_LORA, W), 0.1),
        'rwkv_a0': nrm(22, (DEPTH, W), 0.5),
        'rwkv_a2': nrm(23, (DEPTH, RWKV_A_LORA, W), 0.1),
        'rwkv_k_k': 0.85 + nrm(24, (DEPTH, W), 0.02),
        'rwkv_k_a': 1.0 + nrm(25, (DEPTH, W), 0.02),
        'rwkv_r_k': nrm(26, (DEPTH, W), 0.1),
        'rwkv_ln_w': 1.0 + nrm(27, (DEPTH, W), 0.02),
        'rwkv_ln_b': nrm(28, (DEPTH, W), 0.01),
        'm2_conv_w': nrm(29, (DEPTH, M2_CONV, M2_CONV_CH), M2_CONV ** -0.5),
        'm2_conv_b': nrm(30, (DEPTH, M2_CONV_CH), 0.01),
        'm2_dt_bias': dt_m2 + jnp.log(-jnp.expm1(-dt_m2)),
        'm2_a_log': jnp.log(uni(32, (DEPTH, M2_HEADS), 1.0, 16.0)),
        'm2_d': 1.0 + nrm(33, (DEPTH, M2_HEADS), 0.1),
        'm2_norm': 1.0 + nrm(34, (DEPTH, W), 0.02),
    }


def reference(x, norm_pre, norm_post, w_in, gate_bias, w_up, w_out, hgrn_lb_logits, hgrn_norm,
              s5_a_re, s5_a_im, s5_log_dt, s5_b_re, s5_b_im, s5_c_re, s5_c_im, s5_d, s5_w_glu,
              s5_b_glu, rwkv_mu, rwkv_w0, rwkv_w2, rwkv_a0, rwkv_a2, rwkv_k_k, rwkv_k_a, rwkv_r_k,
              rwkv_ln_w, rwkv_ln_b, m2_conv_w, m2_conv_b, m2_dt_bias, m2_a_log, m2_d, m2_norm):
    lb_all = jnp.cumsum(jax.nn.softmax(hgrn_lb_logits.astype(F32), axis=0), axis=0)
    lb_all = lb_all - lb_all[0:1]
    res = x
    for l in range(DEPTH):
        h = rmsnorm(res, norm_pre[l])
        (q_a, f_a, i_a, z_a, u_b, z_b, rkvwa_c, z_c, xbc_d, dt_d, z_d,
         gate_pre) = project_partitioned(h, w_in[l], IN_SPLITS)
        o_a = hgrn2_branch(q_a, f_a, i_a, lb_all[l], hgrn_norm[l]) * jax.nn.silu(z_a)
        o_b = s5_branch(u_b, s5_a_re[l], s5_a_im[l], s5_log_dt[l], s5_b_re[l], s5_b_im[l],
                        s5_c_re[l], s5_c_im[l], s5_d[l], s5_w_glu[l], s5_b_glu[l]) * jax.nn.silu(z_b)
        o_c = rwkv7_branch(rkvwa_c, rwkv_mu[l], rwkv_w0[l], rwkv_w2[l], rwkv_a0[l], rwkv_a2[l],
                           rwkv_k_k[l], rwkv_k_a[l], rwkv_r_k[l], rwkv_ln_w[l], rwkv_ln_b[l]) * jax.nn.silu(z_c)
        o_d = mamba2_branch(xbc_d, dt_d, z_d, m2_conv_w[l], m2_conv_b[l], m2_dt_bias[l],
                            m2_a_log[l], m2_d[l], m2_norm[l])
        outs = (o_a, o_b, o_c, o_d)
        merged = None
        for b in range(N_BRANCH):
            gate = jax.nn.sigmoid(gate_pre[..., b * D_MODEL:(b + 1) * D_MODEL] + gate_bias[l, b].astype(F32))
            contrib = gate * jnp.einsum('blw,wd->bld', outs[b], w_up[l, b].astype(F32))
            merged = contrib if merged is None else merged + contrib
        y = jnp.einsum('bld,de->ble', merged, w_out[l].astype(F32))
        res = res + rmsnorm(y, norm_post[l])
    return res.astype(x.dtype)
```

```python
import functools
import math

import jax
import jax.numpy as jnp
from jax import lax
from jax.experimental import pallas as pl
from jax.experimental.pallas import tpu as pltpu

F32 = jnp.float32
BF16 = jnp.bfloat16
HI = lax.Precision.HIGHEST

D_MODEL = 2048
W = D_MODEL // 2
N_BRANCH = 4
NORM_EPS = 1e-6

HGRN_HEADS = 8
HGRN_DK = 128
HGRN_CHUNK = 64
HGRN_SUB = 16

S5_GROUP = 16
S5_GROUPS = 64
S5_STATE = 64
S5_CHUNK = 16
S5_MAX_RE = -1e-4
S5_GBLK = 8

RWKV_HEAD = 64
RWKV_LORA = 64
RWKV_CHUNK = 64
RWKV_DECAY_SCALE = 0.606531
RWKV_LN_EPS = 64e-5

M2_HEADS = 16
M2_HEADDIM = 64
M2_GROUPS = 4
M2_STATE = 128
M2_CONV = 4
M2_CHUNK = 128
M2_NORM_EPS = 1e-5

LANE = 128
SUB = 8
VMEM_LIMIT = 56 * 1024 * 1024

COL_QA, COL_FA, COL_IA, COL_ZA = 0, 1, 2, 3
COL_UB, COL_ZB = 4, 5
COL_RC, COL_KC, COL_VC, COL_ZC = 6, 7, 8, 9
COL_XD, COL_ZD, COL_BCD = 10, 11, 12
COL_GATE = 13
N_WIDE = 21
COL_WA = N_WIDE * 8
COL_DT = N_WIDE * 8 + 1
P_COLS = N_WIDE * 1024 + 2 * LANE
MM_TN = 1280


def _cparams(sem):
    return pltpu.CompilerParams(dimension_semantics=sem, vmem_limit_bytes=VMEM_LIMIT)


def _dot(a, b):
    return jnp.dot(a.astype(BF16), b.astype(BF16), preferred_element_type=F32)


def _dot_nt(a, b):
    return lax.dot_general(a.astype(BF16), b.astype(BF16), (((1,), (1,)), ((), ())),
                           preferred_element_type=F32)


def _dot_tn(a, b):
    return lax.dot_general(a.astype(BF16), b.astype(BF16), (((0,), (0,)), ((), ())),
                           preferred_element_type=F32)


def _dot_hi(a, b):
    return jnp.dot(a, b, precision=HI, preferred_element_type=F32)


def _silu(x):
    return x * jax.nn.sigmoid(x)


def _cumsum_rows(x):
    n = x.shape[0]
    row = lax.broadcasted_iota(jnp.int32, x.shape, 0)
    s = 1
    while s < n:
        x = x + jnp.where(row >= s, pltpu.roll(x, s, 0), 0.0)
        s *= 2
    return x


def _prenorm_kernel(x_ref, g_ref, o_ref):
    x = x_ref[...]
    o_ref[...] = (x * lax.rsqrt(jnp.mean(x * x, axis=-1, keepdims=True) + NORM_EPS)
                  * g_ref[...]).astype(o_ref.dtype)


def _prenorm(x, g):
    L = x.shape[0]
    tm = min(512, L)
    return pl.pallas_call(
        _prenorm_kernel,
        out_shape=jax.ShapeDtypeStruct((L, D_MODEL), BF16),
        grid=(L // tm,),
        in_specs=[pl.BlockSpec((tm, D_MODEL), lambda i: (i, 0)),
                  pl.BlockSpec((1, D_MODEL), lambda i: (0, 0))],
        out_specs=pl.BlockSpec((tm, D_MODEL), lambda i: (i, 0)),
        compiler_params=_cparams(("parallel",)),
    )(x, g.reshape(1, D_MODEL))


def _mm_kernel(a_ref, w_ref, o_ref):
    o_ref[...] = jnp.dot(a_ref[...], w_ref[...], preferred_element_type=F32).astype(o_ref.dtype)


def _matmul(a, w, tn, out_dtype=F32):
    M, K = a.shape
    N = w.shape[1]
    tm = min(1024, M)
    return pl.pallas_call(
        _mm_kernel,
        out_shape=jax.ShapeDtypeStruct((M, N), out_dtype),
        grid=(M // tm, N // tn),
        in_specs=[pl.BlockSpec((tm, K), lambda i, j: (i, 0)),
                  pl.BlockSpec((K, tn), lambda i, j: (0, j))],
        out_specs=pl.BlockSpec((tm, tn), lambda i, j: (i, j)),
        compiler_params=_cparams(("parallel", "parallel")),
    )(a, w)


def _hgrn_kernel(q_ref, f_ref, i_ref, z_ref, loglb_ref, log1mlb_ref, omlb_ref, nw_ref,
                 o_ref, st_ref, *, n_chunks):
    @pl.when(pl.program_id(1) == 0)
    def _():
        st_ref[...] = jnp.zeros_like(st_ref)

    C, SB = HGRN_CHUNK, HGRN_SUB
    loglb, log1mlb, omlb, nw = loglb_ref[...], log1mlb_ref[...], omlb_ref[...], nw_ref[...]
    row = lax.broadcasted_iota(jnp.int32, (C, HGRN_DK), 0)
    trow = lax.broadcasted_iota(jnp.int32, (SB, HGRN_DK), 0)
    col_c = lax.broadcasted_iota(jnp.int32, (SB, C), 1)

    def chunk(c, carry):
        rows = pl.ds(pl.multiple_of(c * C, C), C)
        f = f_ref[rows, :]
        log_sig = jnp.minimum(f, 0.0) - jnp.log1p(jnp.exp(-jnp.abs(f)))
        b = log1mlb + log_sig
        logf = jnp.maximum(loglb, b) + jnp.log1p(jnp.exp(-jnp.abs(loglb - b)))
        cum = _cumsum_rows(logf)
        k = omlb * jax.nn.sigmoid(-f)
        q = _silu(q_ref[rows, :]) * (HGRN_DK ** -0.5)
        v = i_ref[rows, :]
        st = st_ref[...]

        o = _dot_nt(q * jnp.exp(cum), st)

        blocks = []
        for bi in range(C // SB):
            r0 = bi * SB
            ref = cum[r0:r0 + 1, :]
            cum_b = cum[r0:r0 + SB, :]
            q_b = q[r0:r0 + SB, :]
            if bi == 0:
                sc = jnp.zeros((SB, C), F32)
            else:
                kmod = k * jnp.exp(jnp.where(row < r0, ref - cum, -jnp.inf))
                sc = _dot_nt(q_b * jnp.exp(cum_b - ref), kmod)
            for s in range(SB):
                d = jnp.where(trow >= s, cum_b - cum[r0 + s:r0 + s + 1, :], -jnp.inf)
                p = q_b * k[r0 + s:r0 + s + 1, :] * jnp.exp(d)
                sc = jnp.where(col_c == r0 + s, jnp.sum(p, axis=-1, keepdims=True), sc)
            blocks.append(sc)
        scores = jnp.concatenate(blocks, axis=0)
        o = o + _dot(scores, v)

        last = cum[C - 1:C, :]
        st_ref[...] = st * jnp.exp(last) + _dot_tn(v, k * jnp.exp(last - cum))

        o = o * lax.rsqrt(jnp.mean(o * o, axis=-1, keepdims=True) + NORM_EPS) * nw
        o_ref[rows, :] = (o * _silu(z_ref[rows, :])).astype(o_ref.dtype)
        return carry

    lax.fori_loop(0, n_chunks, chunk, 0)


def _hgrn(p, loglb, log1mlb, omlb, nw):
    L = p.shape[0]
    T = min(512, L)
    nt = L // T
    blk = lambda c: pl.BlockSpec((T, HGRN_DK), lambda h, t, c=c: (t, c * HGRN_HEADS + h))
    vec = pl.BlockSpec((1, HGRN_DK), lambda h, t: (0, h))
    return pl.pallas_call(
        functools.partial(_hgrn_kernel, n_chunks=T // HGRN_CHUNK),
        out_shape=jax.ShapeDtypeStruct((L, W), BF16),
        grid=(HGRN_HEADS, nt),
        in_specs=[blk(COL_QA), blk(COL_FA), blk(COL_IA), blk(COL_ZA), vec, vec, vec, vec],
        out_specs=pl.BlockSpec((T, HGRN_DK), lambda h, t: (t, h)),
        scratch_shapes=[pltpu.VMEM((HGRN_DK, HGRN_DK), F32)],
        compiler_params=_cparams(("arbitrary", "arbitrary")),
    )(p, p, p, p, loglb, log1mlb, omlb, nw)


def _ssd_kernel(x_ref, bc_ref, z_ref, dt_ref, cw_ref, cb_ref, dtb_ref, a_ref, e_ref, d_ref, nw_ref,
                o_ref, xb_ref, st_ref):
    C = M2_CHUNK
    CH = 2 * W

    @pl.when(pl.program_id(0) == 0)
    def _():
        xb_ref[0:SUB, :] = jnp.zeros((SUB, CH), F32)
        st_ref[...] = jnp.zeros_like(st_ref)

    xb_ref[SUB:SUB + C, 0:W] = x_ref[...]
    xb_ref[SUB:SUB + C, W:CH] = bc_ref[...]
    acc = cb_ref[...] + cw_ref[0:1, :] * xb_ref[pl.ds(SUB - 3, C), :]
    for j in range(1, M2_CONV):
        acc = acc + cw_ref[j:j + 1, :] * xb_ref[pl.ds(SUB - 3 + j, C), :]
    xb_ref[0:SUB, :] = xb_ref[C:C + SUB, :]
    xc = _silu(acc)
    xh = xc[:, 0:W]

    dt = jax.nn.softplus(dt_ref[...] + dtb_ref[...])
    da = dt * a_ref[...]
    r_i = lax.broadcasted_iota(jnp.int32, (C, C), 0)
    c_i = lax.broadcasted_iota(jnp.int32, (C, C), 1)
    causal = r_i >= c_i
    a_cs = _dot_hi(causal.astype(F32), da)
    a_cs_t = a_cs.T
    e = e_ref[...]
    ea = jnp.exp(a_cs)
    dt_e = _dot_hi(dt, e)
    ea_e = _dot_hi(ea, e)
    te_e = _dot_hi(jnp.exp(a_cs[C - 1:C, :] - a_cs), e)
    xd = xh * dt_e
    xd_end = xd * te_e

    GW = W // M2_GROUPS
    lane_g = lax.broadcasted_iota(jnp.int32, (C, GW), 1)
    ys = []
    for g in range(M2_GROUPS):
        gs = slice(g * GW, (g + 1) * GW)
        bg = xc[:, W + g * M2_STATE:W + (g + 1) * M2_STATE]
        cg = xc[:, W + M2_GROUPS * M2_STATE + g * M2_STATE:W + M2_GROUPS * M2_STATE + (g + 1) * M2_STATE]
        bg_t = bg.T
        cb = _dot(cg, bg_t)
        st = st_ref[g]
        y = _dot(cg, st) * ea_e[:, gs]
        xd_g = xd[:, gs]
        ms, xs = [], []
        for j in range(M2_HEADS // M2_GROUPS):
            h = g * (M2_HEADS // M2_GROUPS) + j
            seg = a_cs[:, h:h + 1] - a_cs_t[h:h + 1, :]
            ms.append(cb * jnp.exp(jnp.where(causal, seg, -jnp.inf)))
            xs.append(jnp.where(lane_g // M2_HEADDIM == j, xd_g, 0.0))
        y = y + _dot(jnp.concatenate(ms, axis=1), jnp.concatenate(xs, axis=0))
        st_ref[g] = st * ea_e[C - 1:C, gs] + _dot(bg_t, xd_end[:, gs])
        y = (y + d_ref[:, gs] * xh[:, gs]) * _silu(z_ref[:, gs])
        y = y * lax.rsqrt(jnp.mean(y * y, axis=-1, keepdims=True) + M2_NORM_EPS)
        ys.append(y)
    o_ref[...] = (jnp.concatenate(ys, axis=1) * nw_ref[...]).astype(o_ref.dtype)


def _ssd(p, cw, cb, dtb, a, e, d_e, nw):
    L = p.shape[0]
    C = M2_CHUNK
    wide = lambda c: pl.BlockSpec((C, W), lambda t, c=c: (t, c))
    full = lambda a_: pl.BlockSpec(a_.shape, lambda t: (0,) * a_.ndim)
    return pl.pallas_call(
        _ssd_kernel,
        out_shape=jax.ShapeDtypeStruct((L, W), BF16),
        grid=(L // C,),
        in_specs=[wide(COL_XD), wide(COL_BCD), wide(COL_ZD),
                  pl.BlockSpec((C, LANE), lambda t: (t, COL_DT)),
                  full(cw), full(cb), full(dtb), full(a), full(e), full(d_e), full(nw)],
        out_specs=pl.BlockSpec((C, W), lambda t: (t, 0)),
        scratch_shapes=[pltpu.VMEM((SUB + C + SUB, 2 * W), F32),
                        pltpu.VMEM((M2_GROUPS, M2_STATE, W // M2_GROUPS), F32)],
        compiler_params=_cparams(("arbitrary",)),
    )(p, p, p, p, cw, cb, dtb, a, e, d_e, nw)


def _rwkv_kernel(r_ref, k_ref, v_ref, wa_ref, z_ref, mu_ref, muwa_ref, w0_ref, w2_ref, a0_ref, a2_ref,
                 kk_ref, ka_ref, rk_ref, lnw_ref, lnb_ref, bd_ref,
                 o_ref, xb_ref, wab_ref, st_ref):
    C = RWKV_CHUNK

    @pl.when(pl.program_id(0) == 0)
    def _():
        xb_ref[0:SUB, :] = jnp.zeros((SUB, 3 * W), F32)
        wab_ref[0:SUB, :] = jnp.zeros((SUB, LANE), F32)
        st_ref[...] = jnp.zeros_like(st_ref)

    xb_ref[SUB:SUB + C, 0:W] = r_ref[...]
    xb_ref[SUB:SUB + C, W:2 * W] = k_ref[...]
    xb_ref[SUB:SUB + C, 2 * W:3 * W] = v_ref[...]
    wab_ref[SUB:SUB + C, :] = wa_ref[...]
    cur = xb_ref[SUB:SUB + C, :]
    xs = cur + mu_ref[...] * (xb_ref[pl.ds(SUB - 1, C), :] - cur)
    wa_cur = wab_ref[SUB:SUB + C, :]
    wa = wa_cur + muwa_ref[...] * (wab_ref[pl.ds(SUB - 1, C), :] - wa_cur)
    xb_ref[0:SUB, :] = xb_ref[C:C + SUB, :]
    wab_ref[0:SUB, :] = wab_ref[C:C + SUB, :]

    logw_all = -RWKV_DECAY_SCALE * jax.nn.sigmoid(w0_ref[...] + _dot(jnp.tanh(wa), w2_ref[...]))
    iclr_all = jax.nn.sigmoid(a0_ref[...] + _dot(wa, a2_ref[...]))
    bd = bd_ref[...]

    lane = lax.broadcasted_iota(jnp.int32, (C, LANE), 1)
    lo = lane < RWKV_HEAD
    t_i = lax.broadcasted_iota(jnp.int32, (C, 2 * C), 0)
    s_i = lax.broadcasted_iota(jnp.int32, (C, 2 * C), 1) % C
    strict = t_i > s_i
    incl = t_i >= s_i

    def stack(x):
        return jnp.concatenate([jnp.where(lo, x, 0.0), jnp.where(lo, 0.0, x)], axis=0)

    for i in range(W // LANE):
        sl = slice(i * LANE, (i + 1) * LANE)
        r = xs[:, sl]
        k = xs[:, W + i * LANE:W + (i + 1) * LANE]
        v = xs[:, 2 * W + i * LANE:2 * W + (i + 1) * LANE]
        logw = logw_all[:, sl]
        a = iclr_all[:, sl]
        kk = k * kk_ref[:, sl]
        kk = kk / jnp.maximum(jnp.sqrt(_dot_hi(kk * kk, bd)), 1e-12)
        k2 = k * (1.0 + (a - 1.0) * ka_ref[:, sl])

        cum = _cumsum_rows(logw)
        e_inv = jnp.exp(-cum)
        a_t = -kk * jnp.exp(cum - logw)
        r_t = r * jnp.exp(cum)
        b_s = stack(kk * a * e_inv)
        k_s = stack(k2 * e_inv)
        v_s = stack(v)
        ar = jnp.concatenate([a_t, r_t], axis=0)
        sc = _dot_nt(ar, jnp.concatenate([b_s, k_s], axis=0))
        a_ab = jnp.where(strict, sc[0:C, 0:2 * C], 0.0)
        a_ak = jnp.where(strict, sc[0:C, 2 * C:4 * C], 0.0)
        r_ab = jnp.where(incl, sc[C:2 * C, 0:2 * C], 0.0)
        r_ak = jnp.where(incl, sc[C:2 * C, 2 * C:4 * C], 0.0)
        st = st_ref[i]
        ah = _dot_nt(ar, st)
        w_ = ah[0:C, :] + _dot(a_ak, v_s)
        u_s = stack(w_)
        n_p = stack(a_ab)
        step = 1
        while True:
            u_s = u_s + _dot_hi(n_p, u_s)
            step *= 2
            if step >= C:
                break
            n_p = _dot_hi(n_p, n_p)
        uv = jnp.concatenate([u_s, v_s], axis=0)
        y = ah[C:2 * C, :] + _dot(jnp.concatenate([r_ab, r_ak], axis=1), uv)
        st_ref[i] = (st + _dot_tn(uv, jnp.concatenate([b_s, k_s], axis=0))) * jnp.exp(cum[C - 1:C, :])

        mean = _dot_hi(y, bd) * (1.0 / RWKV_HEAD)
        yc = y - mean
        var = _dot_hi(yc * yc, bd) * (1.0 / RWKV_HEAD)
        yn = yc * lax.rsqrt(var + RWKV_LN_EPS) * lnw_ref[:, sl] + lnb_ref[:, sl]
        bonus = _dot_hi(r * k2 * rk_ref[:, sl], bd) * v
        o_ref[:, sl] = ((yn + bonus) * _silu(z_ref[:, sl])).astype(o_ref.dtype)


def _rwkv(p, mu, muwa, w0, w2p, a0, a2p, k_k, k_a, r_k, ln_w, ln_b, bd):
    L = p.shape[0]
    C = RWKV_CHUNK
    wide = lambda c: pl.BlockSpec((C, W), lambda t, c=c: (t, c))
    full = lambda a_: pl.BlockSpec(a_.shape, lambda t: (0,) * a_.ndim)
    consts = (mu, muwa, w0, w2p, a0, a2p, k_k, k_a, r_k, ln_w, ln_b, bd)
    return pl.pallas_call(
        _rwkv_kernel,
        out_shape=jax.ShapeDtypeStruct((L, W), BF16),
        grid=(L // C,),
        in_specs=[wide(COL_RC), wide(COL_KC), wide(COL_VC),
                  pl.BlockSpec((C, LANE), lambda t: (t, COL_WA)), wide(COL_ZC)]
                 + [full(c) for c in consts],
        out_specs=pl.BlockSpec((C, W), lambda t: (t, 0)),
        scratch_shapes=[pltpu.VMEM((SUB + C + SUB, 3 * W), F32),
                        pltpu.VMEM((SUB + C + SUB, LANE), F32),
                        pltpu.VMEM((W // LANE, LANE, LANE), F32)],
        compiler_params=_cparams(("arbitrary",)),
    )(p, p, p, p, p, *consts)


def _s5_kernel(u_ref, tq_ref, m_ref, msw_ref, l1_ref, l2_ref, l2s_ref, y_ref, xl_ref, xls_ref, xp_ref):
    G = S5_GBLK
    n = u_ref.shape[1]
    for g in range(G):
        u = u_ref[g]
        xl_ref[:, g, :] = jnp.dot(u, m_ref[g], preferred_element_type=F32)
        xls_ref[:, g, :] = jnp.dot(u, msw_ref[g], preferred_element_type=F32)
    l1, l2, l2s = l1_ref[...], l2_ref[...], l2s_ref[...]

    def step(i, carry):
        x, xs = carry
        xp_ref[i] = x
        return (l1 * x + l2 * xs + xl_ref[i], l1 * xs + l2s * x + xls_ref[i])

    zero = jnp.zeros((G, 2 * S5_STATE), F32)
    lax.fori_loop(0, n, step, (zero, zero))
    for g in range(G):
        lhs = jnp.concatenate([u_ref[g], xp_ref[:, g, :].astype(BF16)], axis=1)
        y_ref[g] = jnp.dot(lhs, tq_ref[g], preferred_element_type=F32)


def _s5_scan(u_t, tq, m, msw, l1, l2, l2s):
    G = S5_GBLK
    n = u_t.shape[1]
    cw = S5_CHUNK * S5_GROUP
    blk3 = lambda a_: pl.BlockSpec((G,) + a_.shape[1:], lambda g: (g, 0, 0))
    blk2 = lambda a_: pl.BlockSpec((None, G, a_.shape[-1]), lambda g: (g, 0, 0))
    return pl.pallas_call(
        _s5_kernel,
        out_shape=jax.ShapeDtypeStruct((S5_GROUPS, n, cw), F32),
        grid=(S5_GROUPS // G,),
        in_specs=[blk3(u_t), blk3(tq), blk3(m), blk3(msw), blk2(l1), blk2(l2), blk2(l2s)],
        out_specs=pl.BlockSpec((G, n, cw), lambda g: (g, 0, 0)),
        scratch_shapes=[pltpu.VMEM((n, G, 2 * S5_STATE), F32)] * 3,
        compiler_params=_cparams(("parallel",)),
    )(u_t, tq, m, msw, l1, l2, l2s)


def _s5_out_kernel(y_ref, u_ref, z_ref, d_ref, wg_ref, bg_ref, o_ref):
    y = y_ref[...] + d_ref[...] * u_ref[...]
    g = jax.nn.gelu(y)
    o = g * jax.nn.sigmoid(_dot(g, wg_ref[...]) + bg_ref[...])
    o_ref[...] = (o * _silu(z_ref[...])).astype(o_ref.dtype)


def _s5_out(y, p, d, wg, bg):
    L = y.shape[0]
    tm = min(512, L)
    vec = pl.BlockSpec((1, W), lambda t: (0, 0))
    return pl.pallas_call(
        _s5_out_kernel,
        out_shape=jax.ShapeDtypeStruct((L, W), BF16),
        grid=(L // tm,),
        in_specs=[pl.BlockSpec((tm, W), lambda t: (t, 0)),
                  pl.BlockSpec((tm, W), lambda t: (t, COL_UB)),
                  pl.BlockSpec((tm, W), lambda t: (t, COL_ZB)),
                  vec, pl.BlockSpec((W, W), lambda t: (0, 0)), vec],
        out_specs=pl.BlockSpec((tm, W), lambda t: (t, 0)),
        compiler_params=_cparams(("parallel",)),
    )(y, p, p, d, wg, bg)


def _s5_tables(a_re, a_im, log_dt, b_re, b_im, c_re, c_im):
    J = S5_CHUNK
    lam_re = jnp.minimum(a_re, S5_MAX_RE)
    lam_im = a_im
    dt = jnp.exp(log_dt)[:, None]
    dl_re, dl_im = lam_re * dt, lam_im * dt
    mag = jnp.exp(dl_re)
    num_re = mag * jnp.cos(dl_im) - 1.0
    num_im = mag * jnp.sin(dl_im)
    den = lam_re * lam_re + lam_im * lam_im
    coef_re = (num_re * lam_re + num_im * lam_im) / den
    coef_im = (num_im * lam_re - num_re * lam_im) / den
    bb_re = coef_re[..., None] * b_re - coef_im[..., None] * b_im
    bb_im = coef_re[..., None] * b_im + coef_im[..., None] * b_re
    tau = jnp.arange(J + 1, dtype=F32)[:, None, None]
    pm = jnp.exp(tau * dl_re)
    pw_re, pw_im = pm * jnp.cos(tau * dl_im), pm * jnp.sin(tau * dl_im)
    pb_re = pw_re[..., None] * bb_re - pw_im[..., None] * bb_im
    pb_im = pw_re[..., None] * bb_im + pw_im[..., None] * bb_re
    kern = (jnp.einsum('gop,tgph->tgoh', c_re, pb_re[:J], precision=HI)
            - jnp.einsum('gop,tgph->tgoh', c_im, pb_im[:J], precision=HI))
    t_idx = jnp.arange(J)
    lag = t_idx[None, :] - t_idx[:, None]
    toe = jnp.where((lag >= 0)[:, :, None, None, None],
                    kern[jnp.clip(lag, 0, J - 1)], 0.0)
    toe = toe.transpose(2, 0, 4, 1, 3).reshape(S5_GROUPS, J * S5_GROUP, J * S5_GROUP)
    rev_re, rev_im = pb_re[J - 1 - t_idx], pb_im[J - 1 - t_idx]
    to_m = lambda x: x.transpose(1, 0, 3, 2).reshape(S5_GROUPS, J * S5_GROUP, S5_STATE)
    m = jnp.concatenate([to_m(rev_re), to_m(rev_im)], axis=-1)
    msw = jnp.concatenate([to_m(rev_im), to_m(rev_re)], axis=-1)
    q_re = (c_re[None] * pw_re[1:, :, None, :] - c_im[None] * pw_im[1:, :, None, :])
    q_im = -(c_re[None] * pw_im[1:, :, None, :] + c_im[None] * pw_re[1:, :, None, :])
    to_q = lambda x: x.transpose(1, 3, 0, 2).reshape(S5_GROUPS, S5_STATE, J * S5_GROUP)
    tq = jnp.concatenate([toe, to_q(q_re), to_q(q_im)], axis=1)
    lj_re, lj_im = pw_re[J], pw_im[J]
    l1 = jnp.concatenate([lj_re, lj_re], axis=-1)
    l2 = jnp.concatenate([-lj_im, lj_im], axis=-1)
    l2s = jnp.concatenate([lj_im, -lj_im], axis=-1)
    rs = lambda x: x.reshape(S5_GROUPS // S5_GBLK, S5_GBLK, 2 * S5_STATE)
    return tq.astype(BF16), m.astype(BF16), msw.astype(BF16), rs(l1), rs(l2), rs(l2s)


def _s5(p, tables, d, wg, bg):
    L = p.shape[0]
    n = L // S5_CHUNK
    u = p[:, COL_UB * W:(COL_UB + 1) * W]
    u_t = (u.reshape(n, S5_CHUNK, S5_GROUPS, S5_GROUP).transpose(2, 0, 1, 3)
           .reshape(S5_GROUPS, n, S5_CHUNK * S5_GROUP).astype(BF16))
    y_t = _s5_scan(u_t, *tables)
    y = (y_t.reshape(S5_GROUPS, n, S5_CHUNK, S5_GROUP).transpose(1, 2, 0, 3).reshape(L, W))
    return _s5_out(y, p, d, wg, bg)


def _up_kernel(oa_ref, ob_ref, oc_ref, od_ref, ga_ref, gb_ref, gc_ref, gd_ref, w_ref, bias_ref, o_ref):
    acc = None
    for b, (o_b, g_b) in enumerate(((oa_ref, ga_ref), (ob_ref, gb_ref), (oc_ref, gc_ref), (od_ref, gd_ref))):
        gate = jax.nn.sigmoid(g_b[...] + bias_ref[b:b + 1, :])
        contrib = gate * jnp.dot(o_b[...], w_ref[b], preferred_element_type=F32)
        acc = contrib if acc is None else acc + contrib
    o_ref[...] = acc.astype(o_ref.dtype)


def _up(outs, p, w_up, bias):
    L = p.shape[0]
    tm = min(512, L)
    tn = 512
    nb = D_MODEL // tn
    o_spec = pl.BlockSpec((tm, W), lambda i, j: (i, 0))
    g_spec = lambda b: pl.BlockSpec((tm, tn), lambda i, j, b=b: (i, (COL_GATE * W + b * D_MODEL) // tn + j))
    return pl.pallas_call(
        _up_kernel,
        out_shape=jax.ShapeDtypeStruct((L, D_MODEL), BF16),
        grid=(L // tm, nb),
        in_specs=[o_spec] * 4 + [g_spec(b) for b in range(N_BRANCH)]
                 + [pl.BlockSpec((N_BRANCH, W, tn), lambda i, j: (0, 0, j)),
                    pl.BlockSpec((N_BRANCH, tn), lambda i, j: (0, j))],
        out_specs=pl.BlockSpec((tm, tn), lambda i, j: (i, j)),
        compiler_params=_cparams(("parallel", "arbitrary")),
    )(*outs, p, p, p, p, w_up, bias)


def _out_kernel(m_ref, w_ref, g_ref, res_ref, o_ref):
    y = jnp.dot(m_ref[...], w_ref[...], preferred_element_type=F32)
    o_ref[...] = res_ref[...] + (y * lax.rsqrt(jnp.mean(y * y, axis=-1, keepdims=True) + NORM_EPS)
                                 * g_ref[...])


def _out(merged, w_out, g, res):
    L = res.shape[0]
    tm = min(512, L)
    return pl.pallas_call(
        _out_kernel,
        out_shape=jax.ShapeDtypeStruct((L, D_MODEL), F32),
        grid=(L // tm,),
        in_specs=[pl.BlockSpec((tm, D_MODEL), lambda i: (i, 0)),
                  pl.BlockSpec((D_MODEL, D_MODEL), lambda i: (0, 0)),
                  pl.BlockSpec((1, D_MODEL), lambda i: (0, 0)),
                  pl.BlockSpec((tm, D_MODEL), lambda i: (i, 0))],
        out_specs=pl.BlockSpec((tm, D_MODEL), lambda i: (i, 0)),
        compiler_params=_cparams(("parallel",)),
    )(merged, w_out, g.reshape(1, D_MODEL), res)


def _pack_w_in(w):
    offs = {}
    off = 0
    for name, s in (('qa', W), ('fa', W), ('ia', W), ('za', W), ('ub', W), ('zb', W),
                    ('rkv', 3 * W), ('wa', 2 * RWKV_LORA), ('zc', W),
                    ('xd', W), ('bcd', 2 * M2_GROUPS * M2_STATE), ('dt', M2_HEADS), ('zd', W),
                    ('gate', N_BRANCH * D_MODEL)):
        offs[name] = (off, s)
        off += s
    cut = lambda name: w[:, offs[name][0]:offs[name][0] + offs[name][1]]
    pad = jnp.zeros((D_MODEL, LANE - M2_HEADS), w.dtype)
    cols = [cut(n) for n in ('qa', 'fa', 'ia', 'za', 'ub', 'zb', 'rkv', 'zc', 'xd', 'zd', 'bcd', 'gate',
                             'wa', 'dt')] + [pad]
    return jnp.concatenate(cols, axis=1).astype(BF16)


def _row(v):
    return v.astype(F32).reshape(1, -1)


def kernel(x, norm_pre, norm_post, w_in, gate_bias, w_up, w_out, hgrn_lb_logits, hgrn_norm, s5_a_re, s5_a_im, s5_log_dt, s5_b_re, s5_b_im, s5_c_re, s5_c_im, s5_d, s5_w_glu, s5_b_glu, rwkv_mu, rwkv_w0, rwkv_w2, rwkv_a0, rwkv_a2, rwkv_k_k, rwkv_k_a, rwkv_r_k, rwkv_ln_w, rwkv_ln_b, m2_conv_w, m2_conv_b, m2_dt_bias, m2_a_log, m2_d, m2_norm):
    depth = w_in.shape[0]
    L = x.shape[1]
    lb_all = jnp.cumsum(jax.nn.softmax(hgrn_lb_logits.astype(F32), axis=0), axis=0)
    lb_all = lb_all - lb_all[0:1]

    head_of_lane = jnp.arange(LANE) // RWKV_HEAD
    bd = (head_of_lane[:, None] == head_of_lane[None, :]).astype(F32)
    expand = (jnp.arange(LANE)[:, None] == (jnp.arange(W) // M2_HEADDIM)[None, :]).astype(F32)
    zpad = jnp.zeros((RWKV_LORA, W), F32)
    hpad = jnp.zeros((1, LANE - M2_HEADS), F32)

    res = x.reshape(L, D_MODEL)
    for l in range(depth):
        h = _prenorm(res, norm_pre[l])
        p = _matmul(h, _pack_w_in(w_in[l]), MM_TN)

        lb = lb_all[l]
        o_a = _hgrn(p, _row(jnp.log(lb)), _row(jnp.log1p(-lb)), _row(1.0 - lb), _row(hgrn_norm[l]))

        tables = _s5_tables(s5_a_re[l].astype(F32), s5_a_im[l].astype(F32), s5_log_dt[l].astype(F32),
                            s5_b_re[l].astype(F32), s5_b_im[l].astype(F32),
                            s5_c_re[l].astype(F32), s5_c_im[l].astype(F32))
        o_b = _s5(p, tables, _row(s5_d[l]), s5_w_glu[l].astype(BF16), _row(s5_b_glu[l]))

        mu = rwkv_mu[l].astype(F32)
        o_c = _rwkv(p, _row(mu[:3 * W]), _row(mu[3 * W:]), _row(rwkv_w0[l]),
                    jnp.concatenate([rwkv_w2[l].astype(F32), zpad], axis=0).astype(BF16),
                    _row(rwkv_a0[l]),
                    jnp.concatenate([zpad, rwkv_a2[l].astype(F32)], axis=0).astype(BF16),
                    _row(rwkv_k_k[l]), _row(rwkv_k_a[l]), _row(rwkv_r_k[l]),
                    _row(rwkv_ln_w[l]), _row(rwkv_ln_b[l]), bd)

        o_d = _ssd(p, m2_conv_w[l].astype(F32), _row(m2_conv_b[l]),
                   jnp.concatenate([_row(m2_dt_bias[l]), hpad], axis=1),
                   jnp.concatenate([-jnp.exp(_row(m2_a_log[l])), hpad], axis=1),
                   expand, _row(jnp.repeat(m2_d[l].astype(F32), M2_HEADDIM)), _row(m2_norm[l]))

        merged = _up((o_a, o_b, o_c, o_d), p, w_up[l].astype(BF16), gate_bias[l].astype(F32))
        res = _out(merged, w_out[l].astype(BF16), norm_post[l].astype(F32), res)
    return res.reshape(x.shape).astype(x.dtype)
```

```python
import functools
import math

import jax
import jax.numpy as jnp
from jax import lax
from jax.experimental import pallas as pl
from jax.experimental.pallas import tpu as pltpu

F32 = jnp.float32
BF16 = jnp.bfloat16
HI = lax.Precision.HIGHEST

D_MODEL = 2048
W = D_MODEL // 2
N_BRANCH = 4
NORM_EPS = 1e-6

HGRN_HEADS = 8
HGRN_DK = 128
HGRN_CHUNK = 64
HGRN_SUB = 16

S5_GROUP = 16
S5_GROUPS = 64
S5_STATE = 64
S5_CHUNK = 16
S5_MAX_RE = -1e-4
S5_GBLK = 8

RWKV_HEAD = 64
RWKV_LORA = 64
RWKV_CHUNK = 64
RWKV_SUB = 16
RWKV_DECAY_SCALE = 0.606531
RWKV_LN_EPS = 64e-5

M2_HEADS = 16
M2_HEADDIM = 64
M2_GROUPS = 4
M2_STATE = 128
M2_CONV = 4
M2_CHUNK = 128
M2_NORM_EPS = 1e-5

LANE = 128
SUB = 8
VMEM_LIMIT = 56 * 1024 * 1024

COL_QA, COL_FA, COL_IA, COL_ZA = 0, 1, 2, 3
COL_UB, COL_ZB = 4, 5
COL_RC, COL_KC, COL_VC, COL_ZC = 6, 7, 8, 9
COL_XD, COL_ZD, COL_BCD = 10, 11, 12
COL_GATE = 13
N_WIDE = 21
COL_WA = N_WIDE * 8
COL_DT = N_WIDE * 8 + 1
P_COLS = N_WIDE * 1024 + 2 * LANE
MM_TN = 1280


def _cparams(sem):
    return pltpu.CompilerParams(dimension_semantics=sem, vmem_limit_bytes=VMEM_LIMIT)


def _dot(a, b):
    return jnp.dot(a.astype(BF16), b.astype(BF16), preferred_element_type=F32)


def _dot_nt(a, b):
    return lax.dot_general(a.astype(BF16), b.astype(BF16), (((1,), (1,)), ((), ())),
                           preferred_element_type=F32)


def _dot_tn(a, b):
    return lax.dot_general(a.astype(BF16), b.astype(BF16), (((0,), (0,)), ((), ())),
                           preferred_element_type=F32)


def _dot_hi(a, b):
    return jnp.dot(a, b, precision=HI, preferred_element_type=F32)


def _dot_solve(a, b):
    return _dot(a, b)


def _split(x, n):
    parts = []
    for _ in range(n):
        part = x.astype(BF16)
        parts.append(part)
        x = x - part.astype(F32)
    return parts


def _dot_sel(x, sel_tiled, n):
    return jnp.dot(jnp.concatenate(_split(x, n), axis=1), sel_tiled, preferred_element_type=F32)


def _silu(x):
    return x * jax.nn.sigmoid(x)


def _cumsum_rows(x):
    n = x.shape[0]
    row = lax.broadcasted_iota(jnp.int32, x.shape, 0)
    s = 1
    while s < n:
        x = x + jnp.where(row >= s, pltpu.roll(x, s, 0), 0.0)
        s *= 2
    return x


def _prenorm_kernel(x_ref, g_ref, o_ref):
    x = x_ref[...]
    o_ref[...] = (x * lax.rsqrt(jnp.mean(x * x, axis=-1, keepdims=True) + NORM_EPS)
                  * g_ref[...]).astype(o_ref.dtype)


def _prenorm(x, g):
    L = x.shape[0]
    tm = min(512, L)
    return pl.pallas_call(
        _prenorm_kernel,
        out_shape=jax.ShapeDtypeStruct((L, D_MODEL), BF16),
        grid=(L // tm,),
        in_specs=[pl.BlockSpec((tm, D_MODEL), lambda i: (i, 0)),
                  pl.BlockSpec((1, D_MODEL), lambda i: (0, 0))],
        out_specs=pl.BlockSpec((tm, D_MODEL), lambda i: (i, 0)),
        compiler_params=_cparams(("parallel",)),
    )(x, g.reshape(1, D_MODEL))


def _mm_kernel(a_ref, w_ref, o_ref):
    o_ref[...] = jnp.dot(a_ref[...], w_ref[...], preferred_element_type=F32).astype(o_ref.dtype)


def _matmul(a, w, tn, out_dtype=F32):
    M, K = a.shape
    N = w.shape[1]
    tm = min(1024, M)
    return pl.pallas_call(
        _mm_kernel,
        out_shape=jax.ShapeDtypeStruct((M, N), out_dtype),
        grid=(M // tm, N // tn),
        in_specs=[pl.BlockSpec((tm, K), lambda i, j: (i, 0)),
                  pl.BlockSpec((K, tn), lambda i, j: (0, j))],
        out_specs=pl.BlockSpec((tm, tn), lambda i, j: (i, j)),
        compiler_params=_cparams(("parallel", "parallel")),
    )(a, w)


def _hgrn_kernel(q_ref, f_ref, i_ref, z_ref, loglb_ref, log1mlb_ref, omlb_ref, nw_ref,
                 o_ref, st_ref, *, n_chunks):
    @pl.when(pl.program_id(1) == 0)
    def _():
        st_ref[...] = jnp.zeros_like(st_ref)

    C, SB = HGRN_CHUNK, HGRN_SUB
    loglb, log1mlb, omlb, nw = loglb_ref[...], log1mlb_ref[...], omlb_ref[...], nw_ref[...]
    row = lax.broadcasted_iota(jnp.int32, (C, HGRN_DK), 0)
    trow = lax.broadcasted_iota(jnp.int32, (SB, HGRN_DK), 0)
    col_c = lax.broadcasted_iota(jnp.int32, (SB, C), 1)

    def chunk(c, carry):
        rows = pl.ds(pl.multiple_of(c * C, C), C)
        f = f_ref[rows, :]
        log_sig = jnp.minimum(f, 0.0) - jnp.log1p(jnp.exp(-jnp.abs(f)))
        b = log1mlb + log_sig
        logf = jnp.maximum(loglb, b) + jnp.log1p(jnp.exp(-jnp.abs(loglb - b)))
        cum = _cumsum_rows(logf)
        k = omlb * jax.nn.sigmoid(-f)
        q = _silu(q_ref[rows, :]) * (HGRN_DK ** -0.5)
        v = i_ref[rows, :]
        st = st_ref[...]

        o = _dot_nt(q * jnp.exp(cum), st)

        blocks = []
        for bi in range(C // SB):
            r0 = bi * SB
            ref = cum[r0:r0 + 1, :]
            cum_b = cum[r0:r0 + SB, :]
            q_b = q[r0:r0 + SB, :]
            if bi == 0:
                sc = jnp.zeros((SB, C), F32)
            else:
                kmod = k * jnp.exp(jnp.where(row < r0, ref - cum, -jnp.inf))
                sc = _dot_nt(q_b * jnp.exp(cum_b - ref), kmod)
            for s in range(SB):
                d = jnp.where(trow >= s, cum_b - cum[r0 + s:r0 + s + 1, :], -jnp.inf)
                p = q_b * k[r0 + s:r0 + s + 1, :] * jnp.exp(d)
                sc = jnp.where(col_c == r0 + s, jnp.sum(p, axis=-1, keepdims=True), sc)
            blocks.append(sc)
        scores = jnp.concatenate(blocks, axis=0)
        o = o + _dot(scores, v)

        last = cum[C - 1:C, :]
        st_ref[...] = st * jnp.exp(last) + _dot_tn(v, k * jnp.exp(last - cum))

        o = o * lax.rsqrt(jnp.mean(o * o, axis=-1, keepdims=True) + NORM_EPS) * nw
        o_ref[rows, :] = (o * _silu(z_ref[rows, :])).astype(o_ref.dtype)
        return carry

    lax.fori_loop(0, n_chunks, chunk, 0)


def _hgrn(p, loglb, log1mlb, omlb, nw):
    L = p.shape[0]
    T = min(512, L)
    nt = L // T
    blk = lambda c: pl.BlockSpec((T, HGRN_DK), lambda h, t, c=c: (t, c * HGRN_HEADS + h))
    vec = pl.BlockSpec((1, HGRN_DK), lambda h, t: (0, h))
    return pl.pallas_call(
        functools.partial(_hgrn_kernel, n_chunks=T // HGRN_CHUNK),
        out_shape=jax.ShapeDtypeStruct((L, W), BF16),
        grid=(HGRN_HEADS, nt),
        in_specs=[blk(COL_QA), blk(COL_FA), blk(COL_IA), blk(COL_ZA), vec, vec, vec, vec],
        out_specs=pl.BlockSpec((T, HGRN_DK), lambda h, t: (t, h)),
        scratch_shapes=[pltpu.VMEM((HGRN_DK, HGRN_DK), F32)],
        compiler_params=_cparams(("arbitrary", "arbitrary")),
    )(p, p, p, p, loglb, log1mlb, omlb, nw)


def _ssd_kernel(x_ref, bc_ref, z_ref, dt_ref, cw_ref, cb_ref, dtb_ref, a_ref, e_ref, d_ref, nw_ref,
                o_ref, xb_ref, st_ref):
    C = M2_CHUNK
    CH = 2 * W

    @pl.when(pl.program_id(0) == 0)
    def _():
        xb_ref[0:SUB, :] = jnp.zeros((SUB, CH), F32)
        st_ref[...] = jnp.zeros_like(st_ref)

    xb_ref[SUB:SUB + C, 0:W] = x_ref[...]
    xb_ref[SUB:SUB + C, W:CH] = bc_ref[...]
    acc = cb_ref[...] + cw_ref[0:1, :] * xb_ref[pl.ds(SUB - 3, C), :]
    for j in range(1, M2_CONV):
        acc = acc + cw_ref[j:j + 1, :] * xb_ref[pl.ds(SUB - 3 + j, C), :]
    xb_ref[0:SUB, :] = xb_ref[C:C + SUB, :]
    xc = _silu(acc)
    xh = xc[:, 0:W]

    dt = jax.nn.softplus(dt_ref[...] + dtb_ref[...])
    da = dt * a_ref[...]
    r_i = lax.broadcasted_iota(jnp.int32, (C, C), 0)
    c_i = lax.broadcasted_iota(jnp.int32, (C, C), 1)
    causal = r_i >= c_i
    causal3 = (lax.broadcasted_iota(jnp.int32, (C, 3 * C), 0)
               >= lax.broadcasted_iota(jnp.int32, (C, 3 * C), 1) % C).astype(BF16)
    a_cs = jnp.dot(causal3, jnp.concatenate(_split(da, 3), axis=0),
                   preferred_element_type=F32)
    a_cs_t = a_cs.T
    e = e_ref[...]
    ea = jnp.exp(a_cs)
    dt_e = _dot_sel(dt, e, 2)
    ea_e = _dot_sel(ea, e, 2)
    te_e = _dot_sel(jnp.exp(a_cs[C - 1:C, :] - a_cs), e, 2)
    xd = xh * dt_e
    xd_end = xd * te_e

    GW = W // M2_GROUPS
    lane_g = lax.broadcasted_iota(jnp.int32, (C, GW), 1)
    ys = []
    for g in range(M2_GROUPS):
        gs = slice(g * GW, (g + 1) * GW)
        bg = xc[:, W + g * M2_STATE:W + (g + 1) * M2_STATE]
        cg = xc[:, W + M2_GROUPS * M2_STATE + g * M2_STATE:W + M2_GROUPS * M2_STATE + (g + 1) * M2_STATE]
        bg_t = bg.T
        cb = _dot(cg, bg_t)
        st = st_ref[g]
        y = _dot(cg, st) * ea_e[:, gs]
        xd_g = xd[:, gs]
        ms, xs = [], []
        for j in range(M2_HEADS // M2_GROUPS):
            h = g * (M2_HEADS // M2_GROUPS) + j
            seg = a_cs[:, h:h + 1] - a_cs_t[h:h + 1, :]
            ms.append(cb * jnp.exp(jnp.where(causal, seg, -jnp.inf)))
            xs.append(jnp.where(lane_g // M2_HEADDIM == j, xd_g, 0.0))
        y = y + _dot(jnp.concatenate(ms, axis=1), jnp.concatenate(xs, axis=0))
        st_ref[g] = st * ea_e[C - 1:C, gs] + _dot(bg_t, xd_end[:, gs])
        y = (y + d_ref[:, gs] * xh[:, gs]) * _silu(z_ref[:, gs])
        y = y * lax.rsqrt(jnp.mean(y * y, axis=-1, keepdims=True) + M2_NORM_EPS)
        ys.append(y)
    o_ref[...] = (jnp.concatenate(ys, axis=1) * nw_ref[...]).astype(o_ref.dtype)


def _ssd(p, cw, cb, dtb, a, d_e, nw):
    L = p.shape[0]
    e = (jnp.arange(LANE)[:, None] == (jnp.arange(W) // M2_HEADDIM)[None, :]).astype(BF16)
    e = jnp.tile(e, (2, 1))
    C = M2_CHUNK
    wide = lambda c: pl.BlockSpec((C, W), lambda t, c=c: (t, c))
    full = lambda a_: pl.BlockSpec(a_.shape, lambda t: (0,) * a_.ndim)
    return pl.pallas_call(
        _ssd_kernel,
        out_shape=jax.ShapeDtypeStruct((L, W), BF16),
        grid=(L // C,),
        in_specs=[wide(COL_XD), wide(COL_BCD), wide(COL_ZD),
                  pl.BlockSpec((C, LANE), lambda t: (t, COL_DT)),
                  full(cw), full(cb), full(dtb), full(a), full(e), full(d_e), full(nw)],
        out_specs=pl.BlockSpec((C, W), lambda t: (t, 0)),
        scratch_shapes=[pltpu.VMEM((SUB + C + SUB, 2 * W), F32),
                        pltpu.VMEM((M2_GROUPS, M2_STATE, W // M2_GROUPS), F32)],
        compiler_params=_cparams(("arbitrary",)),
    )(p, p, p, p, cw, cb, dtb, a, e, d_e, nw)


def _rwkv_kernel(r_ref, k_ref, v_ref, wa_ref, z_ref, mu_ref, muwa_ref, w0_ref, w2_ref, a0_ref, a2_ref,
                 kk_ref, ka_ref, rk_ref, lnw_ref, lnb_ref, bd_ref,
                 o_ref, xb_ref, wab_ref, st_ref):
    C = RWKV_CHUNK

    @pl.when(pl.program_id(0) == 0)
    def _():
        xb_ref[0:SUB, :] = jnp.zeros((SUB, 3 * W), F32)
        wab_ref[0:SUB, :] = jnp.zeros((SUB, LANE), F32)
        st_ref[...] = jnp.zeros_like(st_ref)

    xb_ref[SUB:SUB + C, 0:W] = r_ref[...]
    xb_ref[SUB:SUB + C, W:2 * W] = k_ref[...]
    xb_ref[SUB:SUB + C, 2 * W:3 * W] = v_ref[...]
    wab_ref[SUB:SUB + C, :] = wa_ref[...]
    cur = xb_ref[SUB:SUB + C, :]
    xs = cur + mu_ref[...] * (xb_ref[pl.ds(SUB - 1, C), :] - cur)
    wa_cur = wab_ref[SUB:SUB + C, :]
    wa = wa_cur + muwa_ref[...] * (wab_ref[pl.ds(SUB - 1, C), :] - wa_cur)
    xb_ref[0:SUB, :] = xb_ref[C:C + SUB, :]
    wab_ref[0:SUB, :] = wab_ref[C:C + SUB, :]

    logw_all = -RWKV_DECAY_SCALE * jax.nn.sigmoid(w0_ref[...] + _dot(jnp.tanh(wa), w2_ref[...]))
    iclr_all = jax.nn.sigmoid(a0_ref[...] + _dot(wa, a2_ref[...]))
    bd = bd_ref[...]

    lane = lax.broadcasted_iota(jnp.int32, (C, LANE), 1)
    lo = lane < RWKV_HEAD
    t_i = lax.broadcasted_iota(jnp.int32, (C, 2 * C), 0)
    s_i = lax.broadcasted_iota(jnp.int32, (C, 2 * C), 1) % C
    strict = t_i > s_i
    incl_uv = (lax.broadcasted_iota(jnp.int32, (C, 4 * C), 0)
               >= lax.broadcasted_iota(jnp.int32, (C, 4 * C), 1) % C)
    r2 = lax.broadcasted_iota(jnp.int32, (2 * C, 2 * C), 0)
    c2 = lax.broadcasted_iota(jnp.int32, (2 * C, 2 * C), 1)
    same_blk = (r2 // RWKV_SUB) == (c2 // RWKV_SUB)
    eye = (r2 == c2).astype(F32)

    def stack(x):
        return jnp.concatenate([jnp.where(lo, x, 0.0), jnp.where(lo, 0.0, x)], axis=0)

    pairs = range(W // LANE)
    sls = [slice(i * LANE, (i + 1) * LANE) for i in pairs]
    r = [xs[:, sl] for sl in sls]
    k = [xs[:, W + i * LANE:W + (i + 1) * LANE] for i in pairs]
    v = [xs[:, 2 * W + i * LANE:2 * W + (i + 1) * LANE] for i in pairs]
    a = [iclr_all[:, sl] for sl in sls]
    kk = [k[i] * kk_ref[:, sls[i]] for i in pairs]
    ss = [_dot_sel(kk[i] * kk[i], bd, 2) for i in pairs]
    kk = [kk[i] / jnp.maximum(jnp.sqrt(ss[i]), 1e-12) for i in pairs]
    k2 = [k[i] * (1.0 + (a[i] - 1.0) * ka_ref[:, sls[i]]) for i in pairs]
    cum = [_cumsum_rows(logw_all[:, sl]) for sl in sls]
    e_inv = [jnp.exp(-cum[i]) for i in pairs]
    ar = [jnp.concatenate([-kk[i] * jnp.exp(cum[i] - logw_all[:, sls[i]]), r[i] * jnp.exp(cum[i])],
                          axis=0).astype(BF16) for i in pairs]
    bk = [jnp.concatenate([stack(kk[i] * a[i] * e_inv[i]), stack(k2[i] * e_inv[i])],
                          axis=0).astype(BF16) for i in pairs]
    v_s = [stack(v[i]).astype(BF16) for i in pairs]
    sc = [_dot_nt(ar[i], bk[i]) for i in pairs]
    ah = [_dot_nt(ar[i], st_ref[i]) for i in pairs]
    w_ = [ah[i][0:C, :] + _dot(jnp.where(strict, sc[i][0:C, 2 * C:4 * C], 0.0), v_s[i]) for i in pairs]
    n_p = [stack(jnp.where(strict, sc[i][0:C, 0:2 * C], 0.0)) for i in pairs]
    n_d = [jnp.where(same_blk, n_p[i], 0.0) for i in pairs]
    t_d = [eye + n_d[i] for i in pairs]
    pw = n_d
    for _ in range(3):
        pw = [_dot_solve(pw[i], pw[i]) for i in pairs]
        t_d = [t_d[i] + _dot_solve(pw[i], t_d[i]) for i in pairs]
    m_ = [_dot_solve(t_d[i], n_p[i] - n_d[i]) for i in pairs]
    u_s = [_dot_solve(t_d[i], stack(w_[i])) for i in pairs]
    m2 = [_dot_solve(m_[i], m_[i]) for i in pairs]
    u_s = [u_s[i] + _dot_solve(m2[i], u_s[i]) for i in pairs]
    u_s = [u_s[i] + _dot_solve(m_[i], u_s[i]) for i in pairs]
    uv = [jnp.concatenate([u_s[i].astype(BF16), v_s[i]], axis=0) for i in pairs]
    y = [ah[i][C:2 * C, :] + _dot(jnp.where(incl_uv, sc[i][C:2 * C, :], 0.0), uv[i]) for i in pairs]
    for i in pairs:
        st_ref[i] = (st_ref[i] + _dot_tn(uv[i], bk[i])) * jnp.exp(cum[i][C - 1:C, :])
    mean = [_dot_sel(y[i], bd, 2) * (1.0 / RWKV_HEAD) for i in pairs]
    yc = [y[i] - mean[i] for i in pairs]
    var = [_dot_sel(yc[i] * yc[i], bd, 2) * (1.0 / RWKV_HEAD) for i in pairs]
    bonus = [_dot_sel(r[i] * k2[i] * rk_ref[:, sls[i]], bd, 2) * v[i] for i in pairs]
    for i in pairs:
        yn = yc[i] * lax.rsqrt(var[i] + RWKV_LN_EPS) * lnw_ref[:, sls[i]] + lnb_ref[:, sls[i]]
        o_ref[:, sls[i]] = ((yn + bonus[i]) * _silu(z_ref[:, sls[i]])).astype(o_ref.dtype)


def _rwkv(p, mu, muwa, w0, w2p, a0, a2p, k_k, k_a, r_k, ln_w, ln_b):
    L = p.shape[0]
    head_of_lane = jnp.arange(LANE) // RWKV_HEAD
    bd = jnp.tile((head_of_lane[:, None] == head_of_lane[None, :]).astype(BF16), (2, 1))
    C = RWKV_CHUNK
    wide = lambda c: pl.BlockSpec((C, W), lambda t, c=c: (t, c))
    full = lambda a_: pl.BlockSpec(a_.shape, lambda t: (0,) * a_.ndim)
    consts = (mu, muwa, w0, w2p, a0, a2p, k_k, k_a, r_k, ln_w, ln_b, bd)
    return pl.pallas_call(
        _rwkv_kernel,
        out_shape=jax.ShapeDtypeStruct((L, W), BF16),
        grid=(L // C,),
        in_specs=[wide(COL_RC), wide(COL_KC), wide(COL_VC),
                  pl.BlockSpec((C, LANE), lambda t: (t, COL_WA)), wide(COL_ZC)]
                 + [full(c) for c in consts],
        out_specs=pl.BlockSpec((C, W), lambda t: (t, 0)),
        scratch_shapes=[pltpu.VMEM((SUB + C + SUB, 3 * W), F32),
                        pltpu.VMEM((SUB + C + SUB, LANE), F32),
                        pltpu.VMEM((W // LANE, LANE, LANE), F32)],
        compiler_params=_cparams(("arbitrary",)),
    )(p, p, p, p, p, *consts)


def _s5_kernel(u_ref, tq_ref, m_ref, msw_ref, l1_ref, l2_ref, l2s_ref, y_ref, xl_ref, xls_ref, xp_ref):
    G = S5_GBLK
    n = u_ref.shape[1]
    for g in range(G):
        u = u_ref[g]
        xl_ref[:, g, :] = jnp.dot(u, m_ref[g], preferred_element_type=F32)
        xls_ref[:, g, :] = jnp.dot(u, msw_ref[g], preferred_element_type=F32)
    l1, l2, l2s = l1_ref[...], l2_ref[...], l2s_ref[...]

    def step(i, carry):
        x, xs = carry
        xp_ref[i] = x
        return (l1 * x + l2 * xs + xl_ref[i], l1 * xs + l2s * x + xls_ref[i])

    zero = jnp.zeros((G, 2 * S5_STATE), F32)
    lax.fori_loop(0, n, step, (zero, zero))
    for g in range(G):
        lhs = jnp.concatenate([u_ref[g], xp_ref[:, g, :].astype(BF16)], axis=1)
        y_ref[g] = jnp.dot(lhs, tq_ref[g], preferred_element_type=F32)


def _s5_scan(u_t, tq, m, msw, l1, l2, l2s):
    G = S5_GBLK
    n = u_t.shape[1]
    cw = S5_CHUNK * S5_GROUP
    blk3 = lambda a_: pl.BlockSpec((G,) + a_.shape[1:], lambda g: (g, 0, 0))
    blk2 = lambda a_: pl.BlockSpec((None, G, a_.shape[-1]), lambda g: (g, 0, 0))
    return pl.pallas_call(
        _s5_kernel,
        out_shape=jax.ShapeDtypeStruct((S5_GROUPS, n, cw), F32),
        grid=(S5_GROUPS // G,),
        in_specs=[blk3(u_t), blk3(tq), blk3(m), blk3(msw), blk2(l1), blk2(l2), blk2(l2s)],
        out_specs=pl.BlockSpec((G, n, cw), lambda g: (g, 0, 0)),
        scratch_shapes=[pltpu.VMEM((n, G, 2 * S5_STATE), F32)] * 3,
        compiler_params=_cparams(("parallel",)),
    )(u_t, tq, m, msw, l1, l2, l2s)


def _s5_out_kernel(y_ref, u_ref, z_ref, d_ref, wg_ref, bg_ref, o_ref):
    y = y_ref[...] + d_ref[...] * u_ref[...]
    g = jax.nn.gelu(y)
    o = g * jax.nn.sigmoid(_dot(g, wg_ref[...]) + bg_ref[...])
    o_ref[...] = (o * _silu(z_ref[...])).astype(o_ref.dtype)


def _s5_out(y, p, d, wg, bg):
    L = y.shape[0]
    tm = min(512, L)
    vec = pl.BlockSpec((1, W), lambda t: (0, 0))
    return pl.pallas_call(
        _s5_out_kernel,
        out_shape=jax.ShapeDtypeStruct((L, W), BF16),
        grid=(L // tm,),
        in_specs=[pl.BlockSpec((tm, W), lambda t: (t, 0)),
                  pl.BlockSpec((tm, W), lambda t: (t, COL_UB)),
                  pl.BlockSpec((tm, W), lambda t: (t, COL_ZB)),
                  vec, pl.BlockSpec((W, W), lambda t: (0, 0)), vec],
        out_specs=pl.BlockSpec((tm, W), lambda t: (t, 0)),
        compiler_params=_cparams(("parallel",)),
    )(y, p, p, d, wg, bg)


def _s5_tables(a_re, a_im, log_dt, b_re, b_im, c_re, c_im):
    J = S5_CHUNK
    lam_re = jnp.minimum(a_re, S5_MAX_RE)
    lam_im = a_im
    dt = jnp.exp(log_dt)[:, None]
    dl_re, dl_im = lam_re * dt, lam_im * dt
    mag = jnp.exp(dl_re)
    num_re = mag * jnp.cos(dl_im) - 1.0
    num_im = mag * jnp.sin(dl_im)
    den = lam_re * lam_re + lam_im * lam_im
    coef_re = (num_re * lam_re + num_im * lam_im) / den
    coef_im = (num_im * lam_re - num_re * lam_im) / den
    bb_re = coef_re[..., None] * b_re - coef_im[..., None] * b_im
    bb_im = coef_re[..., None] * b_im + coef_im[..., None] * b_re
    tau = jnp.arange(J + 1, dtype=F32)[:, None, None]
    pm = jnp.exp(tau * dl_re)
    pw_re, pw_im = pm * jnp.cos(tau * dl_im), pm * jnp.sin(tau * dl_im)
    pb_re = pw_re[..., None] * bb_re - pw_im[..., None] * bb_im
    pb_im = pw_re[..., None] * bb_im + pw_im[..., None] * bb_re
    kern = (jnp.einsum('gop,tgph->tgoh', c_re, pb_re[:J], precision=HI)
            - jnp.einsum('gop,tgph->tgoh', c_im, pb_im[:J], precision=HI))
    t_idx = jnp.arange(J)
    lag = t_idx[None, :] - t_idx[:, None]
    toe = jnp.where((lag >= 0)[:, :, None, None, None],
                    kern[jnp.clip(lag, 0, J - 1)], 0.0)
    toe = toe.transpose(2, 0, 4, 1, 3).reshape(S5_GROUPS, J * S5_GROUP, J * S5_GROUP)
    rev_re, rev_im = pb_re[J - 1 - t_idx], pb_im[J - 1 - t_idx]
    to_m = lambda x: x.transpose(1, 0, 3, 2).reshape(S5_GROUPS, J * S5_GROUP, S5_STATE)
    m = jnp.concatenate([to_m(rev_re), to_m(rev_im)], axis=-1)
    msw = jnp.concatenate([to_m(rev_im), to_m(rev_re)], axis=-1)
    q_re = (c_re[None] * pw_re[1:, :, None, :] - c_im[None] * pw_im[1:, :, None, :])
    q_im = -(c_re[None] * pw_im[1:, :, None, :] + c_im[None] * pw_re[1:, :, None, :])
    to_q = lambda x: x.transpose(1, 3, 0, 2).reshape(S5_GROUPS, S5_STATE, J * S5_GROUP)
    tq = jnp.concatenate([toe, to_q(q_re), to_q(q_im)], axis=1)
    lj_re, lj_im = pw_re[J], pw_im[J]
    l1 = jnp.concatenate([lj_re, lj_re], axis=-1)
    l2 = jnp.concatenate([-lj_im, lj_im], axis=-1)
    l2s = jnp.concatenate([lj_im, -lj_im], axis=-1)
    rs = lambda x: x.reshape(S5_GROUPS // S5_GBLK, S5_GBLK, 2 * S5_STATE)
    return tq.astype(BF16), m.astype(BF16), msw.astype(BF16), rs(l1), rs(l2), rs(l2s)


def _s5(p, tables, d, wg, bg):
    L = p.shape[0]
    n = L // S5_CHUNK
    u = p[:, COL_UB * W:(COL_UB + 1) * W]
    u_t = (u.reshape(n, S5_CHUNK, S5_GROUPS, S5_GROUP).transpose(2, 0, 1, 3)
           .reshape(S5_GROUPS, n, S5_CHUNK * S5_GROUP).astype(BF16))
    y_t = _s5_scan(u_t, *tables)
    y = (y_t.reshape(S5_GROUPS, n, S5_CHUNK, S5_GROUP).transpose(1, 2, 0, 3).reshape(L, W))
    return _s5_out(y, p, d, wg, bg)


def _up_kernel(oa_ref, ob_ref, oc_ref, od_ref, ga_ref, gb_ref, gc_ref, gd_ref, w_ref, bias_ref, o_ref):
    acc = None
    for b, (o_b, g_b) in enumerate(((oa_ref, ga_ref), (ob_ref, gb_ref), (oc_ref, gc_ref), (od_ref, gd_ref))):
        gate = jax.nn.sigmoid(g_b[...] + bias_ref[b:b + 1, :])
        contrib = gate * jnp.dot(o_b[...], w_ref[b], preferred_element_type=F32)
        acc = contrib if acc is None else acc + contrib
    o_ref[...] = acc.astype(o_ref.dtype)


def _up(outs, p, w_up, bias):
    L = p.shape[0]
    tm = min(512, L)
    tn = 512
    nb = D_MODEL // tn
    o_spec = pl.BlockSpec((tm, W), lambda i, j: (i, 0))
    g_spec = lambda b: pl.BlockSpec((tm, tn), lambda i, j, b=b: (i, (COL_GATE * W + b * D_MODEL) // tn + j))
    return pl.pallas_call(
        _up_kernel,
        out_shape=jax.ShapeDtypeStruct((L, D_MODEL), BF16),
        grid=(L // tm, nb),
        in_specs=[o_spec] * 4 + [g_spec(b) for b in range(N_BRANCH)]
                 + [pl.BlockSpec((N_BRANCH, W, tn), lambda i, j: (0, 0, j)),
                    pl.BlockSpec((N_BRANCH, tn), lambda i, j: (0, j))],
        out_specs=pl.BlockSpec((tm, tn), lambda i, j: (i, j)),
        compiler_params=_cparams(("parallel", "arbitrary")),
    )(*outs, p, p, p, p, w_up, bias)


def _out_kernel(m_ref, w_ref, g_ref, res_ref, o_ref):
    y = jnp.dot(m_ref[...], w_ref[...], preferred_element_type=F32)
    o_ref[...] = res_ref[...] + (y * lax.rsqrt(jnp.mean(y * y, axis=-1, keepdims=True) + NORM_EPS)
                                 * g_ref[...])


def _out(merged, w_out, g, res):
    L = res.shape[0]
    tm = min(512, L)
    return pl.pallas_call(
        _out_kernel,
        out_shape=jax.ShapeDtypeStruct((L, D_MODEL), F32),
        grid=(L // tm,),
        in_specs=[pl.BlockSpec((tm, D_MODEL), lambda i: (i, 0)),
                  pl.BlockSpec((D_MODEL, D_MODEL), lambda i: (0, 0)),
                  pl.BlockSpec((1, D_MODEL), lambda i: (0, 0)),
                  pl.BlockSpec((tm, D_MODEL), lambda i: (i, 0))],
        out_specs=pl.BlockSpec((tm, D_MODEL), lambda i: (i, 0)),
        compiler_params=_cparams(("parallel",)),
    )(merged, w_out, g.reshape(1, D_MODEL), res)


def _pack_w_in(w):
    offs = {}
    off = 0
    for name, s in (('qa', W), ('fa', W), ('ia', W), ('za', W), ('ub', W), ('zb', W),
                    ('rkv', 3 * W), ('wa', 2 * RWKV_LORA), ('zc', W),
                    ('xd', W), ('bcd', 2 * M2_GROUPS * M2_STATE), ('dt', M2_HEADS), ('zd', W),
                    ('gate', N_BRANCH * D_MODEL)):
        offs[name] = (off, s)
        off += s
    cut = lambda name: w[:, offs[name][0]:offs[name][0] + offs[name][1]]
    pad = jnp.zeros((D_MODEL, LANE - M2_HEADS), w.dtype)
    cols = [cut(n) for n in ('qa', 'fa', 'ia', 'za', 'ub', 'zb', 'rkv', 'zc', 'xd', 'zd', 'bcd', 'gate',
                             'wa', 'dt')] + [pad]
    return jnp.concatenate(cols, axis=1).astype(BF16)


def _row(v):
    return v.astype(F32).reshape(1, -1)


def kernel(x, norm_pre, norm_post, w_in, gate_bias, w_up, w_out, hgrn_lb_logits, hgrn_norm, s5_a_re, s5_a_im, s5_log_dt, s5_b_re, s5_b_im, s5_c_re, s5_c_im, s5_d, s5_w_glu, s5_b_glu, rwkv_mu, rwkv_w0, rwkv_w2, rwkv_a0, rwkv_a2, rwkv_k_k, rwkv_k_a, rwkv_r_k, rwkv_ln_w, rwkv_ln_b, m2_conv_w, m2_conv_b, m2_dt_bias, m2_a_log, m2_d, m2_norm):
    depth = w_in.shape[0]
    L = x.shape[1]
    lb_all = jnp.cumsum(jax.nn.softmax(hgrn_lb_logits.astype(F32), axis=0), axis=0)
    lb_all = lb_all - lb_all[0:1]

    zpad = jnp.zeros((RWKV_LORA, W), F32)
    hpad = jnp.zeros((1, LANE - M2_HEADS), F32)

    res = x.reshape(L, D_MODEL)
    for l in range(depth):
        h = _prenorm(res, norm_pre[l])
        p = _matmul(h, _pack_w_in(w_in[l]), MM_TN)

        lb = lb_all[l]
        o_a = _hgrn(p, _row(jnp.log(lb)), _row(jnp.log1p(-lb)), _row(1.0 - lb), _row(hgrn_norm[l]))

        tables = _s5_tables(s5_a_re[l].astype(F32), s5_a_im[l].astype(F32), s5_log_dt[l].astype(F32),
                            s5_b_re[l].astype(F32), s5_b_im[l].astype(F32),
                            s5_c_re[l].astype(F32), s5_c_im[l].astype(F32))
        o_b = _s5(p, tables, _row(s5_d[l]), s5_w_glu[l].astype(BF16), _row(s5_b_glu[l]))

        mu = rwkv_mu[l].astype(F32)
        o_c = _rwkv(p, _row(mu[:3 * W]), _row(mu[3 * W:]), _row(rwkv_w0[l]),
                    jnp.concatenate([rwkv_w2[l].astype(F32), zpad], axis=0).astype(BF16),
                    _row(rwkv_a0[l]),
                    jnp.concatenate([zpad, rwkv_a2[l].astype(F32)], axis=0).astype(BF16),
                    _row(rwkv_k_k[l]), _row(rwkv_k_a[l]), _row(rwkv_r_k[l]),
                    _row(rwkv_ln_w[l]), _row(rwkv_ln_b[l]))

        o_d = _ssd(p, m2_conv_w[l].astype(F32), _row(m2_conv_b[l]),
                   jnp.concatenate([_row(m2_dt_bias[l]), hpad], axis=1),
                   jnp.concatenate([-jnp.exp(_row(m2_a_log[l])), hpad], axis=1),
                   _row(jnp.repeat(m2_d[l].astype(F32), M2_HEADDIM)), _row(m2_norm[l]))

        merged = _up((o_a, o_b, o_c, o_d), p, w_up[l].astype(BF16), gate_bias[l].astype(F32))
        res = _out(merged, w_out[l].astype(BF16), norm_post[l].astype(F32), res)
    return res.reshape(x.shape).astype(x.dtype)
```

```python
import functools
import math

import jax
import jax.numpy as jnp
from jax import lax
from jax.experimental import pallas as pl
from jax.experimental.pallas import tpu as pltpu

F32 = jnp.float32
BF16 = jnp.bfloat16
HI = lax.Precision.HIGHEST

D_MODEL = 2048
W = D_MODEL // 2
N_BRANCH = 4
NORM_EPS = 1e-6

HGRN_HEADS = 8
HGRN_DK = 128
HGRN_CHUNK = 64
HGRN_SUB = 8
HGRN_HPS = 4

S5_GROUP = 16
S5_GROUPS = 64
S5_STATE = 64
S5_CHUNK = 16
S5_MAX_RE = -1e-4
S5_GBLK = 8

RWKV_HEAD = 64
RWKV_LORA = 64
RWKV_CHUNK = 64
RWKV_SUB = 16
RWKV_DECAY_SCALE = 0.606531
RWKV_LN_EPS = 64e-5

M2_HEADS = 16
M2_HEADDIM = 64
M2_GROUPS = 4
M2_STATE = 128
M2_CONV = 4
M2_CHUNK = 128
M2_NORM_EPS = 1e-5

LANE = 128
SUB = 8
VMEM_LIMIT = 56 * 1024 * 1024

COL_QA, COL_FA, COL_IA, COL_ZA = 0, 1, 2, 3
COL_UB, COL_ZB = 4, 5
COL_RC, COL_KC, COL_VC, COL_ZC = 6, 7, 8, 9
COL_XD, COL_ZD, COL_BCD = 10, 11, 12
COL_GATE = 13
N_WIDE = 21
COL_WA = N_WIDE * 8
COL_DT = N_WIDE * 8 + 1
P_COLS = (N_WIDE + 1) * 1024
MM_TN = 2048


def _cparams(sem):
    return pltpu.CompilerParams(dimension_semantics=sem, vmem_limit_bytes=VMEM_LIMIT)


def _dot(a, b):
    return jnp.dot(a.astype(BF16), b.astype(BF16), preferred_element_type=F32)


def _dot_nt(a, b):
    return lax.dot_general(a.astype(BF16), b.astype(BF16), (((1,), (1,)), ((), ())),
                           preferred_element_type=F32)


def _dot_tn(a, b):
    return lax.dot_general(a.astype(BF16), b.astype(BF16), (((0,), (0,)), ((), ())),
                           preferred_element_type=F32)


def _dot_hi(a, b):
    return jnp.dot(a, b, precision=HI, preferred_element_type=F32)


def _dot_solve(a, b):
    return _dot(a, b)


def _split(x, n):
    parts = []
    for _ in range(n):
        part = x.astype(BF16)
        parts.append(part)
        x = x - part.astype(F32)
    return parts


def _dot_sel(x, sel_tiled, n):
    return jnp.dot(jnp.concatenate(_split(x, n), axis=1), sel_tiled, preferred_element_type=F32)


def _silu(x):
    return x * jax.nn.sigmoid(x)


def _cumsum_rows(x):
    n = x.shape[0]
    row = lax.broadcasted_iota(jnp.int32, x.shape, 0)
    s = 1
    while s < n:
        x = x + jnp.where(row >= s, pltpu.roll(x, s, 0), 0.0)
        s *= 2
    return x


def _prenorm_kernel(x_ref, g_ref, o_ref):
    x = x_ref[...]
    o_ref[...] = (x * lax.rsqrt(jnp.mean(x * x, axis=-1, keepdims=True) + NORM_EPS)
                  * g_ref[...]).astype(o_ref.dtype)


def _prenorm(x, g):
    L = x.shape[0]
    tm = min(512, L)
    return pl.pallas_call(
        _prenorm_kernel,
        out_shape=jax.ShapeDtypeStruct((L, D_MODEL), BF16),
        grid=(L // tm,),
        in_specs=[pl.BlockSpec((tm, D_MODEL), lambda i: (i, 0)),
                  pl.BlockSpec((1, D_MODEL), lambda i: (0, 0))],
        out_specs=pl.BlockSpec((tm, D_MODEL), lambda i: (i, 0)),
        compiler_params=_cparams(("parallel",)),
    )(x, g.reshape(1, D_MODEL))


def _mm_kernel(a_ref, w_ref, o_ref):
    o_ref[...] = jnp.dot(a_ref[...], w_ref[...], preferred_element_type=F32).astype(o_ref.dtype)


def _matmul(a, w, tn, out_dtype=F32):
    M, K = a.shape
    N = w.shape[1]
    tm = min(1024, M)
    return pl.pallas_call(
        _mm_kernel,
        out_shape=jax.ShapeDtypeStruct((M, N), out_dtype),
        grid=(M // tm, N // tn),
        in_specs=[pl.BlockSpec((tm, K), lambda i, j: (i, 0)),
                  pl.BlockSpec((K, tn), lambda i, j: (0, j))],
        out_specs=pl.BlockSpec((tm, tn), lambda i, j: (i, j)),
        compiler_params=_cparams(("parallel", "parallel")),
    )(a, w)


def _hgrn_kernel(q_ref, f_ref, i_ref, z_ref, loglb_ref, log1mlb_ref, omlb_ref, nw_ref,
                 o_ref, st_ref, *, n_chunks):
    @pl.when(pl.program_id(1) == 0)
    def _():
        st_ref[...] = jnp.zeros_like(st_ref)

    C, SB, DK = HGRN_CHUNK, HGRN_SUB, HGRN_DK
    NB = C // SB
    heads = range(HGRN_HPS)
    hs = [slice(j * DK, (j + 1) * DK) for j in heads]
    loglb, log1mlb, omlb, nw = loglb_ref[...], log1mlb_ref[...], omlb_ref[...], nw_ref[...]
    trow = lax.broadcasted_iota(jnp.int32, (SB, DK), 0)
    col_c = lax.broadcasted_iota(jnp.int32, (SB, C), 1)
    ones = jnp.ones((DK, DK), BF16)

    def chunk(c, carry):
        rows = pl.ds(pl.multiple_of(c * C, C), C)
        f = f_ref[rows, :]
        log_sig = jnp.minimum(f, 0.0) - jnp.log1p(jnp.exp(-jnp.abs(f)))
        b = log1mlb + log_sig
        logf = jnp.maximum(loglb, b) + jnp.log1p(jnp.exp(-jnp.abs(loglb - b)))
        cum_all = _cumsum_rows(logf)
        k_all = omlb * jax.nn.sigmoid(-f)
        q_all = _silu(q_ref[rows, :]) * (DK ** -0.5)
        cum = [cum_all[:, s_] for s_ in hs]
        k = [k_all[:, s_] for s_ in hs]
        q = [q_all[:, s_] for s_ in hs]
        v = [i_ref[rows, s_].astype(BF16) for s_ in hs]
        st = [st_ref[j] for j in heads]
        o = [_dot_nt(q[j] * jnp.exp(cum[j]), st[j]) for j in heads]

        sc = [[None] * NB for _ in heads]
        for bi in range(1, NB):
            r0 = bi * SB
            for j in heads:
                ref = cum[j][r0:r0 + 1, :]
                kmod = k[j][0:r0, :] * jnp.exp(ref - cum[j][0:r0, :])
                kmod = jnp.concatenate([kmod, jnp.zeros((C - r0, DK), F32)], axis=0)
                sc[j][bi] = _dot_nt(q[j][r0:r0 + SB, :] * jnp.exp(cum[j][r0:r0 + SB, :] - ref), kmod)
        psum = []
        for j in heads:
            ps = []
            for bi in range(NB):
                r0 = bi * SB
                for s in range(SB):
                    d = jnp.where(trow >= s, cum[j][r0:r0 + SB, :] - cum[j][r0 + s:r0 + s + 1, :], -jnp.inf)
                    ps.append((q[j][r0:r0 + SB, :] * k[j][r0 + s:r0 + s + 1, :] * jnp.exp(d)).astype(BF16))
            psum.append(jnp.dot(jnp.concatenate(ps, axis=0), ones, preferred_element_type=F32))
        for j in heads:
            blocks = []
            for bi in range(NB):
                acc = jnp.zeros((SB, C), F32) if bi == 0 else sc[j][bi]
                for s in range(SB):
                    r1 = (bi * SB + s) * SB
                    acc = jnp.where(col_c == bi * SB + s, psum[j][r1:r1 + SB, 0:C], acc)
                blocks.append(acc)
            o[j] = o[j] + _dot(jnp.concatenate(blocks, axis=0), v[j])

        for j in heads:
            last = cum[j][C - 1:C, :]
            st_ref[j] = st[j] * jnp.exp(last) + _dot_tn(v[j], k[j] * jnp.exp(last - cum[j]))
        for j in heads:
            oj = o[j] * lax.rsqrt(jnp.mean(o[j] * o[j], axis=-1, keepdims=True) + NORM_EPS) * nw[:, hs[j]]
            o_ref[rows, hs[j]] = (oj * _silu(z_ref[rows, hs[j]])).astype(o_ref.dtype)
        return carry

    lax.fori_loop(0, n_chunks, chunk, 0)


def _hgrn(p, loglb, log1mlb, omlb, nw):
    L = p.shape[0]
    T = min(512, L)
    nt = L // T
    gw = HGRN_HPS * HGRN_DK
    ng = W // gw
    blk = lambda c: pl.BlockSpec((T, gw), lambda h, t, c=c: (t, c * ng + h))
    vec = pl.BlockSpec((1, gw), lambda h, t: (0, h))
    return pl.pallas_call(
        functools.partial(_hgrn_kernel, n_chunks=T // HGRN_CHUNK),
        out_shape=jax.ShapeDtypeStruct((L, W), BF16),
        grid=(ng, nt),
        in_specs=[blk(COL_QA), blk(COL_FA), blk(COL_IA), blk(COL_ZA), vec, vec, vec, vec],
        out_specs=pl.BlockSpec((T, gw), lambda h, t: (t, h)),
        scratch_shapes=[pltpu.VMEM((HGRN_HPS, HGRN_DK, HGRN_DK), F32)],
        compiler_params=_cparams(("arbitrary", "arbitrary")),
    )(p, p, p, p, loglb, log1mlb, omlb, nw)


def _ssd_kernel(x_ref, bc_ref, z_ref, dt_ref, cw_ref, cb_ref, dtb_ref, a_ref, e_ref, d_ref, nw_ref,
                o_ref, xb_ref, st_ref):
    C = M2_CHUNK
    CH = 2 * W

    @pl.when(pl.program_id(0) == 0)
    def _():
        xb_ref[0:SUB, :] = jnp.zeros((SUB, CH), F32)
        st_ref[...] = jnp.zeros_like(st_ref)

    xb_ref[SUB:SUB + C, 0:W] = x_ref[...]
    xb_ref[SUB:SUB + C, W:CH] = bc_ref[...]
    acc = cb_ref[...] + cw_ref[0:1, :] * xb_ref[pl.ds(SUB - 3, C), :]
    for j in range(1, M2_CONV):
        acc = acc + cw_ref[j:j + 1, :] * xb_ref[pl.ds(SUB - 3 + j, C), :]
    xb_ref[0:SUB, :] = xb_ref[C:C + SUB, :]
    xc = _silu(acc)
    xh = xc[:, 0:W]

    dt = jax.nn.softplus(dt_ref[...] + dtb_ref[...])
    da = dt * a_ref[...]
    r_i = lax.broadcasted_iota(jnp.int32, (C, C), 0)
    c_i = lax.broadcasted_iota(jnp.int32, (C, C), 1)
    causal = r_i >= c_i
    causal3 = (lax.broadcasted_iota(jnp.int32, (C, 3 * C), 0)
               >= lax.broadcasted_iota(jnp.int32, (C, 3 * C), 1) % C).astype(BF16)
    a_cs = jnp.dot(causal3, jnp.concatenate(_split(da, 3), axis=0),
                   preferred_element_type=F32)
    a_cs_t = a_cs.T
    e = e_ref[...]
    ea = jnp.exp(a_cs)
    dt_e = _dot_sel(dt, e, 2)
    ea_e = _dot_sel(ea, e, 2)
    te_e = _dot_sel(jnp.exp(a_cs[C - 1:C, :] - a_cs), e, 2)
    xd = xh * dt_e
    xd_end = xd * te_e

    GW = W // M2_GROUPS
    lane_g = lax.broadcasted_iota(jnp.int32, (C, GW), 1)
    groups = range(M2_GROUPS)
    hpg = M2_HEADS // M2_GROUPS
    gsl = [slice(g * GW, (g + 1) * GW) for g in groups]
    c0 = W + M2_GROUPS * M2_STATE
    bg_t = [xc[:, W + g * M2_STATE:W + (g + 1) * M2_STATE].T.astype(BF16) for g in groups]
    cg = [xc[:, c0 + g * M2_STATE:c0 + (g + 1) * M2_STATE].astype(BF16) for g in groups]
    cb = [_dot(cg[g], bg_t[g]) for g in groups]
    y = [_dot(cg[g], st_ref[g]) * ea_e[:, gsl[g]] for g in groups]
    for g in groups:
        ms, xs = [], []
        for j in range(hpg):
            h = g * hpg + j
            seg = a_cs[:, h:h + 1] - a_cs_t[h:h + 1, :]
            ms.append((cb[g] * jnp.exp(jnp.where(causal, seg, -jnp.inf))).astype(BF16))
            xs.append(jnp.where(lane_g // M2_HEADDIM == j, xd[:, gsl[g]], 0.0).astype(BF16))
        y[g] = y[g] + _dot(jnp.concatenate(ms, axis=1), jnp.concatenate(xs, axis=0))
    for g in groups:
        st_ref[g] = st_ref[g] * ea_e[C - 1:C, gsl[g]] + _dot(bg_t[g], xd_end[:, gsl[g]])
    for g in groups:
        yg = (y[g] + d_ref[:, gsl[g]] * xh[:, gsl[g]]) * _silu(z_ref[:, gsl[g]])
        yg = yg * lax.rsqrt(jnp.mean(yg * yg, axis=-1, keepdims=True) + M2_NORM_EPS)
        o_ref[:, gsl[g]] = (yg * nw_ref[:, gsl[g]]).astype(o_ref.dtype)


def _ssd(p, cw, cb, dtb, a, d_e, nw):
    L = p.shape[0]
    e = (jnp.arange(LANE)[:, None] == (jnp.arange(W) // M2_HEADDIM)[None, :]).astype(BF16)
    e = jnp.tile(e, (2, 1))
    C = M2_CHUNK
    wide = lambda c: pl.BlockSpec((C, W), lambda t, c=c: (t, c))
    full = lambda a_: pl.BlockSpec(a_.shape, lambda t: (0,) * a_.ndim)
    return pl.pallas_call(
        _ssd_kernel,
        out_shape=jax.ShapeDtypeStruct((L, W), BF16),
        grid=(L // C,),
        in_specs=[wide(COL_XD), wide(COL_BCD), wide(COL_ZD),
                  pl.BlockSpec((C, LANE), lambda t: (t, COL_DT)),
                  full(cw), full(cb), full(dtb), full(a), full(e), full(d_e), full(nw)],
        out_specs=pl.BlockSpec((C, W), lambda t: (t, 0)),
        scratch_shapes=[pltpu.VMEM((SUB + C + SUB, 2 * W), F32),
                        pltpu.VMEM((M2_GROUPS, M2_STATE, W // M2_GROUPS), F32)],
        compiler_params=_cparams(("arbitrary",)),
    )(p, p, p, p, cw, cb, dtb, a, e, d_e, nw)


def _rwkv_kernel(r_ref, k_ref, v_ref, wa_ref, z_ref, mu_ref, muwa_ref, w0_ref, w2_ref, a0_ref, a2_ref,
                 kk_ref, ka_ref, rk_ref, lnw_ref, lnb_ref, bd_ref,
                 o_ref, xb_ref, wab_ref, st_ref):
    C = RWKV_CHUNK

    @pl.when(pl.program_id(0) == 0)
    def _():
        xb_ref[0:SUB, :] = jnp.zeros((SUB, 3 * W), F32)
        wab_ref[0:SUB, :] = jnp.zeros((SUB, LANE), F32)
        st_ref[...] = jnp.zeros_like(st_ref)

    xb_ref[SUB:SUB + C, 0:W] = r_ref[...]
    xb_ref[SUB:SUB + C, W:2 * W] = k_ref[...]
    xb_ref[SUB:SUB + C, 2 * W:3 * W] = v_ref[...]
    wab_ref[SUB:SUB + C, :] = wa_ref[...]
    cur = xb_ref[SUB:SUB + C, :]
    xs = cur + mu_ref[...] * (xb_ref[pl.ds(SUB - 1, C), :] - cur)
    wa_cur = wab_ref[SUB:SUB + C, :]
    wa = wa_cur + muwa_ref[...] * (wab_ref[pl.ds(SUB - 1, C), :] - wa_cur)
    xb_ref[0:SUB, :] = xb_ref[C:C + SUB, :]
    wab_ref[0:SUB, :] = wab_ref[C:C + SUB, :]

    logw_all = -RWKV_DECAY_SCALE * jax.nn.sigmoid(w0_ref[...] + _dot(jnp.tanh(wa), w2_ref[...]))
    iclr_all = jax.nn.sigmoid(a0_ref[...] + _dot(wa, a2_ref[...]))
    bd = bd_ref[...]

    lane = lax.broadcasted_iota(jnp.int32, (C, LANE), 1)
    lo = lane < RWKV_HEAD
    t_i = lax.broadcasted_iota(jnp.int32, (C, 2 * C), 0)
    s_i = lax.broadcasted_iota(jnp.int32, (C, 2 * C), 1) % C
    strict = t_i > s_i
    incl_uv = (lax.broadcasted_iota(jnp.int32, (C, 4 * C), 0)
               >= lax.broadcasted_iota(jnp.int32, (C, 4 * C), 1) % C)
    r2 = lax.broadcasted_iota(jnp.int32, (2 * C, 2 * C), 0)
    c2 = lax.broadcasted_iota(jnp.int32, (2 * C, 2 * C), 1)
    same_blk = (r2 // RWKV_SUB) == (c2 // RWKV_SUB)
    eye = (r2 == c2).astype(F32)

    def stack(x):
        return jnp.concatenate([jnp.where(lo, x, 0.0), jnp.where(lo, 0.0, x)], axis=0)

    pairs = range(W // LANE)
    sls = [slice(i * LANE, (i + 1) * LANE) for i in pairs]
    r = [xs[:, sl] for sl in sls]
    k = [xs[:, W + i * LANE:W + (i + 1) * LANE] for i in pairs]
    v = [xs[:, 2 * W + i * LANE:2 * W + (i + 1) * LANE] for i in pairs]
    a = [iclr_all[:, sl] for sl in sls]
    kk = [k[i] * kk_ref[:, sls[i]] for i in pairs]
    ss = [_dot_sel(kk[i] * kk[i], bd, 2) for i in pairs]
    kk = [kk[i] / jnp.maximum(jnp.sqrt(ss[i]), 1e-12) for i in pairs]
    k2 = [k[i] * (1.0 + (a[i] - 1.0) * ka_ref[:, sls[i]]) for i in pairs]
    cum = [_cumsum_rows(logw_all[:, sl]) for sl in sls]
    e_inv = [jnp.exp(-cum[i]) for i in pairs]
    ar = [jnp.concatenate([-kk[i] * jnp.exp(cum[i] - logw_all[:, sls[i]]), r[i] * jnp.exp(cum[i])],
                          axis=0).astype(BF16) for i in pairs]
    bk = [jnp.concatenate([stack(kk[i] * a[i] * e_inv[i]), stack(k2[i] * e_inv[i])],
                          axis=0).astype(BF16) for i in pairs]
    v_s = [stack(v[i]).astype(BF16) for i in pairs]
    sc = [_dot_nt(ar[i], bk[i]) for i in pairs]
    ah = [_dot_nt(ar[i], st_ref[i]) for i in pairs]
    w_ = [ah[i][0:C, :] + _dot(jnp.where(strict, sc[i][0:C, 2 * C:4 * C], 0.0), v_s[i]) for i in pairs]
    n_p = [stack(jnp.where(strict, sc[i][0:C, 0:2 * C], 0.0)) for i in pairs]
    n_d = [jnp.where(same_blk, n_p[i], 0.0) for i in pairs]
    t_d = [eye + n_d[i] for i in pairs]
    pw = n_d
    for _ in range(3):
        pw = [_dot_solve(pw[i], pw[i]) for i in pairs]
        t_d = [t_d[i] + _dot_solve(pw[i], t_d[i]) for i in pairs]
    m_ = [_dot_solve(t_d[i], n_p[i] - n_d[i]) for i in pairs]
    u_s = [_dot_solve(t_d[i], stack(w_[i])) for i in pairs]
    m2 = [_dot_solve(m_[i], m_[i]) for i in pairs]
    u_s = [u_s[i] + _dot_solve(m2[i], u_s[i]) for i in pairs]
    u_s = [u_s[i] + _dot_solve(m_[i], u_s[i]) for i in pairs]
    uv = [jnp.concatenate([u_s[i].astype(BF16), v_s[i]], axis=0) for i in pairs]
    y = [ah[i][C:2 * C, :] + _dot(jnp.where(incl_uv, sc[i][C:2 * C, :], 0.0), uv[i]) for i in pairs]
    for i in pairs:
        st_ref[i] = (st_ref[i] + _dot_tn(uv[i], bk[i])) * jnp.exp(cum[i][C - 1:C, :])
    mean = [_dot_sel(y[i], bd, 2) * (1.0 / RWKV_HEAD) for i in pairs]
    yc = [y[i] - mean[i] for i in pairs]
    var = [_dot_sel(yc[i] * yc[i], bd, 2) * (1.0 / RWKV_HEAD) for i in pairs]
    bonus = [_dot_sel(r[i] * k2[i] * rk_ref[:, sls[i]], bd, 2) * v[i] for i in pairs]
    for i in pairs:
        yn = yc[i] * lax.rsqrt(var[i] + RWKV_LN_EPS) * lnw_ref[:, sls[i]] + lnb_ref[:, sls[i]]
        o_ref[:, sls[i]] = ((yn + bonus[i]) * _silu(z_ref[:, sls[i]])).astype(o_ref.dtype)


def _rwkv(p, mu, muwa, w0, w2p, a0, a2p, k_k, k_a, r_k, ln_w, ln_b):
    L = p.shape[0]
    head_of_lane = jnp.arange(LANE) // RWKV_HEAD
    bd = jnp.tile((head_of_lane[:, None] == head_of_lane[None, :]).astype(BF16), (2, 1))
    C = RWKV_CHUNK
    wide = lambda c: pl.BlockSpec((C, W), lambda t, c=c: (t, c))
    full = lambda a_: pl.BlockSpec(a_.shape, lambda t: (0,) * a_.ndim)
    consts = (mu, muwa, w0, w2p, a0, a2p, k_k, k_a, r_k, ln_w, ln_b, bd)
    return pl.pallas_call(
        _rwkv_kernel,
        out_shape=jax.ShapeDtypeStruct((L, W), BF16),
        grid=(L // C,),
        in_specs=[wide(COL_RC), wide(COL_KC), wide(COL_VC),
                  pl.BlockSpec((C, LANE), lambda t: (t, COL_WA)), wide(COL_ZC)]
                 + [full(c) for c in consts],
        out_specs=pl.BlockSpec((C, W), lambda t: (t, 0)),
        scratch_shapes=[pltpu.VMEM((SUB + C + SUB, 3 * W), F32),
                        pltpu.VMEM((SUB + C + SUB, LANE), F32),
                        pltpu.VMEM((W // LANE, LANE, LANE), F32)],
        compiler_params=_cparams(("arbitrary",)),
    )(p, p, p, p, p, *consts)


def _s5_kernel(u_ref, mm_ref, tz_ref, q_ref, l1_ref, l2_ref, l2s_ref, y_ref,
               xl_ref, xls_ref, xp_ref, car_ref):
    @pl.when(pl.program_id(1) == 0)
    def _():
        car_ref[...] = jnp.zeros_like(car_ref)

    tn = u_ref.shape[0]
    lhs = jnp.concatenate([u_ref[:, j, :].astype(BF16) for j in range(S5_CHUNK)], axis=1)
    xl = jnp.dot(lhs, mm_ref[...], preferred_element_type=F32)
    xl_ref[...] = xl
    xls_ref[...] = jnp.concatenate(
        [pltpu.roll(xl[:, g * LANE:(g + 1) * LANE], S5_STATE, 1) for g in range(S5_GBLK)], axis=1)
    l1, l2, l2s = l1_ref[...], l2_ref[...], l2s_ref[...]

    def step(i, carry):
        x, xs = carry
        row = pl.ds(i, 1)
        xp_ref[row, :] = x
        return (l1 * x + l2 * xs + xl_ref[row, :], l1 * xs + l2s * x + xls_ref[row, :])

    x, xs = lax.fori_loop(0, tn, step, (car_ref[0:1, :], car_ref[1:2, :]))
    car_ref[0:1, :] = x
    car_ref[1:2, :] = xs
    y = (jnp.dot(lhs, tz_ref[...], preferred_element_type=F32)
         + jnp.dot(xp_ref[...].astype(BF16), q_ref[...], preferred_element_type=F32))
    for t in range(S5_CHUNK):
        y_ref[:, t, :] = y[:, t * LANE:(t + 1) * LANE]


def _s5_scan(p3, mm, tz, q, l1, l2, l2s):
    n = p3.shape[0]
    tn = min(256, n)
    nb = W // LANE
    sw = S5_GBLK * 2 * S5_STATE
    per_b = lambda a_: pl.BlockSpec((None,) + a_.shape[1:], lambda b, i: (b, 0, 0))
    return pl.pallas_call(
        _s5_kernel,
        out_shape=jax.ShapeDtypeStruct((n, S5_CHUNK, W), F32),
        grid=(nb, n // tn),
        in_specs=[pl.BlockSpec((tn, S5_CHUNK, LANE), lambda b, i: (i, 0, COL_UB * nb + b)),
                  per_b(mm), per_b(tz), per_b(q), per_b(l1), per_b(l2), per_b(l2s)],
        out_specs=pl.BlockSpec((tn, S5_CHUNK, LANE), lambda b, i: (i, 0, b)),
        scratch_shapes=[pltpu.VMEM((tn, sw), F32)] * 3 + [pltpu.VMEM((SUB, sw), F32)],
        compiler_params=_cparams(("arbitrary", "arbitrary")),
    )(p3, mm, tz, q, l1, l2, l2s)


def _s5_out_kernel(y_ref, u_ref, z_ref, d_ref, wg_ref, bg_ref, o_ref):
    y = y_ref[...] + d_ref[...] * u_ref[...]
    g = jax.nn.gelu(y)
    o = g * jax.nn.sigmoid(_dot(g, wg_ref[...]) + bg_ref[...])
    o_ref[...] = (o * _silu(z_ref[...])).astype(o_ref.dtype)


def _s5_out(y, p, d, wg, bg):
    L = y.shape[0]
    tm = min(512, L)
    vec = pl.BlockSpec((1, W), lambda t: (0, 0))
    return pl.pallas_call(
        _s5_out_kernel,
        out_shape=jax.ShapeDtypeStruct((L, W), BF16),
        grid=(L // tm,),
        in_specs=[pl.BlockSpec((tm, W), lambda t: (t, 0)),
                  pl.BlockSpec((tm, W), lambda t: (t, COL_UB)),
                  pl.BlockSpec((tm, W), lambda t: (t, COL_ZB)),
                  vec, pl.BlockSpec((W, W), lambda t: (0, 0)), vec],
        out_specs=pl.BlockSpec((tm, W), lambda t: (t, 0)),
        compiler_params=_cparams(("parallel",)),
    )(y, p, p, d, wg, bg)


def _s5_tables(a_re, a_im, log_dt, b_re, b_im, c_re, c_im):
    J = S5_CHUNK
    lam_re = jnp.minimum(a_re, S5_MAX_RE)
    lam_im = a_im
    dt = jnp.exp(log_dt)[:, None]
    dl_re, dl_im = lam_re * dt, lam_im * dt
    mag = jnp.exp(dl_re)
    num_re = mag * jnp.cos(dl_im) - 1.0
    num_im = mag * jnp.sin(dl_im)
    den = lam_re * lam_re + lam_im * lam_im
    coef_re = (num_re * lam_re + num_im * lam_im) / den
    coef_im = (num_im * lam_re - num_re * lam_im) / den
    bb_re = coef_re[..., None] * b_re - coef_im[..., None] * b_im
    bb_im = coef_re[..., None] * b_im + coef_im[..., None] * b_re
    tau = jnp.arange(J + 1, dtype=F32)[:, None, None]
    pm = jnp.exp(tau * dl_re)
    pw_re, pw_im = pm * jnp.cos(tau * dl_im), pm * jnp.sin(tau * dl_im)
    pb_re = pw_re[..., None] * bb_re - pw_im[..., None] * bb_im
    pb_im = pw_re[..., None] * bb_im + pw_im[..., None] * bb_re
    kern = (jnp.einsum('gop,tgph->tgoh', c_re, pb_re[:J], precision=HI)
            - jnp.einsum('gop,tgph->tgoh', c_im, pb_im[:J], precision=HI))
    t_idx = jnp.arange(J)
    lag = t_idx[None, :] - t_idx[:, None]
    toe = jnp.where((lag >= 0)[:, :, None, None, None],
                    kern[jnp.clip(lag, 0, J - 1)], 0.0)
    nb, gb = S5_GROUPS // S5_GBLK, S5_GBLK
    eye = jnp.eye(gb, dtype=F32)
    tz = jnp.einsum('stbgoh,gk->bsghtko', toe.reshape(J, J, nb, gb, S5_GROUP, S5_GROUP), eye)
    tz = tz.reshape(nb, J * gb * S5_GROUP, J * gb * S5_GROUP)
    rev_re, rev_im = pb_re[J - 1 - t_idx], pb_im[J - 1 - t_idx]
    rev = jnp.concatenate([rev_re, rev_im], axis=2)
    mm = jnp.einsum('sbgch,gk->bsghkc', rev.reshape(J, nb, gb, 2 * S5_STATE, S5_GROUP), eye)
    mm = mm.reshape(nb, J * gb * S5_GROUP, gb * 2 * S5_STATE)
    q_re = (c_re[None] * pw_re[1:, :, None, :] - c_im[None] * pw_im[1:, :, None, :])
    q_im = -(c_re[None] * pw_im[1:, :, None, :] + c_im[None] * pw_re[1:, :, None, :])
    qc = jnp.concatenate([q_re, q_im], axis=3)
    q = jnp.einsum('tbgoc,gk->bgctko', qc.reshape(J, nb, gb, S5_GROUP, 2 * S5_STATE), eye)
    q = q.reshape(nb, gb * 2 * S5_STATE, J * gb * S5_GROUP)
    lj_re, lj_im = pw_re[J], pw_im[J]
    rs = lambda parts: jnp.concatenate(parts, axis=-1).reshape(nb, 1, gb * 2 * S5_STATE)
    return (mm.astype(BF16), tz.astype(BF16), q.astype(BF16),
            rs([lj_re, lj_re]), rs([-lj_im, lj_im]), rs([lj_im, -lj_im]))


def _s5(p, tables, d, wg, bg):
    L = p.shape[0]
    y3 = _s5_scan(p.reshape(L // S5_CHUNK, S5_CHUNK, P_COLS), *tables)
    return _s5_out(y3.reshape(L, W), p, d, wg, bg)


def _up_kernel(oa_ref, ob_ref, oc_ref, od_ref, ga_ref, gb_ref, gc_ref, gd_ref, w_ref, bias_ref, o_ref):
    acc = None
    for b, (o_b, g_b) in enumerate(((oa_ref, ga_ref), (ob_ref, gb_ref), (oc_ref, gc_ref), (od_ref, gd_ref))):
        gate = jax.nn.sigmoid(g_b[...] + bias_ref[b:b + 1, :])
        contrib = gate * jnp.dot(o_b[...], w_ref[b], preferred_element_type=F32)
        acc = contrib if acc is None else acc + contrib
    o_ref[...] = acc.astype(o_ref.dtype)


def _up(outs, p, w_up, bias):
    L = p.shape[0]
    tm = min(512, L)
    tn = 512
    nb = D_MODEL // tn
    o_spec = pl.BlockSpec((tm, W), lambda i, j: (i, 0))
    g_spec = lambda b: pl.BlockSpec((tm, tn), lambda i, j, b=b: (i, (COL_GATE * W + b * D_MODEL) // tn + j))
    return pl.pallas_call(
        _up_kernel,
        out_shape=jax.ShapeDtypeStruct((L, D_MODEL), BF16),
        grid=(L // tm, nb),
        in_specs=[o_spec] * 4 + [g_spec(b) for b in range(N_BRANCH)]
                 + [pl.BlockSpec((N_BRANCH, W, tn), lambda i, j: (0, 0, j)),
                    pl.BlockSpec((N_BRANCH, tn), lambda i, j: (0, j))],
        out_specs=pl.BlockSpec((tm, tn), lambda i, j: (i, j)),
        compiler_params=_cparams(("parallel", "arbitrary")),
    )(*outs, p, p, p, p, w_up, bias)


def _out_kernel(m_ref, w_ref, g_ref, res_ref, o_ref):
    y = jnp.dot(m_ref[...], w_ref[...], preferred_element_type=F32)
    o_ref[...] = res_ref[...] + (y * lax.rsqrt(jnp.mean(y * y, axis=-1, keepdims=True) + NORM_EPS)
                                 * g_ref[...])


def _out(merged, w_out, g, res):
    L = res.shape[0]
    tm = min(512, L)
    return pl.pallas_call(
        _out_kernel,
        out_shape=jax.ShapeDtypeStruct((L, D_MODEL), F32),
        grid=(L // tm,),
        in_specs=[pl.BlockSpec((tm, D_MODEL), lambda i: (i, 0)),
                  pl.BlockSpec((D_MODEL, D_MODEL), lambda i: (0, 0)),
                  pl.BlockSpec((1, D_MODEL), lambda i: (0, 0)),
                  pl.BlockSpec((tm, D_MODEL), lambda i: (i, 0))],
        out_specs=pl.BlockSpec((tm, D_MODEL), lambda i: (i, 0)),
        compiler_params=_cparams(("parallel",)),
    )(merged, w_out, g.reshape(1, D_MODEL), res)


def _pack_w_in(w):
    offs = {}
    off = 0
    for name, s in (('qa', W), ('fa', W), ('ia', W), ('za', W), ('ub', W), ('zb', W),
                    ('rkv', 3 * W), ('wa', 2 * RWKV_LORA), ('zc', W),
                    ('xd', W), ('bcd', 2 * M2_GROUPS * M2_STATE), ('dt', M2_HEADS), ('zd', W),
                    ('gate', N_BRANCH * D_MODEL)):
        offs[name] = (off, s)
        off += s
    w = w.astype(BF16)
    cut = lambda name: w[:, offs[name][0]:offs[name][0] + offs[name][1]]
    pad = jnp.zeros((D_MODEL, P_COLS - N_WIDE * W - LANE - M2_HEADS), w.dtype)
    cols = [w[:, :offs['wa'][0]]] + [cut(n) for n in ('zc', 'xd', 'zd', 'bcd', 'gate', 'wa', 'dt')] + [pad]
    return jnp.concatenate(cols, axis=1)


def _row(v):
    return v.astype(F32).reshape(1, -1)


def kernel(x, norm_pre, norm_post, w_in, gate_bias, w_up, w_out, hgrn_lb_logits, hgrn_norm, s5_a_re, s5_a_im, s5_log_dt, s5_b_re, s5_b_im, s5_c_re, s5_c_im, s5_d, s5_w_glu, s5_b_glu, rwkv_mu, rwkv_w0, rwkv_w2, rwkv_a0, rwkv_a2, rwkv_k_k, rwkv_k_a, rwkv_r_k, rwkv_ln_w, rwkv_ln_b, m2_conv_w, m2_conv_b, m2_dt_bias, m2_a_log, m2_d, m2_norm):
    depth = w_in.shape[0]
    L = x.shape[1]
    lb_all = jnp.cumsum(jax.nn.softmax(hgrn_lb_logits.astype(F32), axis=0), axis=0)
    lb_all = lb_all - lb_all[0:1]

    zpad = jnp.zeros((RWKV_LORA, W), F32)
    hpad = jnp.zeros((1, LANE - M2_HEADS), F32)

    res = x.reshape(L, D_MODEL)
    for l in range(depth):
        h = _prenorm(res, norm_pre[l])
        p = _matmul(h, _pack_w_in(w_in[l]), MM_TN)

        lb = lb_all[l]
        o_a = _hgrn(p, _row(jnp.log(lb)), _row(jnp.log1p(-lb)), _row(1.0 - lb), _row(hgrn_norm[l]))

        tables = _s5_tables(s5_a_re[l].astype(F32), s5_a_im[l].astype(F32), s5_log_dt[l].astype(F32),
                            s5_b_re[l].astype(F32), s5_b_im[l].astype(F32),
                            s5_c_re[l].astype(F32), s5_c_im[l].astype(F32))
        o_b = _s5(p, tables, _row(s5_d[l]), s5_w_glu[l].astype(BF16), _row(s5_b_glu[l]))

        mu = rwkv_mu[l].astype(F32)
        o_c = _rwkv(p, _row(mu[:3 * W]), _row(mu[3 * W:]), _row(rwkv_w0[l]),
                    jnp.concatenate([rwkv_w2[l].astype(F32), zpad], axis=0).astype(BF16),
                    _row(rwkv_a0[l]),
                    jnp.concatenate([zpad, rwkv_a2[l].astype(F32)], axis=0).astype(BF16),
                    _row(rwkv_k_k[l]), _row(rwkv_k_a[l]), _row(rwkv_r_k[l]),
                    _row(rwkv_ln_w[l]), _row(rwkv_ln_b[l]))

        o_d = _ssd(p, m2_conv_w[l].astype(F32), _row(m2_conv_b[l]),
                   jnp.concatenate([_row(m2_dt_bias[l]), hpad], axis=1),
                   jnp.concatenate([-jnp.exp(_row(m2_a_log[l])), hpad], axis=1),
                   _row(jnp.repeat(m2_d[l].astype(F32), M2_HEADDIM)), _row(m2_norm[l]))

        merged = _up((o_a, o_b, o_c, o_d), p, w_up[l].astype(BF16), gate_bias[l].astype(F32))
        res = _out(merged, w_out[l].astype(BF16), norm_post[l].astype(F32), res)
    return res.reshape(x.shape).astype(x.dtype)
```

```python
import functools
import math

import jax
import jax.numpy as jnp
from jax import lax
from jax.experimental import pallas as pl
from jax.experimental.pallas import tpu as pltpu

F32 = jnp.float32
BF16 = jnp.bfloat16
HI = lax.Precision.HIGHEST

D_MODEL = 2048
W = D_MODEL // 2
N_BRANCH = 4
NORM_EPS = 1e-6

HGRN_HEADS = 8
HGRN_DK = 128
HGRN_CHUNK = 64
HGRN_SUB = 8
HGRN_HPS = 4

S5_GROUP = 16
S5_GROUPS = 64
S5_STATE = 64
S5_CHUNK = 16
S5_MAX_RE = -1e-4
S5_GBLK = 8

RWKV_HEAD = 64
RWKV_LORA = 64
RWKV_CHUNK = 64
RWKV_SUB = 16
RWKV_DECAY_SCALE = 0.606531
RWKV_LN_EPS = 64e-5

M2_HEADS = 16
M2_HEADDIM = 64
M2_GROUPS = 4
M2_STATE = 128
M2_CONV = 4
M2_CHUNK = 128
M2_NORM_EPS = 1e-5

LANE = 128
SUB = 8
VMEM_LIMIT = 56 * 1024 * 1024

COL_QA, COL_FA, COL_IA, COL_ZA = 0, 1, 2, 3
COL_UB, COL_ZB = 4, 5
COL_RC, COL_KC, COL_VC, COL_ZC = 6, 7, 8, 9
COL_XD, COL_ZD, COL_BCD = 10, 11, 12
COL_GATE = 13
N_WIDE = 21
COL_WA = N_WIDE * 8
COL_DT = N_WIDE * 8 + 1
P_COLS = N_WIDE * 1024 + 2 * LANE
MM_TN = 1280
U_TILE = (COL_UB * 1024) // MM_TN
U_OFF = COL_UB * 1024 - U_TILE * MM_TN


def _cparams(sem):
    return pltpu.CompilerParams(dimension_semantics=sem, vmem_limit_bytes=VMEM_LIMIT)


def _dot(a, b):
    return jnp.dot(a.astype(BF16), b.astype(BF16), preferred_element_type=F32)


def _dot_nt(a, b):
    return lax.dot_general(a.astype(BF16), b.astype(BF16), (((1,), (1,)), ((), ())),
                           preferred_element_type=F32)


def _dot_tn(a, b):
    return lax.dot_general(a.astype(BF16), b.astype(BF16), (((0,), (0,)), ((), ())),
                           preferred_element_type=F32)


def _dot_hi(a, b):
    return jnp.dot(a, b, precision=HI, preferred_element_type=F32)


def _dot_solve(a, b):
    return _dot(a, b)


def _split(x, n):
    parts = []
    for _ in range(n):
        part = x.astype(BF16)
        parts.append(part)
        x = x - part.astype(F32)
    return parts


def _dot_sel(x, sel_tiled, n):
    return jnp.dot(jnp.concatenate(_split(x, n), axis=1), sel_tiled, preferred_element_type=F32)


def _silu(x):
    return x * jax.nn.sigmoid(x)


def _cumsum_rows(x):
    n = x.shape[0]
    row = lax.broadcasted_iota(jnp.int32, x.shape, 0)
    s = 1
    while s < n:
        x = x + jnp.where(row >= s, pltpu.roll(x, s, 0), 0.0)
        s *= 2
    return x


def _prenorm_kernel(x_ref, g_ref, o_ref):
    x = x_ref[...]
    o_ref[...] = (x * lax.rsqrt(jnp.mean(x * x, axis=-1, keepdims=True) + NORM_EPS)
                  * g_ref[...]).astype(o_ref.dtype)


def _prenorm(x, g):
    L = x.shape[1]
    tm = min(512, L)
    return pl.pallas_call(
        _prenorm_kernel,
        out_shape=jax.ShapeDtypeStruct((L, D_MODEL), BF16),
        grid=(L // tm,),
        in_specs=[pl.BlockSpec((None, tm, D_MODEL), lambda i: (0, i, 0)),
                  pl.BlockSpec((1, D_MODEL), lambda i: (0, 0))],
        out_specs=pl.BlockSpec((tm, D_MODEL), lambda i: (i, 0)),
        compiler_params=_cparams(("parallel",)),
    )(x, g.reshape(1, D_MODEL))


def _inproj_kernel(a_ref, w_ref, o_ref, u3_ref):
    acc = jnp.dot(a_ref[...], w_ref[...], preferred_element_type=F32)
    o_ref[...] = acc

    @pl.when(pl.program_id(1) == U_TILE)
    def _():
        u3_ref[...] = acc[:, U_OFF:U_OFF + W].reshape(u3_ref.shape)


def _inproj(a, w):
    M, K = a.shape
    tm = min(1024, M)
    return pl.pallas_call(
        _inproj_kernel,
        out_shape=(jax.ShapeDtypeStruct((M, P_COLS), F32),
                   jax.ShapeDtypeStruct((M // S5_CHUNK, S5_CHUNK, W), F32)),
        grid=(M // tm, P_COLS // MM_TN),
        in_specs=[pl.BlockSpec((tm, K), lambda i, j: (i, 0)),
                  pl.BlockSpec((K, MM_TN), lambda i, j: (0, j))],
        out_specs=(pl.BlockSpec((tm, MM_TN), lambda i, j: (i, j)),
                   pl.BlockSpec((tm // S5_CHUNK, S5_CHUNK, W), lambda i, j: (i, 0, 0))),
        compiler_params=_cparams(("parallel", "arbitrary")),
    )(a, w)


def _hgrn_kernel(q_ref, f_ref, i_ref, z_ref, loglb_ref, log1mlb_ref, omlb_ref, nw_ref,
                 o_ref, st_ref, *, n_chunks):
    @pl.when(pl.program_id(1) == 0)
    def _():
        st_ref[...] = jnp.zeros_like(st_ref)

    C, SB, DK = HGRN_CHUNK, HGRN_SUB, HGRN_DK
    NB = C // SB
    heads = range(HGRN_HPS)
    hs = [slice(j * DK, (j + 1) * DK) for j in heads]
    loglb, log1mlb, omlb, nw = loglb_ref[...], log1mlb_ref[...], omlb_ref[...], nw_ref[...]
    trow = lax.broadcasted_iota(jnp.int32, (SB, DK), 0)
    col_c = lax.broadcasted_iota(jnp.int32, (SB, C), 1)
    ones = jnp.ones((DK, DK), BF16)

    def chunk(c, carry):
        rows = pl.ds(pl.multiple_of(c * C, C), C)
        f = f_ref[rows, :]
        log_sig = jnp.minimum(f, 0.0) - jnp.log1p(jnp.exp(-jnp.abs(f)))
        b = log1mlb + log_sig
        logf = jnp.maximum(loglb, b) + jnp.log1p(jnp.exp(-jnp.abs(loglb - b)))
        cum_all = _cumsum_rows(logf)
        k_all = omlb * jax.nn.sigmoid(-f)
        q_all = _silu(q_ref[rows, :]) * (DK ** -0.5)
        cum = [cum_all[:, s_] for s_ in hs]
        k = [k_all[:, s_] for s_ in hs]
        q = [q_all[:, s_] for s_ in hs]
        v = [i_ref[rows, s_].astype(BF16) for s_ in hs]
        st = [st_ref[j] for j in heads]
        o = [_dot_nt(q[j] * jnp.exp(cum[j]), st[j]) for j in heads]

        sc = [[None] * NB for _ in heads]
        for bi in range(1, NB):
            r0 = bi * SB
            for j in heads:
                ref = cum[j][r0:r0 + 1, :]
                kmod = k[j][0:r0, :] * jnp.exp(ref - cum[j][0:r0, :])
                kmod = jnp.concatenate([kmod, jnp.zeros((C - r0, DK), F32)], axis=0)
                sc[j][bi] = _dot_nt(q[j][r0:r0 + SB, :] * jnp.exp(cum[j][r0:r0 + SB, :] - ref), kmod)
        psum = []
        for j in heads:
            ps = []
            for bi in range(NB):
                r0 = bi * SB
                for s in range(SB):
                    d = jnp.where(trow >= s, cum[j][r0:r0 + SB, :] - cum[j][r0 + s:r0 + s + 1, :], -jnp.inf)
                    ps.append((q[j][r0:r0 + SB, :] * k[j][r0 + s:r0 + s + 1, :] * jnp.exp(d)).astype(BF16))
            psum.append(jnp.dot(jnp.concatenate(ps, axis=0), ones, preferred_element_type=F32))
        for j in heads:
            blocks = []
            for bi in range(NB):
                acc = jnp.zeros((SB, C), F32) if bi == 0 else sc[j][bi]
                for s in range(SB):
                    r1 = (bi * SB + s) * SB
                    acc = jnp.where(col_c == bi * SB + s, psum[j][r1:r1 + SB, 0:C], acc)
                blocks.append(acc)
            o[j] = o[j] + _dot(jnp.concatenate(blocks, axis=0), v[j])

        for j in heads:
            last = cum[j][C - 1:C, :]
            st_ref[j] = st[j] * jnp.exp(last) + _dot_tn(v[j], k[j] * jnp.exp(last - cum[j]))
        for j in heads:
            oj = o[j] * lax.rsqrt(jnp.mean(o[j] * o[j], axis=-1, keepdims=True) + NORM_EPS) * nw[:, hs[j]]
            o_ref[rows, hs[j]] = (oj * _silu(z_ref[rows, hs[j]])).astype(o_ref.dtype)
        return carry

    lax.fori_loop(0, n_chunks, chunk, 0)


def _hgrn(p, loglb, log1mlb, omlb, nw):
    L = p.shape[0]
    T = min(512, L)
    nt = L // T
    gw = HGRN_HPS * HGRN_DK
    ng = W // gw
    blk = lambda c: pl.BlockSpec((T, gw), lambda h, t, c=c: (t, c * ng + h))
    vec = pl.BlockSpec((1, gw), lambda h, t: (0, h))
    return pl.pallas_call(
        functools.partial(_hgrn_kernel, n_chunks=T // HGRN_CHUNK),
        out_shape=jax.ShapeDtypeStruct((L, W), BF16),
        grid=(ng, nt),
        in_specs=[blk(COL_QA), blk(COL_FA), blk(COL_IA), blk(COL_ZA), vec, vec, vec, vec],
        out_specs=pl.BlockSpec((T, gw), lambda h, t: (t, h)),
        scratch_shapes=[pltpu.VMEM((HGRN_HPS, HGRN_DK, HGRN_DK), F32)],
        compiler_params=_cparams(("arbitrary", "arbitrary")),
    )(p, p, p, p, loglb, log1mlb, omlb, nw)


def _ssd_kernel(x_ref, bc_ref, z_ref, dt_ref, cw_ref, cb_ref, dtb_ref, a_ref, e_ref, d_ref, nw_ref,
                o_ref, xb_ref, st_ref):
    C = M2_CHUNK
    CH = 2 * W

    @pl.when(pl.program_id(0) == 0)
    def _():
        xb_ref[0:SUB, :] = jnp.zeros((SUB, CH), F32)
        st_ref[...] = jnp.zeros_like(st_ref)

    xb_ref[SUB:SUB + C, 0:W] = x_ref[...]
    xb_ref[SUB:SUB + C, W:CH] = bc_ref[...]
    acc = cb_ref[...] + cw_ref[0:1, :] * xb_ref[pl.ds(SUB - 3, C), :]
    for j in range(1, M2_CONV):
        acc = acc + cw_ref[j:j + 1, :] * xb_ref[pl.ds(SUB - 3 + j, C), :]
    xb_ref[0:SUB, :] = xb_ref[C:C + SUB, :]
    xc = _silu(acc)
    xh = xc[:, 0:W]

    dt = jax.nn.softplus(dt_ref[...] + dtb_ref[...])
    da = dt * a_ref[...]
    r_i = lax.broadcasted_iota(jnp.int32, (C, C), 0)
    c_i = lax.broadcasted_iota(jnp.int32, (C, C), 1)
    causal = r_i >= c_i
    causal3 = (lax.broadcasted_iota(jnp.int32, (C, 3 * C), 0)
               >= lax.broadcasted_iota(jnp.int32, (C, 3 * C), 1) % C).astype(BF16)
    a_cs = jnp.dot(causal3, jnp.concatenate(_split(da, 3), axis=0),
                   preferred_element_type=F32)
    a_cs_t = a_cs.T
    e = e_ref[...]
    ea = jnp.exp(a_cs)
    dt_e = _dot_sel(dt, e, 2)
    ea_e = _dot_sel(ea, e, 2)
    te_e = _dot_sel(jnp.exp(a_cs[C - 1:C, :] - a_cs), e, 2)
    xd = xh * dt_e
    xd_end = xd * te_e

    GW = W // M2_GROUPS
    lane_g = lax.broadcasted_iota(jnp.int32, (C, GW), 1)
    groups = range(M2_GROUPS)
    hpg = M2_HEADS // M2_GROUPS
    gsl = [slice(g * GW, (g + 1) * GW) for g in groups]
    c0 = W + M2_GROUPS * M2_STATE
    bg_t = [xc[:, W + g * M2_STATE:W + (g + 1) * M2_STATE].T.astype(BF16) for g in groups]
    cg = [xc[:, c0 + g * M2_STATE:c0 + (g + 1) * M2_STATE].astype(BF16) for g in groups]
    cb = [_dot(cg[g], bg_t[g]) for g in groups]
    y = [_dot(cg[g], st_ref[g]) * ea_e[:, gsl[g]] for g in groups]
    for g in groups:
        ms, xs = [], []
        for j in range(hpg):
            h = g * hpg + j
            seg = a_cs[:, h:h + 1] - a_cs_t[h:h + 1, :]
            ms.append((cb[g] * jnp.exp(jnp.where(causal, seg, -jnp.inf))).astype(BF16))
            xs.append(jnp.where(lane_g // M2_HEADDIM == j, xd[:, gsl[g]], 0.0).astype(BF16))
        y[g] = y[g] + _dot(jnp.concatenate(ms, axis=1), jnp.concatenate(xs, axis=0))
    for g in groups:
        st_ref[g] = st_ref[g] * ea_e[C - 1:C, gsl[g]] + _dot(bg_t[g], xd_end[:, gsl[g]])
    for g in groups:
        yg = (y[g] + d_ref[:, gsl[g]] * xh[:, gsl[g]]) * _silu(z_ref[:, gsl[g]])
        yg = yg * lax.rsqrt(jnp.mean(yg * yg, axis=-1, keepdims=True) + M2_NORM_EPS)
        o_ref[:, gsl[g]] = (yg * nw_ref[:, gsl[g]]).astype(o_ref.dtype)


def _ssd(p, cw, cb, dtb, a, d_e, nw):
    L = p.shape[0]
    e = (jnp.arange(LANE)[:, None] == (jnp.arange(W) // M2_HEADDIM)[None, :]).astype(BF16)
    e = jnp.tile(e, (2, 1))
    C = M2_CHUNK
    wide = lambda c: pl.BlockSpec((C, W), lambda t, c=c: (t, c))
    full = lambda a_: pl.BlockSpec(a_.shape, lambda t: (0,) * a_.ndim)
    return pl.pallas_call(
        _ssd_kernel,
        out_shape=jax.ShapeDtypeStruct((L, W), BF16),
        grid=(L // C,),
        in_specs=[wide(COL_XD), wide(COL_BCD), wide(COL_ZD),
                  pl.BlockSpec((C, LANE), lambda t: (t, COL_DT)),
                  full(cw), full(cb), full(dtb), full(a), full(e), full(d_e), full(nw)],
        out_specs=pl.BlockSpec((C, W), lambda t: (t, 0)),
        scratch_shapes=[pltpu.VMEM((SUB + C + SUB, 2 * W), F32),
                        pltpu.VMEM((M2_GROUPS, M2_STATE, W // M2_GROUPS), F32)],
        compiler_params=_cparams(("arbitrary",)),
    )(p, p, p, p, cw, cb, dtb, a, e, d_e, nw)


def _rwkv_kernel(r_ref, k_ref, v_ref, wa_ref, z_ref, mu_ref, muwa_ref, w0_ref, w2_ref, a0_ref, a2_ref,
                 kk_ref, ka_ref, rk_ref, lnw_ref, lnb_ref, bd_ref,
                 o_ref, xb_ref, wab_ref, st_ref):
    C = RWKV_CHUNK

    @pl.when(pl.program_id(0) == 0)
    def _():
        xb_ref[0:SUB, :] = jnp.zeros((SUB, 3 * W), F32)
        wab_ref[0:SUB, :] = jnp.zeros((SUB, LANE), F32)
        st_ref[...] = jnp.zeros_like(st_ref)

    xb_ref[SUB:SUB + C, 0:W] = r_ref[...]
    xb_ref[SUB:SUB + C, W:2 * W] = k_ref[...]
    xb_ref[SUB:SUB + C, 2 * W:3 * W] = v_ref[...]
    wab_ref[SUB:SUB + C, :] = wa_ref[...]
    cur = xb_ref[SUB:SUB + C, :]
    xs = cur + mu_ref[...] * (xb_ref[pl.ds(SUB - 1, C), :] - cur)
    wa_cur = wab_ref[SUB:SUB + C, :]
    wa = wa_cur + muwa_ref[...] * (wab_ref[pl.ds(SUB - 1, C), :] - wa_cur)
    xb_ref[0:SUB, :] = xb_ref[C:C + SUB, :]
    wab_ref[0:SUB, :] = wab_ref[C:C + SUB, :]

    logw_all = -RWKV_DECAY_SCALE * jax.nn.sigmoid(w0_ref[...] + _dot(jnp.tanh(wa), w2_ref[...]))
    iclr_all = jax.nn.sigmoid(a0_ref[...] + _dot(wa, a2_ref[...]))
    bd = bd_ref[...]

    lane = lax.broadcasted_iota(jnp.int32, (C, LANE), 1)
    lo = lane < RWKV_HEAD
    t_i = lax.broadcasted_iota(jnp.int32, (C, 2 * C), 0)
    s_i = lax.broadcasted_iota(jnp.int32, (C, 2 * C), 1) % C
    strict = t_i > s_i
    incl_uv = (lax.broadcasted_iota(jnp.int32, (C, 4 * C), 0)
               >= lax.broadcasted_iota(jnp.int32, (C, 4 * C), 1) % C)
    r2 = lax.broadcasted_iota(jnp.int32, (2 * C, 2 * C), 0)
    c2 = lax.broadcasted_iota(jnp.int32, (2 * C, 2 * C), 1)
    same_blk = (r2 // RWKV_SUB) == (c2 // RWKV_SUB)
    eye = (r2 == c2).astype(F32)

    def stack(x):
        return jnp.concatenate([jnp.where(lo, x, 0.0), jnp.where(lo, 0.0, x)], axis=0)

    pairs = range(W // LANE)
    sls = [slice(i * LANE, (i + 1) * LANE) for i in pairs]
    r = [xs[:, sl] for sl in sls]
    k = [xs[:, W + i * LANE:W + (i + 1) * LANE] for i in pairs]
    v = [xs[:, 2 * W + i * LANE:2 * W + (i + 1) * LANE] for i in pairs]
    a = [iclr_all[:, sl] for sl in sls]
    kk = [k[i] * kk_ref[:, sls[i]] for i in pairs]
    ss = [_dot_sel(kk[i] * kk[i], bd, 2) for i in pairs]
    kk = [kk[i] / jnp.maximum(jnp.sqrt(ss[i]), 1e-12) for i in pairs]
    k2 = [k[i] * (1.0 + (a[i] - 1.0) * ka_ref[:, sls[i]]) for i in pairs]
    cum = [_cumsum_rows(logw_all[:, sl]) for sl in sls]
    e_inv = [jnp.exp(-cum[i]) for i in pairs]
    ar = [jnp.concatenate([-kk[i] * jnp.exp(cum[i] - logw_all[:, sls[i]]), r[i] * jnp.exp(cum[i])],
                          axis=0).astype(BF16) for i in pairs]
    bk = [jnp.concatenate([stack(kk[i] * a[i] * e_inv[i]), stack(k2[i] * e_inv[i])],
                          axis=0).astype(BF16) for i in pairs]
    v_s = [stack(v[i]).astype(BF16) for i in pairs]
    sc = [_dot_nt(ar[i], bk[i]) for i in pairs]
    ah = [_dot_nt(ar[i], st_ref[i]) for i in pairs]
    w_ = [ah[i][0:C, :] + _dot(jnp.where(strict, sc[i][0:C, 2 * C:4 * C], 0.0), v_s[i]) for i in pairs]
    n_p = [stack(jnp.where(strict, sc[i][0:C, 0:2 * C], 0.0)) for i in pairs]
    n_d = [jnp.where(same_blk, n_p[i], 0.0) for i in pairs]
    t_d = [eye + n_d[i] for i in pairs]
    pw = n_d
    for _ in range(3):
        pw = [_dot_solve(pw[i], pw[i]) for i in pairs]
        t_d = [t_d[i] + _dot_solve(pw[i], t_d[i]) for i in pairs]
    m_ = [_dot_solve(t_d[i], n_p[i] - n_d[i]) for i in pairs]
    u_s = [_dot_solve(t_d[i], stack(w_[i])) for i in pairs]
    m2 = [_dot_solve(m_[i], m_[i]) for i in pairs]
    u_s = [u_s[i] + _dot_solve(m2[i], u_s[i]) for i in pairs]
    u_s = [u_s[i] + _dot_solve(m_[i], u_s[i]) for i in pairs]
    uv = [jnp.concatenate([u_s[i].astype(BF16), v_s[i]], axis=0) for i in pairs]
    y = [ah[i][C:2 * C, :] + _dot(jnp.where(incl_uv, sc[i][C:2 * C, :], 0.0), uv[i]) for i in pairs]
    for i in pairs:
        st_ref[i] = (st_ref[i] + _dot_tn(uv[i], bk[i])) * jnp.exp(cum[i][C - 1:C, :])
    mean = [_dot_sel(y[i], bd, 2) * (1.0 / RWKV_HEAD) for i in pairs]
    yc = [y[i] - mean[i] for i in pairs]
    var = [_dot_sel(yc[i] * yc[i], bd, 2) * (1.0 / RWKV_HEAD) for i in pairs]
    bonus = [_dot_sel(r[i] * k2[i] * rk_ref[:, sls[i]], bd, 2) * v[i] for i in pairs]
    for i in pairs:
        yn = yc[i] * lax.rsqrt(var[i] + RWKV_LN_EPS) * lnw_ref[:, sls[i]] + lnb_ref[:, sls[i]]
        o_ref[:, sls[i]] = ((yn + bonus[i]) * _silu(z_ref[:, sls[i]])).astype(o_ref.dtype)


def _rwkv(p, mu, muwa, w0, w2p, a0, a2p, k_k, k_a, r_k, ln_w, ln_b):
    L = p.shape[0]
    head_of_lane = jnp.arange(LANE) // RWKV_HEAD
    bd = jnp.tile((head_of_lane[:, None] == head_of_lane[None, :]).astype(BF16), (2, 1))
    C = RWKV_CHUNK
    wide = lambda c: pl.BlockSpec((C, W), lambda t, c=c: (t, c))
    full = lambda a_: pl.BlockSpec(a_.shape, lambda t: (0,) * a_.ndim)
    consts = (mu, muwa, w0, w2p, a0, a2p, k_k, k_a, r_k, ln_w, ln_b, bd)
    return pl.pallas_call(
        _rwkv_kernel,
        out_shape=jax.ShapeDtypeStruct((L, W), BF16),
        grid=(L // C,),
        in_specs=[wide(COL_RC), wide(COL_KC), wide(COL_VC),
                  pl.BlockSpec((C, LANE), lambda t: (t, COL_WA)), wide(COL_ZC)]
                 + [full(c) for c in consts],
        out_specs=pl.BlockSpec((C, W), lambda t: (t, 0)),
        scratch_shapes=[pltpu.VMEM((SUB + C + SUB, 3 * W), F32),
                        pltpu.VMEM((SUB + C + SUB, LANE), F32),
                        pltpu.VMEM((W // LANE, LANE, LANE), F32)],
        compiler_params=_cparams(("arbitrary",)),
    )(p, p, p, p, p, *consts)


def _s5_kernel(u_ref, perm_ref, m_ref, toe_ref, q_ref, l1_ref, l2_ref, l2s_ref, y_ref,
               xl_ref, xls_ref, xp_ref, car_ref):
    @pl.when(pl.program_id(1) == 0)
    def _():
        car_ref[...] = jnp.zeros_like(car_ref)

    tn = u_ref.shape[0]
    gw = S5_CHUNK * S5_GROUP
    perm = perm_ref[...]
    lhs = jnp.concatenate([u_ref[:, j, :].astype(BF16) for j in range(S5_CHUNK)], axis=1)
    lhs = jnp.dot(lhs, perm, preferred_element_type=F32).astype(BF16)
    lg = [lhs[:, g * gw:(g + 1) * gw] for g in range(S5_GBLK)]
    xl = jnp.concatenate([jnp.dot(lg[g], m_ref[g], preferred_element_type=F32) for g in range(S5_GBLK)],
                         axis=1)
    xl_ref[...] = xl
    xls_ref[...] = jnp.concatenate(
        [pltpu.roll(xl[:, g * LANE:(g + 1) * LANE], S5_STATE, 1) for g in range(S5_GBLK)], axis=1)
    l1, l2, l2s = l1_ref[...], l2_ref[...], l2s_ref[...]

    def step(i, carry):
        x, xs = carry
        row = pl.ds(i, 1)
        xp_ref[row, :] = x
        return (l1 * x + l2 * xs + xl_ref[row, :], l1 * xs + l2s * x + xls_ref[row, :])

    x, xs = lax.fori_loop(0, tn, step, (car_ref[0:1, :], car_ref[1:2, :]))
    car_ref[0:1, :] = x
    car_ref[1:2, :] = xs
    xp = xp_ref[...].astype(BF16)
    y = jnp.concatenate(
        [jnp.dot(lg[g], toe_ref[g], preferred_element_type=F32)
         + jnp.dot(xp[:, g * LANE:(g + 1) * LANE], q_ref[g], preferred_element_type=F32)
         for g in range(S5_GBLK)], axis=1)
    y = lax.dot_general(y.astype(BF16), perm, (((1,), (1,)), ((), ())),
                        preferred_element_type=F32)
    for t in range(S5_CHUNK):
        y_ref[:, t, :] = y[:, t * LANE:(t + 1) * LANE]


def _s5_scan(u3, m, toe, q, l1, l2, l2s):
    n = u3.shape[0]
    tn = min(512, n)
    nb = W // LANE
    sw = S5_GBLK * 2 * S5_STATE
    kw = S5_CHUNK * LANE
    r = jnp.arange(kw)
    dst = (r // S5_GROUP % S5_GBLK) * (S5_CHUNK * S5_GROUP) + (r // LANE) * S5_GROUP + r % S5_GROUP
    perm = (dst[:, None] == jnp.arange(kw)[None, :]).astype(BF16)
    per_b = lambda a_: pl.BlockSpec((S5_GBLK,) + a_.shape[1:], lambda b, i: (b, 0, 0))
    vec = pl.BlockSpec((None, 1, sw), lambda b, i: (b, 0, 0))
    return pl.pallas_call(
        _s5_kernel,
        out_shape=jax.ShapeDtypeStruct((n, S5_CHUNK, W), F32),
        grid=(nb, n // tn),
        in_specs=[pl.BlockSpec((tn, S5_CHUNK, LANE), lambda b, i: (i, 0, b)),
                  pl.BlockSpec((kw, kw), lambda b, i: (0, 0)),
                  per_b(m), per_b(toe), per_b(q), vec, vec, vec],
        out_specs=pl.BlockSpec((tn, S5_CHUNK, LANE), lambda b, i: (i, 0, b)),
        scratch_shapes=[pltpu.VMEM((tn, sw), F32)] * 3 + [pltpu.VMEM((SUB, sw), F32)],
        compiler_params=_cparams(("arbitrary", "arbitrary")),
    )(u3, perm, m, toe, q, l1, l2, l2s)


def _s5_out_kernel(y_ref, u_ref, z_ref, d_ref, wg_ref, bg_ref, o_ref):
    y = y_ref[...].reshape(u_ref.shape) + d_ref[...] * u_ref[...]
    g = jax.nn.gelu(y)
    o = g * jax.nn.sigmoid(_dot(g, wg_ref[...]) + bg_ref[...])
    o_ref[...] = (o * _silu(z_ref[...])).astype(o_ref.dtype)


def _s5_out(y3, p, d, wg, bg):
    L = p.shape[0]
    tm = min(512, L)
    vec = pl.BlockSpec((1, W), lambda t: (0, 0))
    return pl.pallas_call(
        _s5_out_kernel,
        out_shape=jax.ShapeDtypeStruct((L, W), BF16),
        grid=(L // tm,),
        in_specs=[pl.BlockSpec((tm // S5_CHUNK, S5_CHUNK, W), lambda t: (t, 0, 0)),
                  pl.BlockSpec((tm, W), lambda t: (t, COL_UB)),
                  pl.BlockSpec((tm, W), lambda t: (t, COL_ZB)),
                  vec, pl.BlockSpec((W, W), lambda t: (0, 0)), vec],
        out_specs=pl.BlockSpec((tm, W), lambda t: (t, 0)),
        compiler_params=_cparams(("parallel",)),
    )(y3, p, p, d, wg, bg)


def _s5_tables(a_re, a_im, log_dt, b_re, b_im, c_re, c_im):
    J = S5_CHUNK
    lam_re = jnp.minimum(a_re, S5_MAX_RE)
    lam_im = a_im
    dt = jnp.exp(log_dt)[:, None]
    dl_re, dl_im = lam_re * dt, lam_im * dt
    mag = jnp.exp(dl_re)
    num_re = mag * jnp.cos(dl_im) - 1.0
    num_im = mag * jnp.sin(dl_im)
    den = lam_re * lam_re + lam_im * lam_im
    coef_re = (num_re * lam_re + num_im * lam_im) / den
    coef_im = (num_im * lam_re - num_re * lam_im) / den
    bb_re = coef_re[..., None] * b_re - coef_im[..., None] * b_im
    bb_im = coef_re[..., None] * b_im + coef_im[..., None] * b_re
    tau = jnp.arange(J + 1, dtype=F32)[:, None, None]
    pm = jnp.exp(tau * dl_re)
    pw_re, pw_im = pm * jnp.cos(tau * dl_im), pm * jnp.sin(tau * dl_im)
    pb_re = pw_re[..., None] * bb_re - pw_im[..., None] * bb_im
    pb_im = pw_re[..., None] * bb_im + pw_im[..., None] * bb_re
    kern = (jnp.einsum('gop,tgph->tgoh', c_re, pb_re[:J], precision=HI)
            - jnp.einsum('gop,tgph->tgoh', c_im, pb_im[:J], precision=HI))
    t_idx = jnp.arange(J)
    lag = t_idx[None, :] - t_idx[:, None]
    toe = jnp.where((lag >= 0)[:, :, None, None, None],
                    kern[jnp.clip(lag, 0, J - 1)], 0.0)
    nb, gb = S5_GROUPS // S5_GBLK, S5_GBLK
    toe = toe.transpose(2, 0, 4, 1, 3).reshape(S5_GROUPS, J * S5_GROUP, J * S5_GROUP)
    rev_re, rev_im = pb_re[J - 1 - t_idx], pb_im[J - 1 - t_idx]
    to_m = lambda x: x.transpose(1, 0, 3, 2).reshape(S5_GROUPS, J * S5_GROUP, S5_STATE)
    m = jnp.concatenate([to_m(rev_re), to_m(rev_im)], axis=-1)
    q_re = (c_re[None] * pw_re[1:, :, None, :] - c_im[None] * pw_im[1:, :, None, :])
    q_im = -(c_re[None] * pw_im[1:, :, None, :] + c_im[None] * pw_re[1:, :, None, :])
    to_q = lambda x: x.transpose(1, 3, 0, 2).reshape(S5_GROUPS, S5_STATE, J * S5_GROUP)
    q = jnp.concatenate([to_q(q_re), to_q(q_im)], axis=1)
    lj_re, lj_im = pw_re[J], pw_im[J]
    rs = lambda parts: jnp.concatenate(parts, axis=-1).reshape(nb, 1, gb * 2 * S5_STATE)
    return (m.astype(BF16), toe.astype(BF16), q.astype(BF16),
            rs([lj_re, lj_re]), rs([-lj_im, lj_im]), rs([lj_im, -lj_im]))


def _s5(u3, p, tables, d, wg, bg):
    return _s5_out(_s5_scan(u3, *tables), p, d, wg, bg)


def _up_kernel(oa_ref, ob_ref, oc_ref, od_ref, ga_ref, gb_ref, gc_ref, gd_ref, w_ref, bias_ref, o_ref):
    acc = None
    for b, (o_b, g_b) in enumerate(((oa_ref, ga_ref), (ob_ref, gb_ref), (oc_ref, gc_ref), (od_ref, gd_ref))):
        gate = jax.nn.sigmoid(g_b[...] + bias_ref[b:b + 1, :])
        contrib = gate * jnp.dot(o_b[...], w_ref[b], preferred_element_type=F32)
        acc = contrib if acc is None else acc + contrib
    o_ref[...] = acc.astype(o_ref.dtype)


def _up(outs, p, w_up, bias):
    L = p.shape[0]
    tm = min(1024, L)
    tn = 512
    nb = D_MODEL // tn
    o_spec = pl.BlockSpec((tm, W), lambda i, j: (i, 0))
    g_spec = lambda b: pl.BlockSpec((tm, tn), lambda i, j, b=b: (i, (COL_GATE * W + b * D_MODEL) // tn + j))
    return pl.pallas_call(
        _up_kernel,
        out_shape=jax.ShapeDtypeStruct((L, D_MODEL), BF16),
        grid=(L // tm, nb),
        in_specs=[o_spec] * 4 + [g_spec(b) for b in range(N_BRANCH)]
                 + [pl.BlockSpec((N_BRANCH, W, tn), lambda i, j: (0, 0, j)),
                    pl.BlockSpec((N_BRANCH, tn), lambda i, j: (0, j))],
        out_specs=pl.BlockSpec((tm, tn), lambda i, j: (i, j)),
        compiler_params=_cparams(("parallel", "arbitrary")),
    )(*outs, p, p, p, p, w_up, bias)


def _out_kernel(m_ref, w_ref, g_ref, res_ref, o_ref):
    y = jnp.dot(m_ref[...], w_ref[...], preferred_element_type=F32)
    o_ref[...] = res_ref[...] + (y * lax.rsqrt(jnp.mean(y * y, axis=-1, keepdims=True) + NORM_EPS)
                                 * g_ref[...])


def _out(merged, w_out, g, res):
    L = res.shape[1]
    tm = min(512, L)
    return pl.pallas_call(
        _out_kernel,
        out_shape=jax.ShapeDtypeStruct((1, L, D_MODEL), F32),
        grid=(L // tm,),
        in_specs=[pl.BlockSpec((tm, D_MODEL), lambda i: (i, 0)),
                  pl.BlockSpec((D_MODEL, D_MODEL), lambda i: (0, 0)),
                  pl.BlockSpec((1, D_MODEL), lambda i: (0, 0)),
                  pl.BlockSpec((None, tm, D_MODEL), lambda i: (0, i, 0))],
        out_specs=pl.BlockSpec((None, tm, D_MODEL), lambda i: (0, i, 0)),
        compiler_params=_cparams(("parallel",)),
    )(merged, w_out, g.reshape(1, D_MODEL), res)


def _pack_w_in(w):
    offs = {}
    off = 0
    for name, s in (('qa', W), ('fa', W), ('ia', W), ('za', W), ('ub', W), ('zb', W),
                    ('rkv', 3 * W), ('wa', 2 * RWKV_LORA), ('zc', W),
                    ('xd', W), ('bcd', 2 * M2_GROUPS * M2_STATE), ('dt', M2_HEADS), ('zd', W),
                    ('gate', N_BRANCH * D_MODEL)):
        offs[name] = (off, s)
        off += s
    w = w.astype(BF16)
    cut = lambda name: w[:, offs[name][0]:offs[name][0] + offs[name][1]]
    pad = jnp.zeros((D_MODEL, P_COLS - N_WIDE * W - LANE - M2_HEADS), w.dtype)
    cols = [w[:, :offs['wa'][0]]] + [cut(n) for n in ('zc', 'xd', 'zd', 'bcd', 'gate', 'wa', 'dt')] + [pad]
    return jnp.concatenate(cols, axis=1)


def _row(v):
    return v.astype(F32).reshape(1, -1)


def kernel(x, norm_pre, norm_post, w_in, gate_bias, w_up, w_out, hgrn_lb_logits, hgrn_norm, s5_a_re, s5_a_im, s5_log_dt, s5_b_re, s5_b_im, s5_c_re, s5_c_im, s5_d, s5_w_glu, s5_b_glu, rwkv_mu, rwkv_w0, rwkv_w2, rwkv_a0, rwkv_a2, rwkv_k_k, rwkv_k_a, rwkv_r_k, rwkv_ln_w, rwkv_ln_b, m2_conv_w, m2_conv_b, m2_dt_bias, m2_a_log, m2_d, m2_norm):
    depth = w_in.shape[0]
    L = x.shape[1]
    lb_all = jnp.cumsum(jax.nn.softmax(hgrn_lb_logits.astype(F32), axis=0), axis=0)
    lb_all = lb_all - lb_all[0:1]

    zpad = jnp.zeros((RWKV_LORA, W), F32)
    hpad = jnp.zeros((1, LANE - M2_HEADS), F32)

    res = x.astype(F32)
    for l in range(depth):
        h = _prenorm(res, norm_pre[l])
        p, u3 = _inproj(h, _pack_w_in(w_in[l]))

        lb = lb_all[l]
        o_a = _hgrn(p, _row(jnp.log(lb)), _row(jnp.log1p(-lb)), _row(1.0 - lb), _row(hgrn_norm[l]))

        tables = _s5_tables(s5_a_re[l].astype(F32), s5_a_im[l].astype(F32), s5_log_dt[l].astype(F32),
                            s5_b_re[l].astype(F32), s5_b_im[l].astype(F32),
                            s5_c_re[l].astype(F32), s5_c_im[l].astype(F32))
        o_b = _s5(u3, p, tables, _row(s5_d[l]), s5_w_glu[l].astype(BF16), _row(s5_b_glu[l]))

        mu = rwkv_mu[l].astype(F32)
        o_c = _rwkv(p, _row(mu[:3 * W]), _row(mu[3 * W:]), _row(rwkv_w0[l]),
                    jnp.concatenate([rwkv_w2[l].astype(F32), zpad], axis=0).astype(BF16),
                    _row(rwkv_a0[l]),
                    jnp.concatenate([zpad, rwkv_a2[l].astype(F32)], axis=0).astype(BF16),
                    _row(rwkv_k_k[l]), _row(rwkv_k_a[l]), _row(rwkv_r_k[l]),
                    _row(rwkv_ln_w[l]), _row(rwkv_ln_b[l]))

        o_d = _ssd(p, m2_conv_w[l].astype(F32), _row(m2_conv_b[l]),
                   jnp.concatenate([_row(m2_dt_bias[l]), hpad], axis=1),
                   jnp.concatenate([-jnp.exp(_row(m2_a_log[l])), hpad], axis=1),
                   _row(jnp.repeat(m2_d[l].astype(F32), M2_HEADDIM)), _row(m2_norm[l]))

        merged = _up((o_a, o_b, o_c, o_d), p, w_up[l].astype(BF16), gate_bias[l].astype(F32))
        res = _out(merged, w_out[l].astype(BF16), norm_post[l].astype(F32), res)
    return res.astype(x.dtype)
```

```python
import functools
import math

import jax
import jax.numpy as jnp
from jax import lax
from jax.experimental import pallas as pl
from jax.experimental.pallas import tpu as pltpu

F32 = jnp.float32
BF16 = jnp.bfloat16
HI = lax.Precision.HIGHEST
LOG2E = 1.4426950408889634

D_MODEL = 2048
W = D_MODEL // 2
N_BRANCH = 4
NORM_EPS = 1e-6

HGRN_HEADS = 8
HGRN_DK = 128
HGRN_CHUNK = 64
HGRN_SUB = 8
HGRN_HPS = 4

S5_GROUP = 16
S5_GROUPS = 64
S5_STATE = 64
S5_CHUNK = 16
S5_MAX_RE = -1e-4
S5_GBLK = 8

RWKV_HEAD = 64
RWKV_LORA = 64
RWKV_CHUNK = 64
RWKV_SUB = 16
RWKV_DECAY_SCALE = 0.606531
RWKV_LN_EPS = 64e-5

M2_HEADS = 16
M2_HEADDIM = 64
M2_GROUPS = 4
M2_STATE = 128
M2_CONV = 4
M2_CHUNK = 128
M2_NORM_EPS = 1e-5

LANE = 128
SUB = 8
VMEM_LIMIT = 56 * 1024 * 1024

COL_QA, COL_FA, COL_IA, COL_ZA = 0, 1, 2, 3
COL_UB, COL_ZB = 4, 5
COL_RC, COL_KC, COL_VC = 6, 7, 8
OFF_WA = 9 * W
OFF_ZC = OFF_WA + 2 * RWKV_LORA
OFF_XD = OFF_ZC + W
OFF_BCD = OFF_XD + W
OFF_DT = OFF_BCD + 2 * M2_GROUPS * M2_STATE
P1_COLS = OFF_DT + LANE
OFF_P2 = OFF_DT + M2_HEADS
P2_ZD, P2_GATE = 0, 1
P2_COLS = W + N_BRANCH * D_MODEL
P1_TN = 1792
P2_TN = 2304
U_TILE = (COL_UB * 1024) // P1_TN
U_OFF = COL_UB * 1024 - U_TILE * P1_TN


def _cparams(sem):
    return pltpu.CompilerParams(dimension_semantics=sem, vmem_limit_bytes=VMEM_LIMIT)


def _dot(a, b):
    return jnp.dot(a.astype(BF16), b.astype(BF16), preferred_element_type=F32)


def _dot_nt(a, b):
    return lax.dot_general(a.astype(BF16), b.astype(BF16), (((1,), (1,)), ((), ())),
                           preferred_element_type=F32)


def _dot_tn(a, b):
    return lax.dot_general(a.astype(BF16), b.astype(BF16), (((0,), (0,)), ((), ())),
                           preferred_element_type=F32)


def _dot_hi(a, b):
    return jnp.dot(a, b, precision=HI, preferred_element_type=F32)


def _dot_solve(a, b):
    return _dot(a, b)


def _split(x, n):
    parts = []
    for _ in range(n):
        part = x.astype(BF16)
        parts.append(part)
        x = x - part.astype(F32)
    return parts


def _dot_sel(x, sel_tiled, n):
    return jnp.dot(jnp.concatenate(_split(x, n), axis=1), sel_tiled, preferred_element_type=F32)


def _silu(x):
    return x * jax.nn.sigmoid(x)


def _cumsum_rows(x):
    n = x.shape[0]
    row = lax.broadcasted_iota(jnp.int32, x.shape, 0)
    s = 1
    while s < n:
        x = x + jnp.where(row >= s, pltpu.roll(x, s, 0), 0.0)
        s *= 2
    return x


def _prenorm_kernel(x_ref, g_ref, o_ref):
    x = x_ref[...]
    o_ref[...] = (x * lax.rsqrt(jnp.mean(x * x, axis=-1, keepdims=True) + NORM_EPS)
                  * g_ref[...]).astype(o_ref.dtype)


def _prenorm(x, g):
    L = x.shape[1]
    tm = min(512, L)
    return pl.pallas_call(
        _prenorm_kernel,
        out_shape=jax.ShapeDtypeStruct((L, D_MODEL), BF16),
        grid=(L // tm,),
        in_specs=[pl.BlockSpec((None, tm, D_MODEL), lambda i: (0, i, 0)),
                  pl.BlockSpec((1, D_MODEL), lambda i: (0, 0))],
        out_specs=pl.BlockSpec((tm, D_MODEL), lambda i: (i, 0)),
        compiler_params=_cparams(("parallel",)),
    )(x, g.reshape(1, D_MODEL))


def _inproj_kernel(a_ref, w_ref, o_ref, u3_ref):
    acc = jnp.dot(a_ref[...], w_ref[...], preferred_element_type=F32)
    o_ref[...] = acc

    @pl.when(pl.program_id(1) == U_TILE)
    def _():
        u3_ref[...] = acc[:, U_OFF:U_OFF + W].reshape(u3_ref.shape)


def _inproj(a, w):
    M, K = a.shape
    tm = min(1024, M)
    return pl.pallas_call(
        _inproj_kernel,
        out_shape=(jax.ShapeDtypeStruct((M, P1_COLS), F32),
                   jax.ShapeDtypeStruct((M // S5_CHUNK, S5_CHUNK, W), F32)),
        grid=(M // tm, P1_COLS // P1_TN),
        in_specs=[pl.BlockSpec((tm, K), lambda i, j: (i, 0)),
                  pl.BlockSpec((K, P1_TN), lambda i, j: (0, j))],
        out_specs=(pl.BlockSpec((tm, P1_TN), lambda i, j: (i, j)),
                   pl.BlockSpec((tm // S5_CHUNK, S5_CHUNK, W), lambda i, j: (i, 0, 0))),
        compiler_params=_cparams(("parallel", "arbitrary")),
    )(a, w)


def _mm_kernel(a_ref, w_ref, o_ref):
    o_ref[...] = jnp.dot(a_ref[...], w_ref[...], preferred_element_type=F32).astype(o_ref.dtype)


def _matmul(a, w, tn, out_dtype):
    M, K = a.shape
    N = w.shape[1]
    tm = min(1024, M)
    return pl.pallas_call(
        _mm_kernel,
        out_shape=jax.ShapeDtypeStruct((M, N), out_dtype),
        grid=(M // tm, N // tn),
        in_specs=[pl.BlockSpec((tm, K), lambda i, j: (i, 0)),
                  pl.BlockSpec((K, tn), lambda i, j: (0, j))],
        out_specs=pl.BlockSpec((tm, tn), lambda i, j: (i, j)),
        compiler_params=_cparams(("parallel", "parallel")),
    )(a, w)


def _hgrn_kernel(q_ref, f_ref, i_ref, z_ref, loglb_ref, log1mlb_ref, omlb_ref, nw_ref,
                 o_ref, st_ref, *, n_chunks):
    @pl.when(pl.program_id(1) == 0)
    def _():
        st_ref[...] = jnp.zeros_like(st_ref)

    C, SB, DK = HGRN_CHUNK, HGRN_SUB, HGRN_DK
    NB = C // SB
    heads = range(HGRN_HPS)
    hs = [slice(j * DK, (j + 1) * DK) for j in heads]
    loglb, log1mlb, omlb, nw = loglb_ref[...], log1mlb_ref[...], omlb_ref[...], nw_ref[...]
    trow = lax.broadcasted_iota(jnp.int32, (SB, DK), 0)
    col_c = lax.broadcasted_iota(jnp.int32, (SB, C), 1)
    ones = jnp.ones((DK, DK), BF16)

    def chunk(c, carry):
        rows = pl.ds(pl.multiple_of(c * C, C), C)
        f = f_ref[rows, :]
        log_sig = jnp.minimum(f, 0.0) - jnp.log1p(jnp.exp(-jnp.abs(f)))
        b = log1mlb + log_sig
        logf = jnp.maximum(loglb, b) + jnp.log1p(jnp.exp(-jnp.abs(loglb - b)))
        cum_all = _cumsum_rows(logf) * LOG2E
        k_all = omlb * jax.nn.sigmoid(-f)
        q_all = _silu(q_ref[rows, :]) * (DK ** -0.5)
        cum = [cum_all[:, s_] for s_ in hs]
        k = [k_all[:, s_] for s_ in hs]
        q = [q_all[:, s_] for s_ in hs]
        v = [i_ref[rows, s_].astype(BF16) for s_ in hs]
        st = [st_ref[j] for j in heads]
        o = [_dot_nt(q[j] * jnp.exp2(cum[j]), st[j]) for j in heads]

        sc = [[None] * NB for _ in heads]
        for bi in range(1, NB):
            r0 = bi * SB
            for j in heads:
                ref = cum[j][r0:r0 + 1, :]
                kmod = k[j][0:r0, :] * jnp.exp2(ref - cum[j][0:r0, :])
                kmod = jnp.concatenate([kmod, jnp.zeros((C - r0, DK), F32)], axis=0)
                sc[j][bi] = _dot_nt(q[j][r0:r0 + SB, :] * jnp.exp2(cum[j][r0:r0 + SB, :] - ref), kmod)
        psum = []
        for j in heads:
            ps = []
            for bi in range(NB):
                r0 = bi * SB
                for s in range(SB):
                    d = jnp.where(trow >= s, cum[j][r0:r0 + SB, :] - cum[j][r0 + s:r0 + s + 1, :], -jnp.inf)
                    ps.append((q[j][r0:r0 + SB, :] * k[j][r0 + s:r0 + s + 1, :] * jnp.exp2(d)).astype(BF16))
            psum.append(jnp.dot(jnp.concatenate(ps, axis=0), ones, preferred_element_type=F32))
        for j in heads:
            blocks = []
            for bi in range(NB):
                acc = jnp.zeros((SB, C), F32) if bi == 0 else sc[j][bi]
                for s in range(SB):
                    r1 = (bi * SB + s) * SB
                    acc = jnp.where(col_c == bi * SB + s, psum[j][r1:r1 + SB, 0:C], acc)
                blocks.append(acc)
            o[j] = o[j] + _dot(jnp.concatenate(blocks, axis=0), v[j])

        for j in heads:
            last = cum[j][C - 1:C, :]
            st_ref[j] = st[j] * jnp.exp2(last) + _dot_tn(v[j], k[j] * jnp.exp2(last - cum[j]))
        for j in heads:
            oj = o[j] * lax.rsqrt(jnp.mean(o[j] * o[j], axis=-1, keepdims=True) + NORM_EPS) * nw[:, hs[j]]
            o_ref[rows, hs[j]] = (oj * _silu(z_ref[rows, hs[j]])).astype(o_ref.dtype)
        return carry

    lax.fori_loop(0, n_chunks, chunk, 0)


def _hgrn(p, loglb, log1mlb, omlb, nw):
    L = p.shape[0]
    T = min(512, L)
    nt = L // T
    gw = HGRN_HPS * HGRN_DK
    ng = W // gw
    blk = lambda c: pl.BlockSpec((T, gw), lambda h, t, c=c: (t, c * ng + h))
    vec = pl.BlockSpec((1, gw), lambda h, t: (0, h))
    return pl.pallas_call(
        functools.partial(_hgrn_kernel, n_chunks=T // HGRN_CHUNK),
        out_shape=jax.ShapeDtypeStruct((L, W), BF16),
        grid=(ng, nt),
        in_specs=[blk(COL_QA), blk(COL_FA), blk(COL_IA), blk(COL_ZA), vec, vec, vec, vec],
        out_specs=pl.BlockSpec((T, gw), lambda h, t: (t, h)),
        scratch_shapes=[pltpu.VMEM((HGRN_HPS, HGRN_DK, HGRN_DK), F32)],
        compiler_params=_cparams(("arbitrary", "arbitrary")),
    )(p, p, p, p, loglb, log1mlb, omlb, nw)


def _ssd_kernel(x_ref, bc_ref, z_ref, dt_ref, cw_ref, cb_ref, dtb_ref, a_ref, e_ref, d_ref, nw_ref,
                o_ref, xb_ref, st_ref):
    C = M2_CHUNK
    CH = 2 * W

    @pl.when(pl.program_id(0) == 0)
    def _():
        xb_ref[0:SUB, :] = jnp.zeros((SUB, CH), F32)
        st_ref[...] = jnp.zeros_like(st_ref)

    xb_ref[SUB:SUB + C, 0:W] = x_ref[...]
    xb_ref[SUB:SUB + C, W:CH] = bc_ref[...]
    acc = cb_ref[...] + cw_ref[0:1, :] * xb_ref[pl.ds(SUB - 3, C), :]
    for j in range(1, M2_CONV):
        acc = acc + cw_ref[j:j + 1, :] * xb_ref[pl.ds(SUB - 3 + j, C), :]
    xb_ref[0:SUB, :] = xb_ref[C:C + SUB, :]
    xc = _silu(acc)
    xh = xc[:, 0:W]

    dt = jax.nn.softplus(dt_ref[...] + dtb_ref[...])
    da = dt * a_ref[...]
    r_i = lax.broadcasted_iota(jnp.int32, (C, C), 0)
    c_i = lax.broadcasted_iota(jnp.int32, (C, C), 1)
    causal = r_i >= c_i
    causal3 = (lax.broadcasted_iota(jnp.int32, (C, 3 * C), 0)
               >= lax.broadcasted_iota(jnp.int32, (C, 3 * C), 1) % C).astype(BF16)
    a_cs = jnp.dot(causal3, jnp.concatenate(_split(da, 3), axis=0),
                   preferred_element_type=F32)
    a_cs_t = a_cs.T
    e = e_ref[...]
    ea = jnp.exp(a_cs)
    dt_e = _dot_sel(dt, e, 2)
    ea_e = _dot_sel(ea, e, 2)
    te_e = _dot_sel(jnp.exp(a_cs[C - 1:C, :] - a_cs), e, 2)
    xd = xh * dt_e
    xd_end = xd * te_e

    GW = W // M2_GROUPS
    lane_g = lax.broadcasted_iota(jnp.int32, (C, GW), 1)
    groups = range(M2_GROUPS)
    hpg = M2_HEADS // M2_GROUPS
    gsl = [slice(g * GW, (g + 1) * GW) for g in groups]
    c0 = W + M2_GROUPS * M2_STATE
    bg_t = [xc[:, W + g * M2_STATE:W + (g + 1) * M2_STATE].T.astype(BF16) for g in groups]
    cg = [xc[:, c0 + g * M2_STATE:c0 + (g + 1) * M2_STATE].astype(BF16) for g in groups]
    cb = [_dot(cg[g], bg_t[g]) for g in groups]
    y = [_dot(cg[g], st_ref[g]) * ea_e[:, gsl[g]] for g in groups]
    for g in groups:
        ms, xs = [], []
        for j in range(hpg):
            h = g * hpg + j
            seg = a_cs[:, h:h + 1] - a_cs_t[h:h + 1, :]
            ms.append((cb[g] * jnp.exp(jnp.where(causal, seg, -jnp.inf))).astype(BF16))
            xs.append(jnp.where(lane_g // M2_HEADDIM == j, xd[:, gsl[g]], 0.0).astype(BF16))
        y[g] = y[g] + _dot(jnp.concatenate(ms, axis=1), jnp.concatenate(xs, axis=0))
    for g in groups:
        st_ref[g] = st_ref[g] * ea_e[C - 1:C, gsl[g]] + _dot(bg_t[g], xd_end[:, gsl[g]])
    for g in groups:
        yg = (y[g] + d_ref[:, gsl[g]] * xh[:, gsl[g]]) * _silu(z_ref[:, gsl[g]].astype(F32))
        yg = yg * lax.rsqrt(jnp.mean(yg * yg, axis=-1, keepdims=True) + M2_NORM_EPS)
        o_ref[:, gsl[g]] = (yg * nw_ref[:, gsl[g]]).astype(o_ref.dtype)


def _ssd(p, p2, cw, cb, dtb, a, d_e, nw):
    L = p.shape[0]
    e = (jnp.arange(LANE)[:, None] == (jnp.arange(W) // M2_HEADDIM)[None, :]).astype(BF16)
    e = jnp.tile(e, (2, 1))
    C = M2_CHUNK
    at = lambda off: pl.BlockSpec((pl.Element(C), pl.Element(W)), lambda t, off=off: (t * C, off))
    full = lambda a_: pl.BlockSpec(a_.shape, lambda t: (0,) * a_.ndim)
    return pl.pallas_call(
        _ssd_kernel,
        out_shape=jax.ShapeDtypeStruct((L, W), BF16),
        grid=(L // C,),
        in_specs=[at(OFF_XD), at(OFF_BCD), pl.BlockSpec((C, W), lambda t: (t, P2_ZD)),
                  pl.BlockSpec((C, LANE), lambda t: (t, OFF_DT // LANE)),
                  full(cw), full(cb), full(dtb), full(a), full(e), full(d_e), full(nw)],
        out_specs=pl.BlockSpec((C, W), lambda t: (t, 0)),
        scratch_shapes=[pltpu.VMEM((SUB + C + SUB, 2 * W), F32),
                        pltpu.VMEM((M2_GROUPS, M2_STATE, W // M2_GROUPS), F32)],
        compiler_params=_cparams(("arbitrary",)),
    )(p, p, p2, p, cw, cb, dtb, a, e, d_e, nw)


def _rwkv_kernel(r_ref, k_ref, v_ref, wa_ref, z_ref, mu_ref, muwa_ref, w0_ref, w2_ref, a0_ref, a2_ref,
                 kk_ref, ka_ref, rk_ref, lnw_ref, lnb_ref,
                 o_ref, xb_ref, wab_ref, st_ref):
    C = RWKV_CHUNK

    @pl.when(pl.program_id(0) == 0)
    def _():
        xb_ref[0:SUB, :] = jnp.zeros((SUB, 3 * W), F32)
        wab_ref[0:SUB, :] = jnp.zeros((SUB, LANE), F32)
        st_ref[...] = jnp.zeros_like(st_ref)

    xb_ref[SUB:SUB + C, 0:W] = r_ref[...]
    xb_ref[SUB:SUB + C, W:2 * W] = k_ref[...]
    xb_ref[SUB:SUB + C, 2 * W:3 * W] = v_ref[...]
    wab_ref[SUB:SUB + C, :] = wa_ref[...]
    cur = xb_ref[SUB:SUB + C, :]
    xs = cur + mu_ref[...] * (xb_ref[pl.ds(SUB - 1, C), :] - cur)
    wa_cur = wab_ref[SUB:SUB + C, :]
    wa = wa_cur + muwa_ref[...] * (wab_ref[pl.ds(SUB - 1, C), :] - wa_cur)
    xb_ref[0:SUB, :] = xb_ref[C:C + SUB, :]
    wab_ref[0:SUB, :] = wab_ref[C:C + SUB, :]

    logw_all = -RWKV_DECAY_SCALE * jax.nn.sigmoid(w0_ref[...] + _dot(jnp.tanh(wa), w2_ref[...]))
    iclr_all = jax.nn.sigmoid(a0_ref[...] + _dot(wa, a2_ref[...]))

    lane = lax.broadcasted_iota(jnp.int32, (C, LANE), 1)
    lo = lane < RWKV_HEAD

    def hsum(x):
        s_lo = jnp.sum(jnp.where(lo, x, 0.0), axis=-1, keepdims=True)
        s_hi = jnp.sum(jnp.where(lo, 0.0, x), axis=-1, keepdims=True)
        return jnp.where(lo, s_lo, s_hi)
    t_i = lax.broadcasted_iota(jnp.int32, (C, 2 * C), 0)
    s_i = lax.broadcasted_iota(jnp.int32, (C, 2 * C), 1) % C
    strict = t_i > s_i
    incl_uv = (lax.broadcasted_iota(jnp.int32, (C, 4 * C), 0)
               >= lax.broadcasted_iota(jnp.int32, (C, 4 * C), 1) % C)
    r2 = lax.broadcasted_iota(jnp.int32, (2 * C, 2 * C), 0)
    c2 = lax.broadcasted_iota(jnp.int32, (2 * C, 2 * C), 1)
    same_blk = (r2 // RWKV_SUB) == (c2 // RWKV_SUB)
    eye = (r2 == c2).astype(F32)

    def stack(x):
        return jnp.concatenate([jnp.where(lo, x, 0.0), jnp.where(lo, 0.0, x)], axis=0)

    pairs = range(W // LANE)
    sls = [slice(i * LANE, (i + 1) * LANE) for i in pairs]
    r = [xs[:, sl] for sl in sls]
    k = [xs[:, W + i * LANE:W + (i + 1) * LANE] for i in pairs]
    v = [xs[:, 2 * W + i * LANE:2 * W + (i + 1) * LANE] for i in pairs]
    a = [iclr_all[:, sl] for sl in sls]
    kk = [k[i] * kk_ref[:, sls[i]] for i in pairs]
    ss = [hsum(kk[i] * kk[i]) for i in pairs]
    kk = [kk[i] / jnp.maximum(jnp.sqrt(ss[i]), 1e-12) for i in pairs]
    k2 = [k[i] * (1.0 + (a[i] - 1.0) * ka_ref[:, sls[i]]) for i in pairs]
    cum = [_cumsum_rows(logw_all[:, sl]) for sl in sls]
    e_inv = [jnp.exp(-cum[i]) for i in pairs]
    ar = [jnp.concatenate([-kk[i] * jnp.exp(cum[i] - logw_all[:, sls[i]]), r[i] * jnp.exp(cum[i])],
                          axis=0).astype(BF16) for i in pairs]
    bk = [jnp.concatenate([stack(kk[i] * a[i] * e_inv[i]), stack(k2[i] * e_inv[i])],
                          axis=0).astype(BF16) for i in pairs]
    v_s = [stack(v[i]).astype(BF16) for i in pairs]
    sc = [_dot_nt(ar[i], bk[i]) for i in pairs]
    ah = [_dot_nt(ar[i], st_ref[i]) for i in pairs]
    w_ = [ah[i][0:C, :] + _dot(jnp.where(strict, sc[i][0:C, 2 * C:4 * C], 0.0), v_s[i]) for i in pairs]
    n_p = [stack(jnp.where(strict, sc[i][0:C, 0:2 * C], 0.0)) for i in pairs]
    n_d = [jnp.where(same_blk, n_p[i], 0.0) for i in pairs]
    t_d = [eye + n_d[i] for i in pairs]
    pw = n_d
    for _ in range(3):
        pw = [_dot_solve(pw[i], pw[i]) for i in pairs]
        t_d = [t_d[i] + _dot_solve(pw[i], t_d[i]) for i in pairs]
    m_ = [_dot_solve(t_d[i], n_p[i] - n_d[i]) for i in pairs]
    u_s = [_dot_solve(t_d[i], stack(w_[i])) for i in pairs]
    m2 = [_dot_solve(m_[i], m_[i]) for i in pairs]
    u_s = [u_s[i] + _dot_solve(m2[i], u_s[i]) for i in pairs]
    u_s = [u_s[i] + _dot_solve(m_[i], u_s[i]) for i in pairs]
    uv = [jnp.concatenate([u_s[i].astype(BF16), v_s[i]], axis=0) for i in pairs]
    y = [ah[i][C:2 * C, :] + _dot(jnp.where(incl_uv, sc[i][C:2 * C, :], 0.0), uv[i]) for i in pairs]
    for i in pairs:
        st_ref[i] = (st_ref[i] + _dot_tn(uv[i], bk[i])) * jnp.exp(cum[i][C - 1:C, :])
    mean = [hsum(y[i]) * (1.0 / RWKV_HEAD) for i in pairs]
    yc = [y[i] - mean[i] for i in pairs]
    var = [hsum(yc[i] * yc[i]) * (1.0 / RWKV_HEAD) for i in pairs]
    bonus = [hsum(r[i] * k2[i] * rk_ref[:, sls[i]]) * v[i] for i in pairs]
    for i in pairs:
        yn = yc[i] * lax.rsqrt(var[i] + RWKV_LN_EPS) * lnw_ref[:, sls[i]] + lnb_ref[:, sls[i]]
        o_ref[:, sls[i]] = ((yn + bonus[i]) * _silu(z_ref[:, sls[i]])).astype(o_ref.dtype)


def _rwkv(p, mu, muwa, w0, w2p, a0, a2p, k_k, k_a, r_k, ln_w, ln_b):
    L = p.shape[0]
    C = RWKV_CHUNK
    wide = lambda c: pl.BlockSpec((C, W), lambda t, c=c: (t, c))
    full = lambda a_: pl.BlockSpec(a_.shape, lambda t: (0,) * a_.ndim)
    consts = (mu, muwa, w0, w2p, a0, a2p, k_k, k_a, r_k, ln_w, ln_b)
    return pl.pallas_call(
        _rwkv_kernel,
        out_shape=jax.ShapeDtypeStruct((L, W), BF16),
        grid=(L // C,),
        in_specs=[wide(COL_RC), wide(COL_KC), wide(COL_VC),
                  pl.BlockSpec((C, LANE), lambda t: (t, OFF_WA // LANE)),
                  pl.BlockSpec((pl.Element(C), pl.Element(W)), lambda t: (t * C, OFF_ZC))]
                 + [full(c) for c in consts],
        out_specs=pl.BlockSpec((C, W), lambda t: (t, 0)),
        scratch_shapes=[pltpu.VMEM((SUB + C + SUB, 3 * W), F32),
                        pltpu.VMEM((SUB + C + SUB, LANE), F32),
                        pltpu.VMEM((W // LANE, LANE, LANE), F32)],
        compiler_params=_cparams(("arbitrary",)),
    )(p, p, p, p, p, *consts)


def _s5_kernel(u_ref, perm_ref, m_ref, toe_ref, q_ref, l1_ref, l2_ref, l2s_ref, y_ref,
               xl_ref, xls_ref, xp_ref, car_ref):
    @pl.when(pl.program_id(1) == 0)
    def _():
        car_ref[...] = jnp.zeros_like(car_ref)

    tn = u_ref.shape[0]
    gw = S5_CHUNK * S5_GROUP
    perm = perm_ref[...]
    lhs = jnp.concatenate([u_ref[:, j, :].astype(BF16) for j in range(S5_CHUNK)], axis=1)
    lhs = jnp.dot(lhs, perm, preferred_element_type=F32).astype(BF16)
    lg = [lhs[:, g * gw:(g + 1) * gw] for g in range(S5_GBLK)]
    xl = jnp.concatenate([jnp.dot(lg[g], m_ref[g], preferred_element_type=F32) for g in range(S5_GBLK)],
                         axis=1)
    xl_ref[...] = xl
    xls_ref[...] = jnp.concatenate(
        [pltpu.roll(xl[:, g * LANE:(g + 1) * LANE], S5_STATE, 1) for g in range(S5_GBLK)], axis=1)
    l1, l2, l2s = l1_ref[...], l2_ref[...], l2s_ref[...]

    def step(i, carry):
        x, xs = carry
        row = pl.ds(i, 1)
        xp_ref[row, :] = x
        return (l1 * x + l2 * xs + xl_ref[row, :], l1 * xs + l2s * x + xls_ref[row, :])

    x, xs = lax.fori_loop(0, tn, step, (car_ref[0:1, :], car_ref[1:2, :]))
    car_ref[0:1, :] = x
    car_ref[1:2, :] = xs
    xp = xp_ref[...].astype(BF16)
    y = jnp.concatenate(
        [jnp.dot(lg[g], toe_ref[g], preferred_element_type=F32)
         + jnp.dot(xp[:, g * LANE:(g + 1) * LANE], q_ref[g], preferred_element_type=F32)
         for g in range(S5_GBLK)], axis=1)
    y = lax.dot_general(y.astype(BF16), perm, (((1,), (1,)), ((), ())),
                        preferred_element_type=F32)
    for t in range(S5_CHUNK):
        y_ref[:, t, :] = y[:, t * LANE:(t + 1) * LANE]


def _s5_scan(u3, m, toe, q, l1, l2, l2s):
    n = u3.shape[0]
    tn = min(512, n)
    nb = W // LANE
    sw = S5_GBLK * 2 * S5_STATE
    kw = S5_CHUNK * LANE
    r = jnp.arange(kw)
    dst = (r // S5_GROUP % S5_GBLK) * (S5_CHUNK * S5_GROUP) + (r // LANE) * S5_GROUP + r % S5_GROUP
    perm = (dst[:, None] == jnp.arange(kw)[None, :]).astype(BF16)
    per_b = lambda a_: pl.BlockSpec((S5_GBLK,) + a_.shape[1:], lambda b, i: (b, 0, 0))
    vec = pl.BlockSpec((None, 1, sw), lambda b, i: (b, 0, 0))
    return pl.pallas_call(
        _s5_kernel,
        out_shape=jax.ShapeDtypeStruct((n, S5_CHUNK, W), F32),
        grid=(nb, n // tn),
        in_specs=[pl.BlockSpec((tn, S5_CHUNK, LANE), lambda b, i: (i, 0, b)),
                  pl.BlockSpec((kw, kw), lambda b, i: (0, 0)),
                  per_b(m), per_b(toe), per_b(q), vec, vec, vec],
        out_specs=pl.BlockSpec((tn, S5_CHUNK, LANE), lambda b, i: (i, 0, b)),
        scratch_shapes=[pltpu.VMEM((tn, sw), F32)] * 3 + [pltpu.VMEM((SUB, sw), F32)],
        compiler_params=_cparams(("arbitrary", "arbitrary")),
    )(u3, perm, m, toe, q, l1, l2, l2s)


def _s5_out_kernel(y_ref, u_ref, z_ref, d_ref, wg_ref, bg_ref, o_ref):
    y = y_ref[...].reshape(u_ref.shape) + d_ref[...] * u_ref[...]
    g = jax.nn.gelu(y)
    o = g * jax.nn.sigmoid(_dot(g, wg_ref[...]) + bg_ref[...])
    o_ref[...] = (o * _silu(z_ref[...])).astype(o_ref.dtype)


def _s5_out(y3, p, d, wg, bg):
    L = p.shape[0]
    tm = min(512, L)
    vec = pl.BlockSpec((1, W), lambda t: (0, 0))
    return pl.pallas_call(
        _s5_out_kernel,
        out_shape=jax.ShapeDtypeStruct((L, W), BF16),
        grid=(L // tm,),
        in_specs=[pl.BlockSpec((tm // S5_CHUNK, S5_CHUNK, W), lambda t: (t, 0, 0)),
                  pl.BlockSpec((tm, W), lambda t: (t, COL_UB)),
                  pl.BlockSpec((tm, W), lambda t: (t, COL_ZB)),
                  vec, pl.BlockSpec((W, W), lambda t: (0, 0)), vec],
        out_specs=pl.BlockSpec((tm, W), lambda t: (t, 0)),
        compiler_params=_cparams(("parallel",)),
    )(y3, p, p, d, wg, bg)


def _s5_tables(a_re, a_im, log_dt, b_re, b_im, c_re, c_im):
    J = S5_CHUNK
    lam_re = jnp.minimum(a_re, S5_MAX_RE)
    lam_im = a_im
    dt = jnp.exp(log_dt)[:, None]
    dl_re, dl_im = lam_re * dt, lam_im * dt
    mag = jnp.exp(dl_re)
    num_re = mag * jnp.cos(dl_im) - 1.0
    num_im = mag * jnp.sin(dl_im)
    den = lam_re * lam_re + lam_im * lam_im
    coef_re = (num_re * lam_re + num_im * lam_im) / den
    coef_im = (num_im * lam_re - num_re * lam_im) / den
    bb_re = coef_re[..., None] * b_re - coef_im[..., None] * b_im
    bb_im = coef_re[..., None] * b_im + coef_im[..., None] * b_re
    tau = jnp.arange(J + 1, dtype=F32)[:, None, None]
    pm = jnp.exp(tau * dl_re)
    pw_re, pw_im = pm * jnp.cos(tau * dl_im), pm * jnp.sin(tau * dl_im)
    pb_re = pw_re[..., None] * bb_re - pw_im[..., None] * bb_im
    pb_im = pw_re[..., None] * bb_im + pw_im[..., None] * bb_re
    kern = (jnp.einsum('gop,tgph->tgoh', c_re, pb_re[:J], precision=HI)
            - jnp.einsum('gop,tgph->tgoh', c_im, pb_im[:J], precision=HI))
    t_idx = jnp.arange(J)
    lag = t_idx[None, :] - t_idx[:, None]
    toe = jnp.where((lag >= 0)[:, :, None, None, None],
                    kern[jnp.clip(lag, 0, J - 1)], 0.0)
    nb, gb = S5_GROUPS // S5_GBLK, S5_GBLK
    toe = toe.transpose(2, 0, 4, 1, 3).reshape(S5_GROUPS, J * S5_GROUP, J * S5_GROUP)
    rev_re, rev_im = pb_re[J - 1 - t_idx], pb_im[J - 1 - t_idx]
    to_m = lambda x: x.transpose(1, 0, 3, 2).reshape(S5_GROUPS, J * S5_GROUP, S5_STATE)
    m = jnp.concatenate([to_m(rev_re), to_m(rev_im)], axis=-1)
    q_re = (c_re[None] * pw_re[1:, :, None, :] - c_im[None] * pw_im[1:, :, None, :])
    q_im = -(c_re[None] * pw_im[1:, :, None, :] + c_im[None] * pw_re[1:, :, None, :])
    to_q = lambda x: x.transpose(1, 3, 0, 2).reshape(S5_GROUPS, S5_STATE, J * S5_GROUP)
    q = jnp.concatenate([to_q(q_re), to_q(q_im)], axis=1)
    lj_re, lj_im = pw_re[J], pw_im[J]
    rs = lambda parts: jnp.concatenate(parts, axis=-1).reshape(nb, 1, gb * 2 * S5_STATE)
    return (m.astype(BF16), toe.astype(BF16), q.astype(BF16),
            rs([lj_re, lj_re]), rs([-lj_im, lj_im]), rs([lj_im, -lj_im]))


def _s5(u3, p, tables, d, wg, bg):
    return _s5_out(_s5_scan(u3, *tables), p, d, wg, bg)


def _up_kernel(oa_ref, ob_ref, oc_ref, od_ref, ga_ref, gb_ref, gc_ref, gd_ref, w_ref, bias_ref, o_ref):
    acc = None
    for b, (o_b, g_b) in enumerate(((oa_ref, ga_ref), (ob_ref, gb_ref), (oc_ref, gc_ref), (od_ref, gd_ref))):
        gate = jax.nn.sigmoid(g_b[...] + bias_ref[b:b + 1, :])
        contrib = gate * jnp.dot(o_b[...], w_ref[b], preferred_element_type=F32)
        acc = contrib if acc is None else acc + contrib
    o_ref[...] = acc.astype(o_ref.dtype)


def _up(outs, p, w_up, bias):
    L = p.shape[0]
    tm = min(1024, L)
    tn = 512
    nb = D_MODEL // tn
    o_spec = pl.BlockSpec((tm, W), lambda i, j: (i, 0))
    g_spec = lambda b: pl.BlockSpec((tm, tn), lambda i, j, b=b: (i, (P2_GATE * W + b * D_MODEL) // tn + j))
    return pl.pallas_call(
        _up_kernel,
        out_shape=jax.ShapeDtypeStruct((L, D_MODEL), BF16),
        grid=(L // tm, nb),
        in_specs=[o_spec] * 4 + [g_spec(b) for b in range(N_BRANCH)]
                 + [pl.BlockSpec((N_BRANCH, W, tn), lambda i, j: (0, 0, j)),
                    pl.BlockSpec((N_BRANCH, tn), lambda i, j: (0, j))],
        out_specs=pl.BlockSpec((tm, tn), lambda i, j: (i, j)),
        compiler_params=_cparams(("parallel", "arbitrary")),
    )(*outs, p, p, p, p, w_up, bias)


def _out_kernel(m_ref, w_ref, g_ref, res_ref, o_ref):
    y = jnp.dot(m_ref[...], w_ref[...], preferred_element_type=F32)
    o_ref[...] = res_ref[...] + (y * lax.rsqrt(jnp.mean(y * y, axis=-1, keepdims=True) + NORM_EPS)
                                 * g_ref[...])


def _out(merged, w_out, g, res):
    L = res.shape[1]
    tm = min(512, L)
    return pl.pallas_call(
        _out_kernel,
        out_shape=jax.ShapeDtypeStruct((1, L, D_MODEL), F32),
        grid=(L // tm,),
        in_specs=[pl.BlockSpec((tm, D_MODEL), lambda i: (i, 0)),
                  pl.BlockSpec((D_MODEL, D_MODEL), lambda i: (0, 0)),
                  pl.BlockSpec((1, D_MODEL), lambda i: (0, 0)),
                  pl.BlockSpec((None, tm, D_MODEL), lambda i: (0, i, 0))],
        out_specs=pl.BlockSpec((None, tm, D_MODEL), lambda i: (0, i, 0)),
        compiler_params=_cparams(("parallel",)),
    )(merged, w_out, g.reshape(1, D_MODEL), res)


def _pack_w_in(w):
    return w[:, :P1_COLS].astype(BF16), w[:, OFF_P2:OFF_P2 + P2_COLS].astype(BF16)


def _row(v):
    return v.astype(F32).reshape(1, -1)


def kernel(x, norm_pre, norm_post, w_in, gate_bias, w_up, w_out, hgrn_lb_logits, hgrn_norm, s5_a_re, s5_a_im, s5_log_dt, s5_b_re, s5_b_im, s5_c_re, s5_c_im, s5_d, s5_w_glu, s5_b_glu, rwkv_mu, rwkv_w0, rwkv_w2, rwkv_a0, rwkv_a2, rwkv_k_k, rwkv_k_a, rwkv_r_k, rwkv_ln_w, rwkv_ln_b, m2_conv_w, m2_conv_b, m2_dt_bias, m2_a_log, m2_d, m2_norm):
    depth = w_in.shape[0]
    L = x.shape[1]
    lb_all = jnp.cumsum(jax.nn.softmax(hgrn_lb_logits.astype(F32), axis=0), axis=0)
    lb_all = lb_all - lb_all[0:1]

    zpad = jnp.zeros((RWKV_LORA, W), F32)
    hpad = jnp.zeros((1, LANE - M2_HEADS), F32)

    res = x.astype(F32)
    for l in range(depth):
        h = _prenorm(res, norm_pre[l])
        w1, w2 = _pack_w_in(w_in[l])
        p, u3 = _inproj(h, w1)
        p2 = _matmul(h, w2, P2_TN, BF16)

        lb = lb_all[l]
        o_a = _hgrn(p, _row(jnp.log(lb)), _row(jnp.log1p(-lb)), _row(1.0 - lb), _row(hgrn_norm[l]))

        tables = _s5_tables(s5_a_re[l].astype(F32), s5_a_im[l].astype(F32), s5_log_dt[l].astype(F32),
                            s5_b_re[l].astype(F32), s5_b_im[l].astype(F32),
                            s5_c_re[l].astype(F32), s5_c_im[l].astype(F32))
        o_b = _s5(u3, p, tables, _row(s5_d[l]), s5_w_glu[l].astype(BF16), _row(s5_b_glu[l]))

        mu = rwkv_mu[l].astype(F32)
        o_c = _rwkv(p, _row(mu[:3 * W]), _row(mu[3 * W:]), _row(rwkv_w0[l]),
                    jnp.concatenate([rwkv_w2[l].astype(F32), zpad], axis=0).astype(BF16),
                    _row(rwkv_a0[l]),
                    jnp.concatenate([zpad, rwkv_a2[l].astype(F32)], axis=0).astype(BF16),
                    _row(rwkv_k_k[l]), _row(rwkv_k_a[l]), _row(rwkv_r_k[l]),
                    _row(rwkv_ln_w[l]), _row(rwkv_ln_b[l]))

        o_d = _ssd(p, p2, m2_conv_w[l].astype(F32), _row(m2_conv_b[l]),
                   jnp.concatenate([_row(m2_dt_bias[l]), hpad], axis=1),
                   jnp.concatenate([-jnp.exp(_row(m2_a_log[l])), hpad], axis=1),
                   _row(jnp.repeat(m2_d[l].astype(F32), M2_HEADDIM)), _row(m2_norm[l]))

        merged = _up((o_a, o_b, o_c, o_d), p2, w_up[l].astype(BF16), gate_bias[l].astype(F32))
        res = _out(merged, w_out[l].astype(BF16), norm_post[l].astype(F32), res)
    return res.astype(x.dtype)
```

```python
import functools
import math

import jax
import jax.numpy as jnp
from jax import lax
from jax.experimental import pallas as pl
from jax.experimental.pallas import tpu as pltpu

F32 = jnp.float32
BF16 = jnp.bfloat16
HI = lax.Precision.HIGHEST
LOG2E = 1.4426950408889634

D_MODEL = 2048
W = D_MODEL // 2
N_BRANCH = 4
NORM_EPS = 1e-6

HGRN_HEADS = 8
HGRN_DK = 128
HGRN_CHUNK = 64
HGRN_SUB = 8
HGRN_HPS = 4

S5_GROUP = 16
S5_GROUPS = 64
S5_STATE = 64
S5_CHUNK = 16
S5_MAX_RE = -1e-4
S5_GBLK = 8

RWKV_HEAD = 64
RWKV_LORA = 64
RWKV_CHUNK = 64
RWKV_SUB = 16
RWKV_DECAY_SCALE = 0.606531
RWKV_LN_EPS = 64e-5

M2_HEADS = 16
M2_HEADDIM = 64
M2_GROUPS = 4
M2_STATE = 128
M2_CONV = 4
M2_CHUNK = 128
M2_NORM_EPS = 1e-5

LANE = 128
SUB = 8
VMEM_LIMIT = 56 * 1024 * 1024

COL_QA, COL_FA, COL_IA, COL_ZA = 0, 1, 2, 3
COL_UB, COL_ZB = 4, 5
COL_RC, COL_KC, COL_VC = 6, 7, 8
OFF_WA = 9 * W
OFF_ZC = OFF_WA + 2 * RWKV_LORA
OFF_XD = OFF_ZC + W
OFF_BCD = OFF_XD + W
OFF_DT = OFF_BCD + 2 * M2_GROUPS * M2_STATE
P1_COLS = OFF_DT + LANE
OFF_P2 = OFF_DT + M2_HEADS
P2_ZD, P2_GATE = 0, 1
P2_COLS = W + N_BRANCH * D_MODEL
P1_TN = 1792
P2_TN = 2304
U_TILE = (COL_UB * 1024) // P1_TN
U_OFF = COL_UB * 1024 - U_TILE * P1_TN


def _cparams(sem):
    return pltpu.CompilerParams(dimension_semantics=sem, vmem_limit_bytes=VMEM_LIMIT)


def _dot(a, b):
    return jnp.dot(a.astype(BF16), b.astype(BF16), preferred_element_type=F32)


def _dot_nt(a, b):
    return lax.dot_general(a.astype(BF16), b.astype(BF16), (((1,), (1,)), ((), ())),
                           preferred_element_type=F32)


def _dot_tn(a, b):
    return lax.dot_general(a.astype(BF16), b.astype(BF16), (((0,), (0,)), ((), ())),
                           preferred_element_type=F32)


def _dot_hi(a, b):
    return jnp.dot(a, b, precision=HI, preferred_element_type=F32)


def _dot_solve(a, b):
    return _dot(a, b)


def _split(x, n):
    parts = []
    for _ in range(n):
        part = x.astype(BF16)
        parts.append(part)
        x = x - part.astype(F32)
    return parts


def _dot_sel(x, sel_tiled, n):
    return jnp.dot(jnp.concatenate(_split(x, n), axis=1), sel_tiled, preferred_element_type=F32)


def _silu(x):
    return x * jax.nn.sigmoid(x)


def _cumsum_rows(x):
    n = x.shape[0]
    row = lax.broadcasted_iota(jnp.int32, x.shape, 0)
    s = 1
    while s < n:
        x = x + jnp.where(row >= s, pltpu.roll(x, s, 0), 0.0)
        s *= 2
    return x


def _prenorm_kernel(x_ref, g_ref, o_ref):
    x = x_ref[...]
    o_ref[...] = (x * lax.rsqrt(jnp.mean(x * x, axis=-1, keepdims=True) + NORM_EPS)
                  * g_ref[...]).astype(o_ref.dtype)


def _prenorm(x, g):
    L = x.shape[1]
    tm = min(512, L)
    return pl.pallas_call(
        _prenorm_kernel,
        out_shape=jax.ShapeDtypeStruct((L, D_MODEL), BF16),
        grid=(L // tm,),
        in_specs=[pl.BlockSpec((None, tm, D_MODEL), lambda i: (0, i, 0)),
                  pl.BlockSpec((1, D_MODEL), lambda i: (0, 0))],
        out_specs=pl.BlockSpec((tm, D_MODEL), lambda i: (i, 0)),
        compiler_params=_cparams(("parallel",)),
    )(x, g.reshape(1, D_MODEL))


def _inproj_kernel(a_ref, w_ref, o_ref, u3_ref):
    acc = jnp.dot(a_ref[...], w_ref[...], preferred_element_type=F32)
    o_ref[...] = acc

    @pl.when(pl.program_id(1) == U_TILE)
    def _():
        u3_ref[...] = acc[:, U_OFF:U_OFF + W].reshape(u3_ref.shape)


def _inproj(a, w):
    M, K = a.shape
    tm = min(1024, M)
    return pl.pallas_call(
        _inproj_kernel,
        out_shape=(jax.ShapeDtypeStruct((M, P1_COLS), F32),
                   jax.ShapeDtypeStruct((M // S5_CHUNK, S5_CHUNK, W), F32)),
        grid=(M // tm, P1_COLS // P1_TN),
        in_specs=[pl.BlockSpec((tm, K), lambda i, j: (i, 0)),
                  pl.BlockSpec((K, P1_TN), lambda i, j: (0, j))],
        out_specs=(pl.BlockSpec((tm, P1_TN), lambda i, j: (i, j)),
                   pl.BlockSpec((tm // S5_CHUNK, S5_CHUNK, W), lambda i, j: (i, 0, 0))),
        compiler_params=_cparams(("parallel", "arbitrary")),
    )(a, w)


def _mm_kernel(a_ref, w_ref, o_ref):
    o_ref[...] = jnp.dot(a_ref[...], w_ref[...], preferred_element_type=F32).astype(o_ref.dtype)


def _matmul(a, w, tn, out_dtype):
    M, K = a.shape
    N = w.shape[1]
    tm = min(1024, M)
    return pl.pallas_call(
        _mm_kernel,
        out_shape=jax.ShapeDtypeStruct((M, N), out_dtype),
        grid=(M // tm, N // tn),
        in_specs=[pl.BlockSpec((tm, K), lambda i, j: (i, 0)),
                  pl.BlockSpec((K, tn), lambda i, j: (0, j))],
        out_specs=pl.BlockSpec((tm, tn), lambda i, j: (i, j)),
        compiler_params=_cparams(("parallel", "parallel")),
    )(a, w)


def _hgrn_kernel(q_ref, f_ref, i_ref, z_ref, loglb_ref, log1mlb_ref, omlb_ref, nw_ref,
                 o_ref, st_ref, *, n_chunks):
    @pl.when(pl.program_id(1) == 0)
    def _():
        st_ref[...] = jnp.zeros_like(st_ref)

    C, SB, DK = HGRN_CHUNK, HGRN_SUB, HGRN_DK
    NB = C // SB
    heads = range(HGRN_HPS)
    hs = [slice(j * DK, (j + 1) * DK) for j in heads]
    loglb, log1mlb, omlb, nw = loglb_ref[...], log1mlb_ref[...], omlb_ref[...], nw_ref[...]
    trow = lax.broadcasted_iota(jnp.int32, (SB, DK), 0)
    col_c = lax.broadcasted_iota(jnp.int32, (SB, C), 1)
    ones = jnp.ones((DK, DK), BF16)

    def chunk(c, carry):
        rows = pl.ds(pl.multiple_of(c * C, C), C)
        f = f_ref[rows, :]
        log_sig = jnp.minimum(f, 0.0) - jnp.log1p(jnp.exp(-jnp.abs(f)))
        b = log1mlb + log_sig
        logf = jnp.maximum(loglb, b) + jnp.log1p(jnp.exp(-jnp.abs(loglb - b)))
        cum_all = _cumsum_rows(logf) * LOG2E
        k_all = omlb * jax.nn.sigmoid(-f)
        q_all = _silu(q_ref[rows, :]) * (DK ** -0.5)
        cum = [cum_all[:, s_] for s_ in hs]
        k = [k_all[:, s_] for s_ in hs]
        q = [q_all[:, s_] for s_ in hs]
        v = [i_ref[rows, s_].astype(BF16) for s_ in hs]
        st = [st_ref[j] for j in heads]
        o = [_dot_nt(q[j] * jnp.exp2(cum[j]), st[j]) for j in heads]

        sc = [[None] * NB for _ in heads]
        for bi in range(1, NB):
            r0 = bi * SB
            for j in heads:
                ref = cum[j][r0:r0 + 1, :]
                kmod = k[j][0:r0, :] * jnp.exp2(ref - cum[j][0:r0, :])
                kmod = jnp.concatenate([kmod, jnp.zeros((C - r0, DK), F32)], axis=0)
                sc[j][bi] = _dot_nt(q[j][r0:r0 + SB, :] * jnp.exp2(cum[j][r0:r0 + SB, :] - ref), kmod)
        psum = []
        for j in heads:
            ps = []
            for bi in range(NB):
                r0 = bi * SB
                for s in range(SB):
                    d = jnp.where(trow >= s, cum[j][r0:r0 + SB, :] - cum[j][r0 + s:r0 + s + 1, :], -jnp.inf)
                    ps.append((q[j][r0:r0 + SB, :] * k[j][r0 + s:r0 + s + 1, :] * jnp.exp2(d)).astype(BF16))
            psum.append(jnp.dot(jnp.concatenate(ps, axis=0), ones, preferred_element_type=F32))
        for j in heads:
            blocks = []
            for bi in range(NB):
                acc = jnp.zeros((SB, C), F32) if bi == 0 else sc[j][bi]
                for s in range(SB):
                    r1 = (bi * SB + s) * SB
                    acc = jnp.where(col_c == bi * SB + s, psum[j][r1:r1 + SB, 0:C], acc)
                blocks.append(acc)
            o[j] = o[j] + _dot(jnp.concatenate(blocks, axis=0), v[j])

        for j in heads:
            last = cum[j][C - 1:C, :]
            st_ref[j] = st[j] * jnp.exp2(last) + _dot_tn(v[j], k[j] * jnp.exp2(last - cum[j]))
        for j in heads:
            oj = o[j] * lax.rsqrt(jnp.mean(o[j] * o[j], axis=-1, keepdims=True) + NORM_EPS) * nw[:, hs[j]]
            o_ref[rows, hs[j]] = (oj * _silu(z_ref[rows, hs[j]])).astype(o_ref.dtype)
        return carry

    lax.fori_loop(0, n_chunks, chunk, 0)


def _hgrn(p, loglb, log1mlb, omlb, nw):
    L = p.shape[0]
    T = min(512, L)
    nt = L // T
    gw = HGRN_HPS * HGRN_DK
    ng = W // gw
    blk = lambda c: pl.BlockSpec((T, gw), lambda h, t, c=c: (t, c * ng + h))
    vec = pl.BlockSpec((1, gw), lambda h, t: (0, h))
    return pl.pallas_call(
        functools.partial(_hgrn_kernel, n_chunks=T // HGRN_CHUNK),
        out_shape=jax.ShapeDtypeStruct((L, W), BF16),
        grid=(ng, nt),
        in_specs=[blk(COL_QA), blk(COL_FA), blk(COL_IA), blk(COL_ZA), vec, vec, vec, vec],
        out_specs=pl.BlockSpec((T, gw), lambda h, t: (t, h)),
        scratch_shapes=[pltpu.VMEM((HGRN_HPS, HGRN_DK, HGRN_DK), F32)],
        compiler_params=_cparams(("arbitrary", "arbitrary")),
    )(p, p, p, p, loglb, log1mlb, omlb, nw)


def _ssd_kernel(x_ref, bc_ref, z_ref, dt_ref, cw_ref, cb_ref, dtb_ref, a_ref, e_ref, d_ref, nw_ref,
                o_ref, xb_ref, st_ref):
    C = M2_CHUNK
    CH = 2 * W

    @pl.when(pl.program_id(0) == 0)
    def _():
        xb_ref[0:SUB, :] = jnp.zeros((SUB, CH), F32)
        st_ref[...] = jnp.zeros_like(st_ref)

    xb_ref[SUB:SUB + C, 0:W] = x_ref[...]
    xb_ref[SUB:SUB + C, W:CH] = bc_ref[...]
    acc = cb_ref[...] + cw_ref[0:1, :] * xb_ref[pl.ds(SUB - 3, C), :]
    for j in range(1, M2_CONV):
        acc = acc + cw_ref[j:j + 1, :] * xb_ref[pl.ds(SUB - 3 + j, C), :]
    xb_ref[0:SUB, :] = xb_ref[C:C + SUB, :]
    xc = _silu(acc)
    xh = xc[:, 0:W]

    dt = jax.nn.softplus(dt_ref[...] + dtb_ref[...])
    da = dt * a_ref[...]
    r_i = lax.broadcasted_iota(jnp.int32, (C, C), 0)
    c_i = lax.broadcasted_iota(jnp.int32, (C, C), 1)
    causal = r_i >= c_i
    causal3 = (lax.broadcasted_iota(jnp.int32, (C, 3 * C), 0)
               >= lax.broadcasted_iota(jnp.int32, (C, 3 * C), 1) % C).astype(BF16)
    a_cs = jnp.dot(causal3, jnp.concatenate(_split(da, 3), axis=0),
                   preferred_element_type=F32)
    a_cs_t = a_cs.T
    e = e_ref[...]
    ea = jnp.exp(a_cs)
    dt_e = _dot_sel(dt, e, 2)
    ea_e = _dot_sel(ea, e, 2)
    te_e = _dot_sel(jnp.exp(a_cs[C - 1:C, :] - a_cs), e, 2)
    xd = xh * dt_e
    xd_end = xd * te_e

    GW = W // M2_GROUPS
    lane_g = lax.broadcasted_iota(jnp.int32, (C, GW), 1)
    groups = range(M2_GROUPS)
    hpg = M2_HEADS // M2_GROUPS
    gsl = [slice(g * GW, (g + 1) * GW) for g in groups]
    c0 = W + M2_GROUPS * M2_STATE
    bg_t = [xc[:, W + g * M2_STATE:W + (g + 1) * M2_STATE].T.astype(BF16) for g in groups]
    cg = [xc[:, c0 + g * M2_STATE:c0 + (g + 1) * M2_STATE].astype(BF16) for g in groups]
    cb = [_dot(cg[g], bg_t[g]) for g in groups]
    y = [_dot(cg[g], st_ref[g]) * ea_e[:, gsl[g]] for g in groups]
    for g in groups:
        ms, xs = [], []
        for j in range(hpg):
            h = g * hpg + j
            seg = a_cs[:, h:h + 1] - a_cs_t[h:h + 1, :]
            ms.append((cb[g] * jnp.exp(jnp.where(causal, seg, -jnp.inf))).astype(BF16))
            xs.append(jnp.where(lane_g // M2_HEADDIM == j, xd[:, gsl[g]], 0.0).astype(BF16))
        y[g] = y[g] + _dot(jnp.concatenate(ms, axis=1), jnp.concatenate(xs, axis=0))
    for g in groups:
        st_ref[g] = st_ref[g] * ea_e[C - 1:C, gsl[g]] + _dot(bg_t[g], xd_end[:, gsl[g]])
    for g in groups:
        yg = (y[g] + d_ref[:, gsl[g]] * xh[:, gsl[g]]) * _silu(z_ref[:, gsl[g]].astype(F32))
        yg = yg * lax.rsqrt(jnp.mean(yg * yg, axis=-1, keepdims=True) + M2_NORM_EPS)
        o_ref[:, gsl[g]] = (yg * nw_ref[:, gsl[g]]).astype(o_ref.dtype)


def _ssd(p, p2, cw, cb, dtb, a, d_e, nw):
    L = p.shape[0]
    e = (jnp.arange(LANE)[:, None] == (jnp.arange(W) // M2_HEADDIM)[None, :]).astype(BF16)
    e = jnp.tile(e, (2, 1))
    C = M2_CHUNK
    at = lambda off: pl.BlockSpec((pl.Element(C), pl.Element(W)), lambda t, off=off: (t * C, off))
    full = lambda a_: pl.BlockSpec(a_.shape, lambda t: (0,) * a_.ndim)
    return pl.pallas_call(
        _ssd_kernel,
        out_shape=jax.ShapeDtypeStruct((L, W), BF16),
        grid=(L // C,),
        in_specs=[at(OFF_XD), at(OFF_BCD), pl.BlockSpec((C, W), lambda t: (t, P2_ZD)),
                  pl.BlockSpec((C, LANE), lambda t: (t, OFF_DT // LANE)),
                  full(cw), full(cb), full(dtb), full(a), full(e), full(d_e), full(nw)],
        out_specs=pl.BlockSpec((C, W), lambda t: (t, 0)),
        scratch_shapes=[pltpu.VMEM((SUB + C + SUB, 2 * W), F32),
                        pltpu.VMEM((M2_GROUPS, M2_STATE, W // M2_GROUPS), F32)],
        compiler_params=_cparams(("arbitrary",)),
    )(p, p, p2, p, cw, cb, dtb, a, e, d_e, nw)


def _rwkv_kernel(r_ref, k_ref, v_ref, wa_ref, z_ref, mu_ref, muwa_ref, w0_ref, w2_ref, a0_ref, a2_ref,
                 kk_ref, ka_ref, rk_ref, lnw_ref, lnb_ref,
                 o_ref, xb_ref, wab_ref, st_ref):
    C = RWKV_CHUNK

    @pl.when(pl.program_id(0) == 0)
    def _():
        xb_ref[0:SUB, :] = jnp.zeros((SUB, 3 * W), F32)
        wab_ref[0:SUB, :] = jnp.zeros((SUB, LANE), F32)
        st_ref[...] = jnp.zeros_like(st_ref)

    xb_ref[SUB:SUB + C, 0:W] = r_ref[...]
    xb_ref[SUB:SUB + C, W:2 * W] = k_ref[...]
    xb_ref[SUB:SUB + C, 2 * W:3 * W] = v_ref[...]
    wab_ref[SUB:SUB + C, :] = wa_ref[...]
    cur = xb_ref[SUB:SUB + C, :]
    xs = cur + mu_ref[...] * (xb_ref[pl.ds(SUB - 1, C), :] - cur)
    wa_cur = wab_ref[SUB:SUB + C, :]
    wa = wa_cur + muwa_ref[...] * (wab_ref[pl.ds(SUB - 1, C), :] - wa_cur)
    xb_ref[0:SUB, :] = xb_ref[C:C + SUB, :]
    wab_ref[0:SUB, :] = wab_ref[C:C + SUB, :]

    logw_all = -RWKV_DECAY_SCALE * jax.nn.sigmoid(w0_ref[...] + _dot(jnp.tanh(wa), w2_ref[...]))
    iclr_all = jax.nn.sigmoid(a0_ref[...] + _dot(wa, a2_ref[...]))

    lane = lax.broadcasted_iota(jnp.int32, (C, LANE), 1)
    lo = lane < RWKV_HEAD

    def hsum(x):
        s_lo = jnp.sum(jnp.where(lo, x, 0.0), axis=-1, keepdims=True)
        s_hi = jnp.sum(jnp.where(lo, 0.0, x), axis=-1, keepdims=True)
        return jnp.where(lo, s_lo, s_hi)
    t_i = lax.broadcasted_iota(jnp.int32, (C, 2 * C), 0)
    s_i = lax.broadcasted_iota(jnp.int32, (C, 2 * C), 1) % C
    strict = t_i > s_i
    incl_uv = (lax.broadcasted_iota(jnp.int32, (C, 4 * C), 0)
               >= lax.broadcasted_iota(jnp.int32, (C, 4 * C), 1) % C)
    r2 = lax.broadcasted_iota(jnp.int32, (2 * C, 2 * C), 0)
    c2 = lax.broadcasted_iota(jnp.int32, (2 * C, 2 * C), 1)
    same_blk = (r2 // RWKV_SUB) == (c2 // RWKV_SUB)
    eye = (r2 == c2).astype(F32)

    def stack(x):
        return jnp.concatenate([jnp.where(lo, x, 0.0), jnp.where(lo, 0.0, x)], axis=0)

    pairs = list(range(W // LANE))
    sls = [slice(i * LANE, (i + 1) * LANE) for i in pairs]
    d = [dict() for _ in pairs]

    def p_keys(i):
        sl = sls[i]
        k = xs[:, W + i * LANE:W + (i + 1) * LANE]
        kk = k * kk_ref[:, sl]
        d[i].update(r=xs[:, sl], v=xs[:, 2 * W + i * LANE:2 * W + (i + 1) * LANE],
                    kk=kk, ss=hsum(kk * kk), k2=k * (1.0 + (iclr_all[:, sl] - 1.0) * ka_ref[:, sl]))

    def p_norm(i):
        d[i]['kk'] = d[i]['kk'] / jnp.maximum(jnp.sqrt(d[i]['ss']), 1e-12)

    def p_decay(i):
        cum = _cumsum_rows(logw_all[:, sls[i]])
        d[i].update(cum=cum, e_inv=jnp.exp(-cum))

    def p_ar(i):
        cum = d[i]['cum']
        d[i]['ar'] = jnp.concatenate([-d[i]['kk'] * jnp.exp(cum - logw_all[:, sls[i]]), d[i]['r'] * jnp.exp(cum)],
                                     axis=0).astype(BF16)

    def p_bk(i):
        e_inv = d[i]['e_inv']
        d[i]['bk'] = jnp.concatenate([stack(d[i]['kk'] * iclr_all[:, sls[i]] * e_inv), stack(d[i]['k2'] * e_inv)],
                                     axis=0).astype(BF16)
        d[i]['v_s'] = stack(d[i]['v']).astype(BF16)

    def s_scores(i):
        d[i]['sc'] = _dot_nt(d[i]['ar'], d[i]['bk'])

    def s_state(i):
        d[i]['ah'] = _dot_nt(d[i]['ar'], st_ref[i])

    def s_w(i):
        sc = d[i]['sc']
        d[i]['w'] = d[i]['ah'][0:C, :] + _dot(jnp.where(strict, sc[0:C, 2 * C:4 * C], 0.0), d[i]['v_s'])

    def s_n(i):
        n_p = stack(jnp.where(strict, d[i]['sc'][0:C, 0:2 * C], 0.0))
        n_d = jnp.where(same_blk, n_p, 0.0)
        d[i].update(n_p=n_p, n_d=n_d, pw=n_d, t_d=eye + n_d)

    def s_pow(i):
        d[i]['pw'] = _dot_solve(d[i]['pw'], d[i]['pw'])
        d[i]['t_d'] = d[i]['t_d'] + _dot_solve(d[i]['pw'], d[i]['t_d'])

    def s_m(i):
        d[i]['m'] = _dot_solve(d[i]['t_d'], d[i]['n_p'] - d[i]['n_d'])
        d[i]['u'] = _dot_solve(d[i]['t_d'], stack(d[i]['w']))

    def s_m2(i):
        d[i]['m2'] = _dot_solve(d[i]['m'], d[i]['m'])

    def s_u1(i):
        d[i]['u'] = d[i]['u'] + _dot_solve(d[i]['m2'], d[i]['u'])

    def s_u2(i):
        d[i]['u'] = d[i]['u'] + _dot_solve(d[i]['m'], d[i]['u'])

    def s_y(i):
        uv = jnp.concatenate([d[i]['u'].astype(BF16), d[i]['v_s']], axis=0)
        d[i]['uv'] = uv
        d[i]['y'] = d[i]['ah'][C:2 * C, :] + _dot(jnp.where(incl_uv, d[i]['sc'][C:2 * C, :], 0.0), uv)

    def s_carry(i):
        st_ref[i] = (st_ref[i] + _dot_tn(d[i]['uv'], d[i]['bk'])) * jnp.exp(d[i]['cum'][C - 1:C, :])

    def e_center(i):
        d[i]['yc'] = d[i]['y'] - hsum(d[i]['y']) * (1.0 / RWKV_HEAD)

    def e_var(i):
        d[i]['var'] = hsum(d[i]['yc'] * d[i]['yc']) * (1.0 / RWKV_HEAD)
        d[i]['bonus'] = hsum(d[i]['r'] * d[i]['k2'] * rk_ref[:, sls[i]]) * d[i]['v']

    def e_out(i):
        sl = sls[i]
        yn = d[i]['yc'] * lax.rsqrt(d[i]['var'] + RWKV_LN_EPS) * lnw_ref[:, sl] + lnb_ref[:, sl]
        o_ref[:, sl] = ((yn + d[i]['bonus']) * _silu(z_ref[:, sl])).astype(o_ref.dtype)

    for stage in (p_keys, p_norm, p_decay, p_ar, p_bk,
                  s_scores, s_state, s_w, s_n, s_pow, s_pow, s_pow, s_m, s_m2, s_u1, s_u2, s_y, s_carry,
                  e_center, e_var, e_out):
        for i in pairs:
            stage(i)


def _rwkv(p, mu, muwa, w0, w2p, a0, a2p, k_k, k_a, r_k, ln_w, ln_b):
    L = p.shape[0]
    C = RWKV_CHUNK
    wide = lambda c: pl.BlockSpec((C, W), lambda t, c=c: (t, c))
    full = lambda a_: pl.BlockSpec(a_.shape, lambda t: (0,) * a_.ndim)
    consts = (mu, muwa, w0, w2p, a0, a2p, k_k, k_a, r_k, ln_w, ln_b)
    return pl.pallas_call(
        _rwkv_kernel,
        out_shape=jax.ShapeDtypeStruct((L, W), BF16),
        grid=(L // C,),
        in_specs=[wide(COL_RC), wide(COL_KC), wide(COL_VC),
                  pl.BlockSpec((C, LANE), lambda t: (t, OFF_WA // LANE)),
                  pl.BlockSpec((pl.Element(C), pl.Element(W)), lambda t: (t * C, OFF_ZC))]
                 + [full(c) for c in consts],
        out_specs=pl.BlockSpec((C, W), lambda t: (t, 0)),
        scratch_shapes=[pltpu.VMEM((SUB + C + SUB, 3 * W), F32),
                        pltpu.VMEM((SUB + C + SUB, LANE), F32),
                        pltpu.VMEM((W // LANE, LANE, LANE), F32)],
        compiler_params=_cparams(("arbitrary",)),
    )(p, p, p, p, p, *consts)


def _s5_kernel(u_ref, perm_ref, m_ref, toe_ref, q_ref, l1_ref, l2_ref, l2s_ref, y_ref,
               xl_ref, xls_ref, xp_ref, car_ref):
    @pl.when(pl.program_id(1) == 0)
    def _():
        car_ref[...] = jnp.zeros_like(car_ref)

    tn = u_ref.shape[0]
    gw = S5_CHUNK * S5_GROUP
    perm = perm_ref[...]
    lhs = jnp.concatenate([u_ref[:, j, :].astype(BF16) for j in range(S5_CHUNK)], axis=1)
    lhs = jnp.dot(lhs, perm, preferred_element_type=F32).astype(BF16)
    lg = [lhs[:, g * gw:(g + 1) * gw] for g in range(S5_GBLK)]
    xl = jnp.concatenate([jnp.dot(lg[g], m_ref[g], preferred_element_type=F32) for g in range(S5_GBLK)],
                         axis=1)
    xl_ref[...] = xl
    xls_ref[...] = jnp.concatenate(
        [pltpu.roll(xl[:, g * LANE:(g + 1) * LANE], S5_STATE, 1) for g in range(S5_GBLK)], axis=1)
    l1, l2, l2s = l1_ref[...], l2_ref[...], l2s_ref[...]

    def step(i, carry):
        x, xs = carry
        row = pl.ds(i, 1)
        xp_ref[row, :] = x
        return (l1 * x + l2 * xs + xl_ref[row, :], l1 * xs + l2s * x + xls_ref[row, :])

    x, xs = lax.fori_loop(0, tn, step, (car_ref[0:1, :], car_ref[1:2, :]))
    car_ref[0:1, :] = x
    car_ref[1:2, :] = xs
    xp = xp_ref[...].astype(BF16)
    y = jnp.concatenate(
        [jnp.dot(lg[g], toe_ref[g], preferred_element_type=F32)
         + jnp.dot(xp[:, g * LANE:(g + 1) * LANE], q_ref[g], preferred_element_type=F32)
         for g in range(S5_GBLK)], axis=1)
    y = lax.dot_general(y.astype(BF16), perm, (((1,), (1,)), ((), ())),
                        preferred_element_type=F32)
    for t in range(S5_CHUNK):
        y_ref[:, t, :] = y[:, t * LANE:(t + 1) * LANE]


def _s5_scan(u3, m, toe, q, l1, l2, l2s):
    n = u3.shape[0]
    tn = min(512, n)
    nb = W // LANE
    sw = S5_GBLK * 2 * S5_STATE
    kw = S5_CHUNK * LANE
    r = jnp.arange(kw)
    dst = (r // S5_GROUP % S5_GBLK) * (S5_CHUNK * S5_GROUP) + (r // LANE) * S5_GROUP + r % S5_GROUP
    perm = (dst[:, None] == jnp.arange(kw)[None, :]).astype(BF16)
    per_b = lambda a_: pl.BlockSpec((S5_GBLK,) + a_.shape[1:], lambda b, i: (b, 0, 0))
    vec = pl.BlockSpec((None, 1, sw), lambda b, i: (b, 0, 0))
    return pl.pallas_call(
        _s5_kernel,
        out_shape=jax.ShapeDtypeStruct((n, S5_CHUNK, W), F32),
        grid=(nb, n // tn),
        in_specs=[pl.BlockSpec((tn, S5_CHUNK, LANE), lambda b, i: (i, 0, b)),
                  pl.BlockSpec((kw, kw), lambda b, i: (0, 0)),
                  per_b(m), per_b(toe), per_b(q), vec, vec, vec],
        out_specs=pl.BlockSpec((tn, S5_CHUNK, LANE), lambda b, i: (i, 0, b)),
        scratch_shapes=[pltpu.VMEM((tn, sw), F32)] * 3 + [pltpu.VMEM((SUB, sw), F32)],
        compiler_params=_cparams(("arbitrary", "arbitrary")),
    )(u3, perm, m, toe, q, l1, l2, l2s)


def _s5_out_kernel(y_ref, u_ref, z_ref, d_ref, wg_ref, bg_ref, o_ref):
    y = y_ref[...].reshape(u_ref.shape) + d_ref[...] * u_ref[...]
    g = jax.nn.gelu(y)
    o = g * jax.nn.sigmoid(_dot(g, wg_ref[...]) + bg_ref[...])
    o_ref[...] = (o * _silu(z_ref[...])).astype(o_ref.dtype)


def _s5_out(y3, p, d, wg, bg):
    L = p.shape[0]
    tm = min(512, L)
    vec = pl.BlockSpec((1, W), lambda t: (0, 0))
    return pl.pallas_call(
        _s5_out_kernel,
        out_shape=jax.ShapeDtypeStruct((L, W), BF16),
        grid=(L // tm,),
        in_specs=[pl.BlockSpec((tm // S5_CHUNK, S5_CHUNK, W), lambda t: (t, 0, 0)),
                  pl.BlockSpec((tm, W), lambda t: (t, COL_UB)),
                  pl.BlockSpec((tm, W), lambda t: (t, COL_ZB)),
                  vec, pl.BlockSpec((W, W), lambda t: (0, 0)), vec],
        out_specs=pl.BlockSpec((tm, W), lambda t: (t, 0)),
        compiler_params=_cparams(("parallel",)),
    )(y3, p, p, d, wg, bg)


def _s5_tables(a_re, a_im, log_dt, b_re, b_im, c_re, c_im):
    J = S5_CHUNK
    lam_re = jnp.minimum(a_re, S5_MAX_RE)
    lam_im = a_im
    dt = jnp.exp(log_dt)[:, None]
    dl_re, dl_im = lam_re * dt, lam_im * dt
    mag = jnp.exp(dl_re)
    num_re = mag * jnp.cos(dl_im) - 1.0
    num_im = mag * jnp.sin(dl_im)
    den = lam_re * lam_re + lam_im * lam_im
    coef_re = (num_re * lam_re + num_im * lam_im) / den
    coef_im = (num_im * lam_re - num_re * lam_im) / den
    bb_re = coef_re[..., None] * b_re - coef_im[..., None] * b_im
    bb_im = coef_re[..., None] * b_im + coef_im[..., None] * b_re
    tau = jnp.arange(J + 1, dtype=F32)[:, None, None]
    pm = jnp.exp(tau * dl_re)
    pw_re, pw_im = pm * jnp.cos(tau * dl_im), pm * jnp.sin(tau * dl_im)
    pb_re = pw_re[..., None] * bb_re - pw_im[..., None] * bb_im
    pb_im = pw_re[..., None] * bb_im + pw_im[..., None] * bb_re
    kern = (jnp.einsum('gop,tgph->tgoh', c_re, pb_re[:J], precision=HI)
            - jnp.einsum('gop,tgph->tgoh', c_im, pb_im[:J], precision=HI))
    t_idx = jnp.arange(J)
    toe = jnp.stack([jnp.concatenate([jnp.zeros((s,) + kern.shape[1:], F32), kern[:J - s]], axis=0)
                     for s in range(J)], axis=0)
    nb, gb = S5_GROUPS // S5_GBLK, S5_GBLK
    toe = toe.transpose(2, 0, 4, 1, 3).reshape(S5_GROUPS, J * S5_GROUP, J * S5_GROUP)
    rev_re, rev_im = pb_re[J - 1 - t_idx], pb_im[J - 1 - t_idx]
    to_m = lambda x: x.transpose(1, 0, 3, 2).reshape(S5_GROUPS, J * S5_GROUP, S5_STATE)
    m = jnp.concatenate([to_m(rev_re), to_m(rev_im)], axis=-1)
    q_re = (c_re[None] * pw_re[1:, :, None, :] - c_im[None] * pw_im[1:, :, None, :])
    q_im = -(c_re[None] * pw_im[1:, :, None, :] + c_im[None] * pw_re[1:, :, None, :])
    to_q = lambda x: x.transpose(1, 3, 0, 2).reshape(S5_GROUPS, S5_STATE, J * S5_GROUP)
    q = jnp.concatenate([to_q(q_re), to_q(q_im)], axis=1)
    lj_re, lj_im = pw_re[J], pw_im[J]
    rs = lambda parts: jnp.concatenate(parts, axis=-1).reshape(nb, 1, gb * 2 * S5_STATE)
    return (m.astype(BF16), toe.astype(BF16), q.astype(BF16),
            rs([lj_re, lj_re]), rs([-lj_im, lj_im]), rs([lj_im, -lj_im]))


def _s5(u3, p, tables, d, wg, bg):
    return _s5_out(_s5_scan(u3, *tables), p, d, wg, bg)


def _up_kernel(oa_ref, ob_ref, oc_ref, od_ref, ga_ref, gb_ref, gc_ref, gd_ref, w_ref, bias_ref, o_ref):
    acc = None
    for b, (o_b, g_b) in enumerate(((oa_ref, ga_ref), (ob_ref, gb_ref), (oc_ref, gc_ref), (od_ref, gd_ref))):
        gate = jax.nn.sigmoid(g_b[...] + bias_ref[b:b + 1, :])
        contrib = gate * jnp.dot(o_b[...], w_ref[b], preferred_element_type=F32)
        acc = contrib if acc is None else acc + contrib
    o_ref[...] = acc.astype(o_ref.dtype)


def _up(outs, p, w_up, bias):
    L = p.shape[0]
    tm = min(1024, L)
    tn = 512
    nb = D_MODEL // tn
    o_spec = pl.BlockSpec((tm, W), lambda i, j: (i, 0))
    g_spec = lambda b: pl.BlockSpec((tm, tn), lambda i, j, b=b: (i, (P2_GATE * W + b * D_MODEL) // tn + j))
    return pl.pallas_call(
        _up_kernel,
        out_shape=jax.ShapeDtypeStruct((L, D_MODEL), BF16),
        grid=(L // tm, nb),
        in_specs=[o_spec] * 4 + [g_spec(b) for b in range(N_BRANCH)]
                 + [pl.BlockSpec((N_BRANCH, W, tn), lambda i, j: (0, 0, j)),
                    pl.BlockSpec((N_BRANCH, tn), lambda i, j: (0, j))],
        out_specs=pl.BlockSpec((tm, tn), lambda i, j: (i, j)),
        compiler_params=_cparams(("parallel", "arbitrary")),
    )(*outs, p, p, p, p, w_up, bias)


def _out_kernel(m_ref, w_ref, g_ref, res_ref, o_ref):
    y = jnp.dot(m_ref[...], w_ref[...], preferred_element_type=F32)
    o_ref[...] = res_ref[...] + (y * lax.rsqrt(jnp.mean(y * y, axis=-1, keepdims=True) + NORM_EPS)
                                 * g_ref[...])


def _out(merged, w_out, g, res):
    L = res.shape[1]
    tm = min(512, L)
    return pl.pallas_call(
        _out_kernel,
        out_shape=jax.ShapeDtypeStruct((1, L, D_MODEL), F32),
        grid=(L // tm,),
        in_specs=[pl.BlockSpec((tm, D_MODEL), lambda i: (i, 0)),
                  pl.BlockSpec((D_MODEL, D_MODEL), lambda i: (0, 0)),
                  pl.BlockSpec((1, D_MODEL), lambda i: (0, 0)),
                  pl.BlockSpec((None, tm, D_MODEL), lambda i: (0, i, 0))],
        out_specs=pl.BlockSpec((None, tm, D_MODEL), lambda i: (0, i, 0)),
        compiler_params=_cparams(("parallel",)),
    )(merged, w_out, g.reshape(1, D_MODEL), res)


def _pack_w_in(w):
    return w[:, :P1_COLS].astype(BF16), w[:, OFF_P2:OFF_P2 + P2_COLS].astype(BF16)


def _row(v):
    return v.astype(F32).reshape(1, -1)


def kernel(x, norm_pre, norm_post, w_in, gate_bias, w_up, w_out, hgrn_lb_logits, hgrn_norm, s5_a_re, s5_a_im, s5_log_dt, s5_b_re, s5_b_im, s5_c_re, s5_c_im, s5_d, s5_w_glu, s5_b_glu, rwkv_mu, rwkv_w0, rwkv_w2, rwkv_a0, rwkv_a2, rwkv_k_k, rwkv_k_a, rwkv_r_k, rwkv_ln_w, rwkv_ln_b, m2_conv_w, m2_conv_b, m2_dt_bias, m2_a_log, m2_d, m2_norm):
    depth = w_in.shape[0]
    L = x.shape[1]
    lb_all = jnp.cumsum(jax.nn.softmax(hgrn_lb_logits.astype(F32), axis=0), axis=0)
    lb_all = lb_all - lb_all[0:1]

    zpad = jnp.zeros((RWKV_LORA, W), F32)
    hpad = jnp.zeros((1, LANE - M2_HEADS), F32)

    res = x.astype(F32)
    for l in range(depth):
        h = _prenorm(res, norm_pre[l])
        w1, w2 = _pack_w_in(w_in[l])
        p, u3 = _inproj(h, w1)
        p2 = _matmul(h, w2, P2_TN, BF16)

        lb = lb_all[l]
        o_a = _hgrn(p, _row(jnp.log(lb)), _row(jnp.log1p(-lb)), _row(1.0 - lb), _row(hgrn_norm[l]))

        tables = _s5_tables(s5_a_re[l].astype(F32), s5_a_im[l].astype(F32), s5_log_dt[l].astype(F32),
                            s5_b_re[l].astype(F32), s5_b_im[l].astype(F32),
                            s5_c_re[l].astype(F32), s5_c_im[l].astype(F32))
        o_b = _s5(u3, p, tables, _row(s5_d[l]), s5_w_glu[l].astype(BF16), _row(s5_b_glu[l]))

        mu = rwkv_mu[l].astype(F32)
        o_c = _rwkv(p, _row(mu[:3 * W]), _row(mu[3 * W:]), _row(rwkv_w0[l]),
                    jnp.concatenate([rwkv_w2[l].astype(F32), zpad], axis=0).astype(BF16),
                    _row(rwkv_a0[l]),
                    jnp.concatenate([zpad, rwkv_a2[l].astype(F32)], axis=0).astype(BF16),
                    _row(rwkv_k_k[l]), _row(rwkv_k_a[l]), _row(rwkv_r_k[l]),
                    _row(rwkv_ln_w[l]), _row(rwkv_ln_b[l]))

        o_d = _ssd(p, p2, m2_conv_w[l].astype(F32), _row(m2_conv_b[l]),
                   jnp.concatenate([_row(m2_dt_bias[l]), hpad], axis=1),
                   jnp.concatenate([-jnp.exp(_row(m2_a_log[l])), hpad], axis=1),
                   _row(jnp.repeat(m2_d[l].astype(F32), M2_HEADDIM)), _row(m2_norm[l]))

        merged = _up((o_a, o_b, o_c, o_d), p2, w_up[l].astype(BF16), gate_bias[l].astype(F32))
        res = _out(merged, w_out[l].astype(BF16), norm_post[l].astype(F32), res)
    return res.astype(x.dtype)
```

```python
import functools

import jax
import jax.numpy as jnp
from jax import lax
from jax.experimental import pallas as pl
from jax.experimental.pallas import tpu as pltpu

F32 = jnp.float32
BF16 = jnp.bfloat16
HI = lax.Precision.HIGHEST
LOG2E = 1.4426950408889634

D_MODEL = 2048
W = D_MODEL // 2
N_BRANCH = 4
NORM_EPS = 1e-6

HGRN_HEADS = 8
HGRN_DK = 128
HGRN_CHUNK = 64
HGRN_SUB = 8
HGRN_HPS = 4

S5_GROUP = 16
S5_GROUPS = 64
S5_STATE = 64
S5_CHUNK = 16
S5_MAX_RE = -1e-4
S5_GBLK = 8

RWKV_HEAD = 64
RWKV_LORA = 64
RWKV_CHUNK = 64
RWKV_SUB = 16
RWKV_DECAY_SCALE = 0.606531
RWKV_LN_EPS = 64e-5

M2_HEADS = 16
M2_HEADDIM = 64
M2_GROUPS = 4
M2_STATE = 128
M2_CONV = 4
M2_CHUNK = 128
M2_NORM_EPS = 1e-5

LANE = 128
SUB = 8
VMEM_LIMIT = 56 * 1024 * 1024

COL_QA, COL_FA, COL_IA, COL_ZA = 0, 1, 2, 3
COL_UB, COL_ZB = 4, 5
COL_RC, COL_KC, COL_VC = 6, 7, 8
OFF_WA = 9 * W
OFF_ZC = OFF_WA + 2 * RWKV_LORA
OFF_XD = OFF_ZC + W
OFF_BCD = OFF_XD + W
OFF_DT = OFF_BCD + 2 * M2_GROUPS * M2_STATE
P1_COLS = OFF_DT + LANE
OFF_P2 = OFF_DT + M2_HEADS
P2_ZD, P2_GATE = 0, 1
P2_COLS = W + N_BRANCH * D_MODEL
P1_TN = 1792
P2_TN = 2304
U_TILE = (COL_UB * 1024) // P1_TN
U_OFF = COL_UB * 1024 - U_TILE * P1_TN


def _cparams(sem):
    return pltpu.CompilerParams(dimension_semantics=sem, vmem_limit_bytes=VMEM_LIMIT)


def _dot(a, b):
    return jnp.dot(a.astype(BF16), b.astype(BF16), preferred_element_type=F32)


def _dot_nt(a, b):
    return lax.dot_general(a.astype(BF16), b.astype(BF16), (((1,), (1,)), ((), ())),
                           preferred_element_type=F32)


def _dot_tn(a, b):
    return lax.dot_general(a.astype(BF16), b.astype(BF16), (((0,), (0,)), ((), ())),
                           preferred_element_type=F32)


def _dot_solve(a, b):
    return _dot(a, b)


def _split(x, n):
    parts = []
    for _ in range(n):
        part = x.astype(BF16)
        parts.append(part)
        x = x - part.astype(F32)
    return parts


def _dot_sel(x, sel_tiled, n):
    return jnp.dot(jnp.concatenate(_split(x, n), axis=1), sel_tiled, preferred_element_type=F32)


def _silu(x):
    return x * jax.nn.sigmoid(x)


def _cumsum_rows(x):
    n = x.shape[0]
    row = lax.broadcasted_iota(jnp.int32, x.shape, 0)
    s = 1
    while s < n:
        x = x + jnp.where(row >= s, pltpu.roll(x, s, 0), 0.0)
        s *= 2
    return x


def _prenorm_kernel(x_ref, g_ref, o_ref):
    x = x_ref[...]
    o_ref[...] = (x * lax.rsqrt(jnp.mean(x * x, axis=-1, keepdims=True) + NORM_EPS)
                  * g_ref[...]).astype(o_ref.dtype)


def _prenorm(x, g):
    L = x.shape[1]
    tm = min(512, L)
    return pl.pallas_call(
        _prenorm_kernel,
        out_shape=jax.ShapeDtypeStruct((L, D_MODEL), BF16),
        grid=(L // tm,),
        in_specs=[pl.BlockSpec((None, tm, D_MODEL), lambda i: (0, i, 0)),
                  pl.BlockSpec((1, D_MODEL), lambda i: (0, 0))],
        out_specs=pl.BlockSpec((tm, D_MODEL), lambda i: (i, 0)),
        compiler_params=_cparams(("parallel",)),
    )(x, g.reshape(1, D_MODEL))


def _inproj_kernel(a_ref, w_ref, o_ref, u3_ref):
    acc = jnp.dot(a_ref[...], w_ref[...], preferred_element_type=F32)
    o_ref[...] = acc

    @pl.when(pl.program_id(1) == U_TILE)
    def _():
        u3_ref[...] = acc[:, U_OFF:U_OFF + W].reshape(u3_ref.shape)


def _inproj(a, w):
    M, K = a.shape
    tm = min(1024, M)
    return pl.pallas_call(
        _inproj_kernel,
        out_shape=(jax.ShapeDtypeStruct((M, P1_COLS), F32),
                   jax.ShapeDtypeStruct((M // S5_CHUNK, S5_CHUNK, W), F32)),
        grid=(M // tm, P1_COLS // P1_TN),
        in_specs=[pl.BlockSpec((tm, K), lambda i, j: (i, 0)),
                  pl.BlockSpec((K, P1_TN), lambda i, j: (0, j))],
        out_specs=(pl.BlockSpec((tm, P1_TN), lambda i, j: (i, j)),
                   pl.BlockSpec((tm // S5_CHUNK, S5_CHUNK, W), lambda i, j: (i, 0, 0))),
        compiler_params=_cparams(("parallel", "arbitrary")),
    )(a, w)


def _mm_kernel(a_ref, w_ref, o_ref):
    o_ref[...] = jnp.dot(a_ref[...], w_ref[...], preferred_element_type=F32).astype(o_ref.dtype)


def _matmul(a, w, tn, out_dtype):
    M, K = a.shape
    N = w.shape[1]
    tm = min(1024, M)
    return pl.pallas_call(
        _mm_kernel,
        out_shape=jax.ShapeDtypeStruct((M, N), out_dtype),
        grid=(M // tm, N // tn),
        in_specs=[pl.BlockSpec((tm, K), lambda i, j: (i, 0)),
                  pl.BlockSpec((K, tn), lambda i, j: (0, j))],
        out_specs=pl.BlockSpec((tm, tn), lambda i, j: (i, j)),
        compiler_params=_cparams(("parallel", "parallel")),
    )(a, w)


def _hgrn_kernel(q_ref, f_ref, i_ref, z_ref, loglb_ref, log1mlb_ref, omlb_ref, nw_ref,
                 o_ref, st_ref, *, n_chunks):
    @pl.when(pl.program_id(1) == 0)
    def _():
        st_ref[...] = jnp.zeros_like(st_ref)

    C, SB, DK = HGRN_CHUNK, HGRN_SUB, HGRN_DK
    NB = C // SB
    heads = range(HGRN_HPS)
    hs = [slice(j * DK, (j + 1) * DK) for j in heads]
    loglb, log1mlb, omlb, nw = loglb_ref[...], log1mlb_ref[...], omlb_ref[...], nw_ref[...]
    trow = lax.broadcasted_iota(jnp.int32, (SB, DK), 0)
    col_c = lax.broadcasted_iota(jnp.int32, (SB, C), 1)
    ones = jnp.ones((DK, DK), BF16)

    def chunk(c, carry):
        rows = pl.ds(pl.multiple_of(c * C, C), C)
        f = f_ref[rows, :]
        log_sig = jnp.minimum(f, 0.0) - jnp.log(1.0 + jnp.exp(-jnp.abs(f)))
        b = log1mlb + log_sig
        logf = jnp.maximum(loglb, b) + jnp.log(1.0 + jnp.exp(-jnp.abs(loglb - b)))
        cum_all = _cumsum_rows(logf) * LOG2E
        k_all = omlb * jax.nn.sigmoid(-f)
        q_all = _silu(q_ref[rows, :]) * (DK ** -0.5)
        cum = [cum_all[:, s_] for s_ in hs]
        k = [k_all[:, s_] for s_ in hs]
        q = [q_all[:, s_] for s_ in hs]
        v = [i_ref[rows, s_].astype(BF16) for s_ in hs]
        st = [st_ref[j] for j in heads]
        o = [_dot_nt(q[j] * jnp.exp2(cum[j]), st[j]) for j in heads]

        sc = [[None] * NB for _ in heads]
        for bi in range(1, NB):
            r0 = bi * SB
            for j in heads:
                ref = cum[j][r0:r0 + 1, :]
                kmod = k[j][0:r0, :] * jnp.exp2(ref - cum[j][0:r0, :])
                kmod = jnp.concatenate([kmod, jnp.zeros((C - r0, DK), F32)], axis=0)
                sc[j][bi] = _dot_nt(q[j][r0:r0 + SB, :] * jnp.exp2(cum[j][r0:r0 + SB, :] - ref), kmod)
        psum = []
        for j in heads:
            ps = []
            for bi in range(NB):
                r0 = bi * SB
                for s in range(SB):
                    d = jnp.where(trow >= s, cum[j][r0:r0 + SB, :] - cum[j][r0 + s:r0 + s + 1, :], -jnp.inf)
                    ps.append((q[j][r0:r0 + SB, :] * k[j][r0 + s:r0 + s + 1, :] * jnp.exp2(d)).astype(BF16))
            psum.append(jnp.dot(jnp.concatenate(ps, axis=0), ones, preferred_element_type=F32))
        for j in heads:
            blocks = []
            for bi in range(NB):
                acc = jnp.zeros((SB, C), F32) if bi == 0 else sc[j][bi]
                for s in range(SB):
                    r1 = (bi * SB + s) * SB
                    acc = jnp.where(col_c == bi * SB + s, psum[j][r1:r1 + SB, 0:C], acc)
                blocks.append(acc)
            o[j] = o[j] + _dot(jnp.concatenate(blocks, axis=0), v[j])

        for j in heads:
            last = cum[j][C - 1:C, :]
            st_ref[j] = st[j] * jnp.exp2(last) + _dot_tn(v[j], k[j] * jnp.exp2(last - cum[j]))
        for j in heads:
            oj = o[j] * lax.rsqrt(jnp.mean(o[j] * o[j], axis=-1, keepdims=True) + NORM_EPS) * nw[:, hs[j]]
            o_ref[rows, hs[j]] = (oj * _silu(z_ref[rows, hs[j]])).astype(o_ref.dtype)
        return carry

    lax.fori_loop(0, n_chunks, chunk, 0)


def _hgrn(p, loglb, log1mlb, omlb, nw):
    L = p.shape[0]
    T = min(512, L)
    nt = L // T
    gw = HGRN_HPS * HGRN_DK
    ng = W // gw
    blk = lambda c: pl.BlockSpec((T, gw), lambda h, t, c=c: (t, c * ng + h))
    vec = pl.BlockSpec((1, gw), lambda h, t: (0, h))
    return pl.pallas_call(
        functools.partial(_hgrn_kernel, n_chunks=T // HGRN_CHUNK),
        out_shape=jax.ShapeDtypeStruct((L, W), BF16),
        grid=(ng, nt),
        in_specs=[blk(COL_QA), blk(COL_FA), blk(COL_IA), blk(COL_ZA), vec, vec, vec, vec],
        out_specs=pl.BlockSpec((T, gw), lambda h, t: (t, h)),
        scratch_shapes=[pltpu.VMEM((HGRN_HPS, HGRN_DK, HGRN_DK), F32)],
        compiler_params=_cparams(("arbitrary", "arbitrary")),
    )(p, p, p, p, loglb, log1mlb, omlb, nw)


def _ssd_kernel(x_ref, bc_ref, z_ref, dt_ref, cw_ref, cb_ref, dtb_ref, a_ref, e_ref, d_ref, nw_ref,
                o_ref, xb_ref, st_ref):
    C = M2_CHUNK
    CH = 2 * W

    @pl.when(pl.program_id(0) == 0)
    def _():
        xb_ref[0:SUB, :] = jnp.zeros((SUB, CH), F32)
        st_ref[...] = jnp.zeros_like(st_ref)

    xb_ref[SUB:SUB + C, 0:W] = x_ref[...]
    xb_ref[SUB:SUB + C, W:CH] = bc_ref[...]
    acc = cb_ref[...] + cw_ref[0:1, :] * xb_ref[pl.ds(SUB - 3, C), :]
    for j in range(1, M2_CONV):
        acc = acc + cw_ref[j:j + 1, :] * xb_ref[pl.ds(SUB - 3 + j, C), :]
    xb_ref[0:SUB, :] = xb_ref[C:C + SUB, :]
    xc = _silu(acc)
    xh = xc[:, 0:W]

    dt = jax.nn.softplus(dt_ref[...] + dtb_ref[...])
    da = dt * a_ref[...]
    r_i = lax.broadcasted_iota(jnp.int32, (C, C), 0)
    c_i = lax.broadcasted_iota(jnp.int32, (C, C), 1)
    causal = r_i >= c_i
    causal3 = (lax.broadcasted_iota(jnp.int32, (C, 3 * C), 0)
               >= lax.broadcasted_iota(jnp.int32, (C, 3 * C), 1) % C).astype(BF16)
    a_cs = jnp.dot(causal3, jnp.concatenate(_split(da, 3), axis=0),
                   preferred_element_type=F32)
    a_cs_t = a_cs.T
    e = e_ref[...]
    ea = jnp.exp(a_cs)
    dt_e = _dot_sel(dt, e, 2)
    ea_e = _dot_sel(ea, e, 2)
    te_e = _dot_sel(jnp.exp(a_cs[C - 1:C, :] - a_cs), e, 2)
    xd = xh * dt_e
    xd_end = xd * te_e

    GW = W // M2_GROUPS
    lane_g = lax.broadcasted_iota(jnp.int32, (C, GW), 1)
    groups = range(M2_GROUPS)
    hpg = M2_HEADS // M2_GROUPS
    gsl = [slice(g * GW, (g + 1) * GW) for g in groups]
    c0 = W + M2_GROUPS * M2_STATE
    bg_t = [xc[:, W + g * M2_STATE:W + (g + 1) * M2_STATE].T.astype(BF16) for g in groups]
    cg = [xc[:, c0 + g * M2_STATE:c0 + (g + 1) * M2_STATE].astype(BF16) for g in groups]
    cb = [_dot(cg[g], bg_t[g]) for g in groups]
    y = [_dot(cg[g], st_ref[g]) * ea_e[:, gsl[g]] for g in groups]
    for g in groups:
        ms, xs = [], []
        for j in range(hpg):
            h = g * hpg + j
            seg = a_cs[:, h:h + 1] - a_cs_t[h:h + 1, :]
            ms.append((cb[g] * jnp.exp(jnp.where(causal, seg, -jnp.inf))).astype(BF16))
            xs.append(jnp.where(lane_g // M2_HEADDIM == j, xd[:, gsl[g]], 0.0).astype(BF16))
        y[g] = y[g] + _dot(jnp.concatenate(ms, axis=1), jnp.concatenate(xs, axis=0))
    for g in groups:
        st_ref[g] = st_ref[g] * ea_e[C - 1:C, gsl[g]] + _dot(bg_t[g], xd_end[:, gsl[g]])
    for g in groups:
        yg = (y[g] + d_ref[:, gsl[g]] * xh[:, gsl[g]]) * _silu(z_ref[:, gsl[g]].astype(F32))
        yg = yg * lax.rsqrt(jnp.mean(yg * yg, axis=-1, keepdims=True) + M2_NORM_EPS)
        o_ref[:, gsl[g]] = (yg * nw_ref[:, gsl[g]]).astype(o_ref.dtype)


def _ssd(p, p2, cw, cb, dtb, a, d_e, nw):
    L = p.shape[0]
    e = (jnp.arange(LANE)[:, None] == (jnp.arange(W) // M2_HEADDIM)[None, :]).astype(BF16)
    e = jnp.tile(e, (2, 1))
    C = M2_CHUNK
    at = lambda off: pl.BlockSpec((pl.Element(C), pl.Element(W)), lambda t, off=off: (t * C, off))
    full = lambda a_: pl.BlockSpec(a_.shape, lambda t: (0,) * a_.ndim)
    return pl.pallas_call(
        _ssd_kernel,
        out_shape=jax.ShapeDtypeStruct((L, W), BF16),
        grid=(L // C,),
        in_specs=[at(OFF_XD), at(OFF_BCD), pl.BlockSpec((C, W), lambda t: (t, P2_ZD)),
                  pl.BlockSpec((C, LANE), lambda t: (t, OFF_DT // LANE)),
                  full(cw), full(cb), full(dtb), full(a), full(e), full(d_e), full(nw)],
        out_specs=pl.BlockSpec((C, W), lambda t: (t, 0)),
        scratch_shapes=[pltpu.VMEM((SUB + C + SUB, 2 * W), F32),
                        pltpu.VMEM((M2_GROUPS, M2_STATE, W // M2_GROUPS), F32)],
        compiler_params=_cparams(("arbitrary",)),
    )(p, p, p2, p, cw, cb, dtb, a, e, d_e, nw)


def _rwkv_kernel(r_ref, k_ref, v_ref, wa_ref, z_ref, mu_ref, muwa_ref, w0_ref, w2_ref, a0_ref, a2_ref,
                 kk_ref, ka_ref, rk_ref, lnw_ref, lnb_ref,
                 o_ref, xb_ref, wab_ref, st_ref):
    C = RWKV_CHUNK

    @pl.when(pl.program_id(0) == 0)
    def _():
        xb_ref[0:SUB, :] = jnp.zeros((SUB, 3 * W), F32)
        wab_ref[0:SUB, :] = jnp.zeros((SUB, LANE), F32)
        st_ref[...] = jnp.zeros_like(st_ref)

    xb_ref[SUB:SUB + C, 0:W] = r_ref[...]
    xb_ref[SUB:SUB + C, W:2 * W] = k_ref[...]
    xb_ref[SUB:SUB + C, 2 * W:3 * W] = v_ref[...]
    wab_ref[SUB:SUB + C, :] = wa_ref[...]
    cur = xb_ref[SUB:SUB + C, :]
    xs = cur + mu_ref[...] * (xb_ref[pl.ds(SUB - 1, C), :] - cur)
    wa_cur = wab_ref[SUB:SUB + C, :]
    wa = wa_cur + muwa_ref[...] * (wab_ref[pl.ds(SUB - 1, C), :] - wa_cur)
    xb_ref[0:SUB, :] = xb_ref[C:C + SUB, :]
    wab_ref[0:SUB, :] = wab_ref[C:C + SUB, :]

    logw_all = -RWKV_DECAY_SCALE * jax.nn.sigmoid(w0_ref[...] + _dot(jnp.tanh(wa), w2_ref[...]))
    iclr_all = jax.nn.sigmoid(a0_ref[...] + _dot(wa, a2_ref[...]))

    lane = lax.broadcasted_iota(jnp.int32, (C, LANE), 1)
    lo = lane < RWKV_HEAD

    def hsum(x):
        s_lo = jnp.sum(jnp.where(lo, x, 0.0), axis=-1, keepdims=True)
        s_hi = jnp.sum(jnp.where(lo, 0.0, x), axis=-1, keepdims=True)
        return jnp.where(lo, s_lo, s_hi)
    t_i = lax.broadcasted_iota(jnp.int32, (C, 2 * C), 0)
    s_i = lax.broadcasted_iota(jnp.int32, (C, 2 * C), 1) % C
    strict = t_i > s_i
    incl_uv = (lax.broadcasted_iota(jnp.int32, (C, 4 * C), 0)
               >= lax.broadcasted_iota(jnp.int32, (C, 4 * C), 1) % C)
    r2 = lax.broadcasted_iota(jnp.int32, (2 * C, 2 * C), 0)
    c2 = lax.broadcasted_iota(jnp.int32, (2 * C, 2 * C), 1)
    same_blk = (r2 // RWKV_SUB) == (c2 // RWKV_SUB)
    eye = (r2 == c2).astype(F32)

    def stack(x):
        return jnp.concatenate([jnp.where(lo, x, 0.0), jnp.where(lo, 0.0, x)], axis=0)

    pairs = list(range(W // LANE))
    sls = [slice(i * LANE, (i + 1) * LANE) for i in pairs]
    d = [dict() for _ in pairs]

    def p_keys(i):
        sl = sls[i]
        k = xs[:, W + i * LANE:W + (i + 1) * LANE]
        kk = k * kk_ref[:, sl]
        d[i].update(r=xs[:, sl], v=xs[:, 2 * W + i * LANE:2 * W + (i + 1) * LANE],
                    kk=kk, ss=hsum(kk * kk), k2=k * (1.0 + (iclr_all[:, sl] - 1.0) * ka_ref[:, sl]))

    def p_norm(i):
        d[i]['kk'] = d[i]['kk'] / jnp.maximum(jnp.sqrt(d[i]['ss']), 1e-12)

    def p_decay(i):
        cum = _cumsum_rows(logw_all[:, sls[i]])
        d[i].update(cum=cum, e_inv=jnp.exp(-cum))

    def p_ar(i):
        cum = d[i]['cum']
        d[i]['ar'] = jnp.concatenate([-d[i]['kk'] * jnp.exp(cum - logw_all[:, sls[i]]), d[i]['r'] * jnp.exp(cum)],
                                     axis=0).astype(BF16)

    def p_bk(i):
        e_inv = d[i]['e_inv']
        d[i]['bk'] = jnp.concatenate([stack(d[i]['kk'] * iclr_all[:, sls[i]] * e_inv), stack(d[i]['k2'] * e_inv)],
                                     axis=0).astype(BF16)
        d[i]['v_s'] = stack(d[i]['v']).astype(BF16)

    def s_scores(i):
        d[i]['sc'] = _dot_nt(d[i]['ar'], d[i]['bk'])

    def s_state(i):
        d[i]['ah'] = _dot_nt(d[i]['ar'], st_ref[i])

    def s_w(i):
        sc = d[i]['sc']
        d[i]['w'] = d[i]['ah'][0:C, :] + _dot(jnp.where(strict, sc[0:C, 2 * C:4 * C], 0.0), d[i]['v_s'])

    def s_n(i):
        n_p = stack(jnp.where(strict, d[i]['sc'][0:C, 0:2 * C], 0.0))
        n_d = jnp.where(same_blk, n_p, 0.0)
        d[i].update(n_p=n_p, n_d=n_d, pw=n_d, t_d=eye + n_d)

    def s_pow(i):
        d[i]['pw'] = _dot_solve(d[i]['pw'], d[i]['pw'])
        d[i]['t_d'] = d[i]['t_d'] + _dot_solve(d[i]['pw'], d[i]['t_d'])

    def s_m(i):
        d[i]['m'] = _dot_solve(d[i]['t_d'], d[i]['n_p'] - d[i]['n_d'])
        d[i]['u'] = _dot_solve(d[i]['t_d'], stack(d[i]['w']))

    def s_m2(i):
        d[i]['m2'] = _dot_solve(d[i]['m'], d[i]['m'])

    def s_u1(i):
        d[i]['u'] = d[i]['u'] + _dot_solve(d[i]['m2'], d[i]['u'])

    def s_u2(i):
        d[i]['u'] = d[i]['u'] + _dot_solve(d[i]['m'], d[i]['u'])

    def s_y(i):
        uv = jnp.concatenate([d[i]['u'].astype(BF16), d[i]['v_s']], axis=0)
        d[i]['uv'] = uv
        d[i]['y'] = d[i]['ah'][C:2 * C, :] + _dot(jnp.where(incl_uv, d[i]['sc'][C:2 * C, :], 0.0), uv)

    def s_carry(i):
        st_ref[i] = (st_ref[i] + _dot_tn(d[i]['uv'], d[i]['bk'])) * jnp.exp(d[i]['cum'][C - 1:C, :])

    def e_center(i):
        d[i]['yc'] = d[i]['y'] - hsum(d[i]['y']) * (1.0 / RWKV_HEAD)

    def e_var(i):
        d[i]['var'] = hsum(d[i]['yc'] * d[i]['yc']) * (1.0 / RWKV_HEAD)
        d[i]['bonus'] = hsum(d[i]['r'] * d[i]['k2'] * rk_ref[:, sls[i]]) * d[i]['v']

    def e_out(i):
        sl = sls[i]
        yn = d[i]['yc'] * lax.rsqrt(d[i]['var'] + RWKV_LN_EPS) * lnw_ref[:, sl] + lnb_ref[:, sl]
        o_ref[:, sl] = ((yn + d[i]['bonus']) * _silu(z_ref[:, sl])).astype(o_ref.dtype)

    for stage in (p_keys, p_norm, p_decay, p_ar, p_bk,
                  s_scores, s_state, s_w, s_n, s_pow, s_pow, s_pow, s_m, s_m2, s_u1, s_u2, s_y, s_carry,
                  e_center, e_var, e_out):
        for i in pairs:
            stage(i)


def _rwkv(p, mu, muwa, w0, w2p, a0, a2p, k_k, k_a, r_k, ln_w, ln_b):
    L = p.shape[0]
    C = RWKV_CHUNK
    wide = lambda c: pl.BlockSpec((C, W), lambda t, c=c: (t, c))
    full = lambda a_: pl.BlockSpec(a_.shape, lambda t: (0,) * a_.ndim)
    consts = (mu, muwa, w0, w2p, a0, a2p, k_k, k_a, r_k, ln_w, ln_b)
    return pl.pallas_call(
        _rwkv_kernel,
        out_shape=jax.ShapeDtypeStruct((L, W), BF16),
        grid=(L // C,),
        in_specs=[wide(COL_RC), wide(COL_KC), wide(COL_VC),
                  pl.BlockSpec((C, LANE), lambda t: (t, OFF_WA // LANE)),
                  pl.BlockSpec((pl.Element(C), pl.Element(W)), lambda t: (t * C, OFF_ZC))]
                 + [full(c) for c in consts],
        out_specs=pl.BlockSpec((C, W), lambda t: (t, 0)),
        scratch_shapes=[pltpu.VMEM((SUB + C + SUB, 3 * W), F32),
                        pltpu.VMEM((SUB + C + SUB, LANE), F32),
                        pltpu.VMEM((W // LANE, LANE, LANE), F32)],
        compiler_params=_cparams(("arbitrary",)),
    )(p, p, p, p, p, *consts)


def _s5_kernel(u_ref, perm_ref, m_ref, toe_ref, q_ref, l1_ref, l2_ref, l2s_ref, y_ref,
               xl_ref, xls_ref, xp_ref, car_ref):
    @pl.when(pl.program_id(1) == 0)
    def _():
        car_ref[...] = jnp.zeros_like(car_ref)

    tn = u_ref.shape[0]
    gw = S5_CHUNK * S5_GROUP
    perm = perm_ref[...]
    lhs = jnp.concatenate([u_ref[:, j, :].astype(BF16) for j in range(S5_CHUNK)], axis=1)
    lhs = jnp.dot(lhs, perm, preferred_element_type=F32).astype(BF16)
    lg = [lhs[:, g * gw:(g + 1) * gw] for g in range(S5_GBLK)]
    xl = jnp.concatenate([jnp.dot(lg[g], m_ref[g], preferred_element_type=F32) for g in range(S5_GBLK)],
                         axis=1)
    xl_ref[...] = xl
    xls_ref[...] = jnp.concatenate(
        [pltpu.roll(xl[:, g * LANE:(g + 1) * LANE], S5_STATE, 1) for g in range(S5_GBLK)], axis=1)
    l1, l2, l2s = l1_ref[...], l2_ref[...], l2s_ref[...]

    def step(i, carry):
        x, xs = carry
        row = pl.ds(i, 1)
        xp_ref[row, :] = x
        return (l1 * x + l2 * xs + xl_ref[row, :], l1 * xs + l2s * x + xls_ref[row, :])

    x, xs = lax.fori_loop(0, tn, step, (car_ref[0:1, :], car_ref[1:2, :]))
    car_ref[0:1, :] = x
    car_ref[1:2, :] = xs
    xp = xp_ref[...].astype(BF16)
    y = jnp.concatenate(
        [jnp.dot(lg[g], toe_ref[g], preferred_element_type=F32)
         + jnp.dot(xp[:, g * LANE:(g + 1) * LANE], q_ref[g], preferred_element_type=F32)
         for g in range(S5_GBLK)], axis=1)
    y = lax.dot_general(y.astype(BF16), perm, (((1,), (1,)), ((), ())),
                        preferred_element_type=F32)
    for t in range(S5_CHUNK):
        y_ref[:, t, :] = y[:, t * LANE:(t + 1) * LANE]


def _s5_scan(u3, m, toe, q, l1, l2, l2s):
    n = u3.shape[0]
    tn = min(512, n)
    nb = W // LANE
    sw = S5_GBLK * 2 * S5_STATE
    kw = S5_CHUNK * LANE
    r = jnp.arange(kw)
    dst = (r // S5_GROUP % S5_GBLK) * (S5_CHUNK * S5_GROUP) + (r // LANE) * S5_GROUP + r % S5_GROUP
    perm = (dst[:, None] == jnp.arange(kw)[None, :]).astype(BF16)
    per_b = lambda a_: pl.BlockSpec((S5_GBLK,) + a_.shape[1:], lambda b, i: (b, 0, 0))
    vec = pl.BlockSpec((None, 1, sw), lambda b, i: (b, 0, 0))
    return pl.pallas_call(
        _s5_kernel,
        out_shape=jax.ShapeDtypeStruct((n, S5_CHUNK, W), F32),
        grid=(nb, n // tn),
        in_specs=[pl.BlockSpec((tn, S5_CHUNK, LANE), lambda b, i: (i, 0, b)),
                  pl.BlockSpec((kw, kw), lambda b, i: (0, 0)),
                  per_b(m), per_b(toe), per_b(q), vec, vec, vec],
        out_specs=pl.BlockSpec((tn, S5_CHUNK, LANE), lambda b, i: (i, 0, b)),
        scratch_shapes=[pltpu.VMEM((tn, sw), F32)] * 3 + [pltpu.VMEM((SUB, sw), F32)],
        compiler_params=_cparams(("arbitrary", "arbitrary")),
    )(u3, perm, m, toe, q, l1, l2, l2s)


def _s5_out_kernel(y_ref, u_ref, z_ref, d_ref, wg_ref, bg_ref, o_ref):
    y = y_ref[...].reshape(u_ref.shape) + d_ref[...] * u_ref[...]
    g = jax.nn.gelu(y)
    o = g * jax.nn.sigmoid(_dot(g, wg_ref[...]) + bg_ref[...])
    o_ref[...] = (o * _silu(z_ref[...])).astype(o_ref.dtype)


def _s5_out(y3, p, d, wg, bg):
    L = p.shape[0]
    tm = min(512, L)
    vec = pl.BlockSpec((1, W), lambda t: (0, 0))
    return pl.pallas_call(
        _s5_out_kernel,
        out_shape=jax.ShapeDtypeStruct((L, W), BF16),
        grid=(L // tm,),
        in_specs=[pl.BlockSpec((tm // S5_CHUNK, S5_CHUNK, W), lambda t: (t, 0, 0)),
                  pl.BlockSpec((tm, W), lambda t: (t, COL_UB)),
                  pl.BlockSpec((tm, W), lambda t: (t, COL_ZB)),
                  vec, pl.BlockSpec((W, W), lambda t: (0, 0)), vec],
        out_specs=pl.BlockSpec((tm, W), lambda t: (t, 0)),
        compiler_params=_cparams(("parallel",)),
    )(y3, p, p, d, wg, bg)


def _s5_tables(a_re, a_im, log_dt, b_re, b_im, c_re, c_im):
    J = S5_CHUNK
    lam_re = jnp.minimum(a_re, S5_MAX_RE)
    lam_im = a_im
    dt = jnp.exp(log_dt)[:, None]
    dl_re, dl_im = lam_re * dt, lam_im * dt
    mag = jnp.exp(dl_re)
    num_re = mag * jnp.cos(dl_im) - 1.0
    num_im = mag * jnp.sin(dl_im)
    den = lam_re * lam_re + lam_im * lam_im
    coef_re = (num_re * lam_re + num_im * lam_im) / den
    coef_im = (num_im * lam_re - num_re * lam_im) / den
    bb_re = coef_re[..., None] * b_re - coef_im[..., None] * b_im
    bb_im = coef_re[..., None] * b_im + coef_im[..., None] * b_re
    tau = jnp.arange(J + 1, dtype=F32)[:, None, None]
    pm = jnp.exp(tau * dl_re)
    pw_re, pw_im = pm * jnp.cos(tau * dl_im), pm * jnp.sin(tau * dl_im)
    pb_re = pw_re[..., None] * bb_re - pw_im[..., None] * bb_im
    pb_im = pw_re[..., None] * bb_im + pw_im[..., None] * bb_re
    kern = (jnp.einsum('gop,tgph->tgoh', c_re, pb_re[:J], precision=HI)
            - jnp.einsum('gop,tgph->tgoh', c_im, pb_im[:J], precision=HI))
    t_idx = jnp.arange(J)
    lag = t_idx[None, :] - t_idx[:, None]
    onehot = (lag[:, :, None] == t_idx[None, None, :]).astype(F32)
    toe = jnp.einsum('stu,ugoh->stgoh', onehot, kern, precision=HI)
    nb, gb = S5_GROUPS // S5_GBLK, S5_GBLK
    toe = toe.transpose(2, 0, 4, 1, 3).reshape(S5_GROUPS, J * S5_GROUP, J * S5_GROUP)
    rev_re, rev_im = pb_re[J - 1 - t_idx], pb_im[J - 1 - t_idx]
    to_m = lambda x: x.transpose(1, 0, 3, 2).reshape(S5_GROUPS, J * S5_GROUP, S5_STATE)
    m = jnp.concatenate([to_m(rev_re), to_m(rev_im)], axis=-1)
    q_re = (c_re[None] * pw_re[1:, :, None, :] - c_im[None] * pw_im[1:, :, None, :])
    q_im = -(c_re[None] * pw_im[1:, :, None, :] + c_im[None] * pw_re[1:, :, None, :])
    to_q = lambda x: x.transpose(1, 3, 0, 2).reshape(S5_GROUPS, S5_STATE, J * S5_GROUP)
    q = jnp.concatenate([to_q(q_re), to_q(q_im)], axis=1)
    lj_re, lj_im = pw_re[J], pw_im[J]
    rs = lambda parts: jnp.concatenate(parts, axis=-1).reshape(nb, 1, gb * 2 * S5_STATE)
    return (m.astype(BF16), toe.astype(BF16), q.astype(BF16),
            rs([lj_re, lj_re]), rs([-lj_im, lj_im]), rs([lj_im, -lj_im]))


def _s5(u3, p, tables, d, wg, bg):
    return _s5_out(_s5_scan(u3, *tables), p, d, wg, bg)


def _up_kernel(oa_ref, ob_ref, oc_ref, od_ref, ga_ref, gb_ref, gc_ref, gd_ref, w_ref, bias_ref, o_ref):
    acc = None
    for b, (o_b, g_b) in enumerate(((oa_ref, ga_ref), (ob_ref, gb_ref), (oc_ref, gc_ref), (od_ref, gd_ref))):
        gate = jax.nn.sigmoid(g_b[...] + bias_ref[b:b + 1, :])
        contrib = gate * jnp.dot(o_b[...], w_ref[b], preferred_element_type=F32)
        acc = contrib if acc is None else acc + contrib
    o_ref[...] = acc.astype(o_ref.dtype)


def _up(outs, p, w_up, bias):
    L = p.shape[0]
    tm = min(1024, L)
    tn = 512
    nb = D_MODEL // tn
    o_spec = pl.BlockSpec((tm, W), lambda i, j: (i, 0))
    g_spec = lambda b: pl.BlockSpec((tm, tn), lambda i, j, b=b: (i, (P2_GATE * W + b * D_MODEL) // tn + j))
    return pl.pallas_call(
        _up_kernel,
        out_shape=jax.ShapeDtypeStruct((L, D_MODEL), BF16),
        grid=(L // tm, nb),
        in_specs=[o_spec] * 4 + [g_spec(b) for b in range(N_BRANCH)]
                 + [pl.BlockSpec((N_BRANCH, W, tn), lambda i, j: (0, 0, j)),
                    pl.BlockSpec((N_BRANCH, tn), lambda i, j: (0, j))],
        out_specs=pl.BlockSpec((tm, tn), lambda i, j: (i, j)),
        compiler_params=_cparams(("parallel", "arbitrary")),
    )(*outs, p, p, p, p, w_up, bias)


def _out_kernel(m_ref, w_ref, g_ref, res_ref, o_ref):
    y = jnp.dot(m_ref[...], w_ref[...], preferred_element_type=F32)
    o_ref[...] = res_ref[...] + (y * lax.rsqrt(jnp.mean(y * y, axis=-1, keepdims=True) + NORM_EPS)
                                 * g_ref[...])


def _out(merged, w_out, g, res):
    L = res.shape[1]
    tm = min(512, L)
    return pl.pallas_call(
        _out_kernel,
        out_shape=jax.ShapeDtypeStruct((1, L, D_MODEL), F32),
        grid=(L // tm,),
        in_specs=[pl.BlockSpec((tm, D_MODEL), lambda i: (i, 0)),
                  pl.BlockSpec((D_MODEL, D_MODEL), lambda i: (0, 0)),
                  pl.BlockSpec((1, D_MODEL), lambda i: (0, 0)),
                  pl.BlockSpec((None, tm, D_MODEL), lambda i: (0, i, 0))],
        out_specs=pl.BlockSpec((None, tm, D_MODEL), lambda i: (0, i, 0)),
        compiler_params=_cparams(("parallel",)),
    )(merged, w_out, g.reshape(1, D_MODEL), res)


def _pack_w_in(w):
    wb = w.astype(BF16)
    return wb, wb[:, OFF_P2:OFF_P2 + P2_COLS]


def _row(v):
    return v.astype(F32).reshape(1, -1)


def kernel(x, norm_pre, norm_post, w_in, gate_bias, w_up, w_out, hgrn_lb_logits, hgrn_norm, s5_a_re, s5_a_im, s5_log_dt, s5_b_re, s5_b_im, s5_c_re, s5_c_im, s5_d, s5_w_glu, s5_b_glu, rwkv_mu, rwkv_w0, rwkv_w2, rwkv_a0, rwkv_a2, rwkv_k_k, rwkv_k_a, rwkv_r_k, rwkv_ln_w, rwkv_ln_b, m2_conv_w, m2_conv_b, m2_dt_bias, m2_a_log, m2_d, m2_norm):
    depth = w_in.shape[0]
    L = x.shape[1]
    lb_all = jnp.cumsum(jax.nn.softmax(hgrn_lb_logits.astype(F32), axis=0), axis=0)
    lb_all = lb_all - lb_all[0:1]

    zpad = jnp.zeros((RWKV_LORA, W), F32)
    hpad = jnp.zeros((1, LANE - M2_HEADS), F32)

    res = x.astype(F32)
    for l in range(depth):
        h = _prenorm(res, norm_pre[l])
        w1, w2 = _pack_w_in(w_in[l])
        p, u3 = _inproj(h, w1)
        p2 = _matmul(h, w2, P2_TN, BF16)

        lb = lb_all[l]
        o_a = _hgrn(p, _row(jnp.log(lb)), _row(jnp.log1p(-lb)), _row(1.0 - lb), _row(hgrn_norm[l]))

        tables = _s5_tables(s5_a_re[l].astype(F32), s5_a_im[l].astype(F32), s5_log_dt[l].astype(F32),
                            s5_b_re[l].astype(F32), s5_b_im[l].astype(F32),
                            s5_c_re[l].astype(F32), s5_c_im[l].astype(F32))
        o_b = _s5(u3, p, tables, _row(s5_d[l]), s5_w_glu[l].astype(BF16), _row(s5_b_glu[l]))

        mu = rwkv_mu[l].astype(F32)
        o_c = _rwkv(p, _row(mu[:3 * W]), _row(mu[3 * W:]), _row(rwkv_w0[l]),
                    jnp.concatenate([rwkv_w2[l].astype(F32), zpad], axis=0).astype(BF16),
                    _row(rwkv_a0[l]),
                    jnp.concatenate([zpad, rwkv_a2[l].astype(F32)], axis=0).astype(BF16),
                    _row(rwkv_k_k[l]), _row(rwkv_k_a[l]), _row(rwkv_r_k[l]),
                    _row(rwkv_ln_w[l]), _row(rwkv_ln_b[l]))

        o_d = _ssd(p, p2, m2_conv_w[l].astype(F32), _row(m2_conv_b[l]),
                   jnp.concatenate([_row(m2_dt_bias[l]), hpad], axis=1),
                   jnp.concatenate([-jnp.exp(_row(m2_a_log[l])), hpad], axis=1),
                   _row(jnp.repeat(m2_d[l].astype(F32), M2_HEADDIM)), _row(m2_norm[l]))

        merged = _up((o_a, o_b, o_c, o_d), p2, w_up[l].astype(BF16), gate_bias[l].astype(F32))
        res = _out(merged, w_out[l].astype(BF16), norm_post[l].astype(F32), res)
    return res.astype(x.dtype)
```

```python
import functools

import jax
import jax.numpy as jnp
from jax import lax
from jax.experimental import pallas as pl
from jax.experimental.pallas import tpu as pltpu

F32 = jnp.float32
BF16 = jnp.bfloat16
HI = lax.Precision.HIGHEST
LOG2E = 1.4426950408889634

D_MODEL = 2048
W = D_MODEL // 2
N_BRANCH = 4
NORM_EPS = 1e-6

HGRN_HEADS = 8
HGRN_DK = 128
HGRN_CHUNK = 64
HGRN_SUB = 8
HGRN_HPS = 8

S5_GROUP = 16
S5_GROUPS = 64
S5_STATE = 64
S5_CHUNK = 16
S5_MAX_RE = -1e-4
S5_GBLK = 8

RWKV_HEAD = 64
RWKV_LORA = 64
RWKV_CHUNK = 64
RWKV_SUB = 16
RWKV_DECAY_SCALE = 0.606531
RWKV_LN_EPS = 64e-5

M2_HEADS = 16
M2_HEADDIM = 64
M2_GROUPS = 4
M2_STATE = 128
M2_CONV = 4
M2_CHUNK = 128
M2_NORM_EPS = 1e-5

LANE = 128
SUB = 8
VMEM_LIMIT = 56 * 1024 * 1024

COL_QA, COL_FA, COL_IA, COL_ZA = 0, 1, 2, 3
COL_UB, COL_ZB = 4, 5
COL_RC, COL_KC, COL_VC = 6, 7, 8
OFF_WA = 9 * W
OFF_ZC = OFF_WA + 2 * RWKV_LORA
OFF_XD = OFF_ZC + W
OFF_BCD = OFF_XD + W
OFF_DT = OFF_BCD + 2 * M2_GROUPS * M2_STATE
P1_COLS = OFF_DT + LANE
OFF_P2 = OFF_DT + M2_HEADS
P2_ZD, P2_GATE = 0, 1
P2_COLS = W + N_BRANCH * D_MODEL
P1_TN = 1792
P2_TN = 2304
U_TILE = (COL_UB * 1024) // P1_TN
U_OFF = COL_UB * 1024 - U_TILE * P1_TN


def _cparams(sem):
    return pltpu.CompilerParams(dimension_semantics=sem, vmem_limit_bytes=VMEM_LIMIT)


def _dot(a, b):
    return jnp.dot(a.astype(BF16), b.astype(BF16), preferred_element_type=F32)


def _dot_nt(a, b):
    return lax.dot_general(a.astype(BF16), b.astype(BF16), (((1,), (1,)), ((), ())),
                           preferred_element_type=F32)


def _dot_tn(a, b):
    return lax.dot_general(a.astype(BF16), b.astype(BF16), (((0,), (0,)), ((), ())),
                           preferred_element_type=F32)


def _dot_solve(a, b):
    return _dot(a, b)


def _split(x, n):
    parts = []
    for _ in range(n):
        part = x.astype(BF16)
        parts.append(part)
        x = x - part.astype(F32)
    return parts


def _dot_sel(x, sel_tiled, n):
    return jnp.dot(jnp.concatenate(_split(x, n), axis=1), sel_tiled, preferred_element_type=F32)


def _silu(x):
    return x * jax.nn.sigmoid(x)


def _cumsum_rows(x):
    n = x.shape[0]
    row = lax.broadcasted_iota(jnp.int32, x.shape, 0)
    s = 1
    while s < n:
        x = x + jnp.where(row >= s, pltpu.roll(x, s, 0), 0.0)
        s *= 2
    return x


def _prenorm_kernel(x_ref, g_ref, o_ref):
    x = x_ref[...]
    o_ref[...] = (x * lax.rsqrt(jnp.mean(x * x, axis=-1, keepdims=True) + NORM_EPS)
                  * g_ref[...]).astype(o_ref.dtype)


def _prenorm(x, g):
    L = x.shape[1]
    tm = min(512, L)
    return pl.pallas_call(
        _prenorm_kernel,
        out_shape=jax.ShapeDtypeStruct((L, D_MODEL), BF16),
        grid=(L // tm,),
        in_specs=[pl.BlockSpec((None, tm, D_MODEL), lambda i: (0, i, 0)),
                  pl.BlockSpec((1, D_MODEL), lambda i: (0, 0))],
        out_specs=pl.BlockSpec((tm, D_MODEL), lambda i: (i, 0)),
        compiler_params=_cparams(("parallel",)),
    )(x, g.reshape(1, D_MODEL))


def _inproj_kernel(a_ref, w_ref, o_ref, u3_ref):
    acc = jnp.dot(a_ref[...], w_ref[...], preferred_element_type=F32)
    o_ref[...] = acc

    @pl.when(pl.program_id(1) == U_TILE)
    def _():
        u3_ref[...] = acc[:, U_OFF:U_OFF + W].reshape(u3_ref.shape)


def _inproj(a, w):
    M, K = a.shape
    tm = min(1024, M)
    return pl.pallas_call(
        _inproj_kernel,
        out_shape=(jax.ShapeDtypeStruct((M, P1_COLS), F32),
                   jax.ShapeDtypeStruct((M // S5_CHUNK, S5_CHUNK, W), F32)),
        grid=(M // tm, P1_COLS // P1_TN),
        in_specs=[pl.BlockSpec((tm, K), lambda i, j: (i, 0)),
                  pl.BlockSpec((K, P1_TN), lambda i, j: (0, j))],
        out_specs=(pl.BlockSpec((tm, P1_TN), lambda i, j: (i, j)),
                   pl.BlockSpec((tm // S5_CHUNK, S5_CHUNK, W), lambda i, j: (i, 0, 0))),
        compiler_params=_cparams(("parallel", "arbitrary")),
    )(a, w)


def _mm_kernel(a_ref, w_ref, o_ref):
    o_ref[...] = jnp.dot(a_ref[...], w_ref[...], preferred_element_type=F32).astype(o_ref.dtype)


def _matmul(a, w, tn, out_dtype):
    M, K = a.shape
    N = w.shape[1]
    tm = min(1024, M)
    return pl.pallas_call(
        _mm_kernel,
        out_shape=jax.ShapeDtypeStruct((M, N), out_dtype),
        grid=(M // tm, N // tn),
        in_specs=[pl.BlockSpec((tm, K), lambda i, j: (i, 0)),
                  pl.BlockSpec((K, tn), lambda i, j: (0, j))],
        out_specs=pl.BlockSpec((tm, tn), lambda i, j: (i, j)),
        compiler_params=_cparams(("parallel", "parallel")),
    )(a, w)


def _hgrn_kernel(q_ref, f_ref, i_ref, z_ref, loglb_ref, log1mlb_ref, omlb_ref, nw_ref,
                 o_ref, st_ref, *, n_chunks):
    @pl.when(pl.program_id(1) == 0)
    def _():
        st_ref[...] = jnp.zeros_like(st_ref)

    C, SB, DK = HGRN_CHUNK, HGRN_SUB, HGRN_DK
    NB = C // SB
    heads = range(HGRN_HPS)
    hs = [slice(j * DK, (j + 1) * DK) for j in heads]
    loglb, log1mlb, omlb, nw = loglb_ref[...], log1mlb_ref[...], omlb_ref[...], nw_ref[...]
    trow = lax.broadcasted_iota(jnp.int32, (SB, DK), 0)
    col_c = lax.broadcasted_iota(jnp.int32, (SB, C), 1)
    ones = jnp.ones((DK, DK), BF16)

    def chunk(c, carry):
        rows = pl.ds(pl.multiple_of(c * C, C), C)
        f = f_ref[rows, :]
        log_sig = jnp.minimum(f, 0.0) - jnp.log(1.0 + jnp.exp(-jnp.abs(f)))
        b = log1mlb + log_sig
        logf = jnp.maximum(loglb, b) + jnp.log(1.0 + jnp.exp(-jnp.abs(loglb - b)))
        cum_all = _cumsum_rows(logf) * LOG2E
        k_all = omlb * jax.nn.sigmoid(-f)
        q_all = _silu(q_ref[rows, :]) * (DK ** -0.5)
        cum = [cum_all[:, s_] for s_ in hs]
        k = [k_all[:, s_] for s_ in hs]
        q = [q_all[:, s_] for s_ in hs]
        v = [i_ref[rows, s_].astype(BF16) for s_ in hs]
        st = [st_ref[j] for j in heads]
        o = [_dot_nt(q[j] * jnp.exp2(cum[j]), st[j]) for j in heads]

        sc = [[None] * NB for _ in heads]
        for bi in range(1, NB):
            r0 = bi * SB
            for j in heads:
                ref = cum[j][r0:r0 + 1, :]
                kmod = k[j][0:r0, :] * jnp.exp2(ref - cum[j][0:r0, :])
                kmod = jnp.concatenate([kmod, jnp.zeros((C - r0, DK), F32)], axis=0)
                sc[j][bi] = _dot_nt(q[j][r0:r0 + SB, :] * jnp.exp2(cum[j][r0:r0 + SB, :] - ref), kmod)
        psum = []
        for j in heads:
            ps = []
            for bi in range(NB):
                r0 = bi * SB
                for s in range(SB):
                    d = jnp.where(trow >= s, cum[j][r0:r0 + SB, :] - cum[j][r0 + s:r0 + s + 1, :], -jnp.inf)
                    ps.append((q[j][r0:r0 + SB, :] * k[j][r0 + s:r0 + s + 1, :] * jnp.exp2(d)).astype(BF16))
            psum.append(jnp.dot(jnp.concatenate(ps, axis=0), ones, preferred_element_type=F32))
        for j in heads:
            blocks = []
            for bi in range(NB):
                acc = jnp.zeros((SB, C), F32) if bi == 0 else sc[j][bi]
                for s in range(SB):
                    r1 = (bi * SB + s) * SB
                    acc = jnp.where(col_c == bi * SB + s, psum[j][r1:r1 + SB, 0:C], acc)
                blocks.append(acc)
            o[j] = o[j] + _dot(jnp.concatenate(blocks, axis=0), v[j])

        for j in heads:
            last = cum[j][C - 1:C, :]
            st_ref[j] = st[j] * jnp.exp2(last) + _dot_tn(v[j], k[j] * jnp.exp2(last - cum[j]))
        for j in heads:
            oj = o[j] * lax.rsqrt(jnp.mean(o[j] * o[j], axis=-1, keepdims=True) + NORM_EPS) * nw[:, hs[j]]
            o_ref[rows, hs[j]] = (oj * _silu(z_ref[rows, hs[j]])).astype(o_ref.dtype)
        return carry

    lax.fori_loop(0, n_chunks, chunk, 0)


def _hgrn(p, loglb, log1mlb, omlb, nw):
    L = p.shape[0]
    T = min(512, L)
    nt = L // T
    gw = HGRN_HPS * HGRN_DK
    ng = W // gw
    blk = lambda c: pl.BlockSpec((T, gw), lambda h, t, c=c: (t, c * ng + h))
    vec = pl.BlockSpec((1, gw), lambda h, t: (0, h))
    return pl.pallas_call(
        functools.partial(_hgrn_kernel, n_chunks=T // HGRN_CHUNK),
        out_shape=jax.ShapeDtypeStruct((L, W), BF16),
        grid=(ng, nt),
        in_specs=[blk(COL_QA), blk(COL_FA), blk(COL_IA), blk(COL_ZA), vec, vec, vec, vec],
        out_specs=pl.BlockSpec((T, gw), lambda h, t: (t, h)),
        scratch_shapes=[pltpu.VMEM((HGRN_HPS, HGRN_DK, HGRN_DK), F32)],
        compiler_params=_cparams(("arbitrary", "arbitrary")),
    )(p, p, p, p, loglb, log1mlb, omlb, nw)


def _ssd_kernel(x_ref, bc_ref, z_ref, dt_ref, cw_ref, cb_ref, dtb_ref, a_ref, e_ref, d_ref, nw_ref,
                o_ref, xb_ref, st_ref):
    C = M2_CHUNK
    CH = 2 * W

    @pl.when(pl.program_id(0) == 0)
    def _():
        xb_ref[0:SUB, :] = jnp.zeros((SUB, CH), F32)
        st_ref[...] = jnp.zeros_like(st_ref)

    xb_ref[SUB:SUB + C, 0:W] = x_ref[...]
    xb_ref[SUB:SUB + C, W:CH] = bc_ref[...]
    acc = cb_ref[...] + cw_ref[0:1, :] * xb_ref[pl.ds(SUB - 3, C), :]
    for j in range(1, M2_CONV):
        acc = acc + cw_ref[j:j + 1, :] * xb_ref[pl.ds(SUB - 3 + j, C), :]
    xb_ref[0:SUB, :] = xb_ref[C:C + SUB, :]
    xc = _silu(acc)
    xh = xc[:, 0:W]

    dt = jax.nn.softplus(dt_ref[...] + dtb_ref[...])
    da = dt * a_ref[...]
    r_i = lax.broadcasted_iota(jnp.int32, (C, C), 0)
    c_i = lax.broadcasted_iota(jnp.int32, (C, C), 1)
    causal = r_i >= c_i
    causal3 = (lax.broadcasted_iota(jnp.int32, (C, 3 * C), 0)
               >= lax.broadcasted_iota(jnp.int32, (C, 3 * C), 1) % C).astype(BF16)
    a_cs = jnp.dot(causal3, jnp.concatenate(_split(da, 3), axis=0),
                   preferred_element_type=F32)
    a_cs_t = a_cs.T
    e = e_ref[...]
    ea = jnp.exp(a_cs)
    dt_e = _dot_sel(dt, e, 2)
    ea_e = _dot_sel(ea, e, 2)
    te_e = _dot_sel(jnp.exp(a_cs[C - 1:C, :] - a_cs), e, 2)
    xd = xh * dt_e
    xd_end = xd * te_e

    GW = W // M2_GROUPS
    lane_g = lax.broadcasted_iota(jnp.int32, (C, GW), 1)
    groups = range(M2_GROUPS)
    hpg = M2_HEADS // M2_GROUPS
    gsl = [slice(g * GW, (g + 1) * GW) for g in groups]
    c0 = W + M2_GROUPS * M2_STATE
    bg_t = [xc[:, W + g * M2_STATE:W + (g + 1) * M2_STATE].T.astype(BF16) for g in groups]
    cg = [xc[:, c0 + g * M2_STATE:c0 + (g + 1) * M2_STATE].astype(BF16) for g in groups]
    cb = [_dot(cg[g], bg_t[g]) for g in groups]
    y = [_dot(cg[g], st_ref[g]) * ea_e[:, gsl[g]] for g in groups]
    for g in groups:
        ms, xs = [], []
        for j in range(hpg):
            h = g * hpg + j
            seg = a_cs[:, h:h + 1] - a_cs_t[h:h + 1, :]
            ms.append((cb[g] * jnp.exp(jnp.where(causal, seg, -jnp.inf))).astype(BF16))
            xs.append(jnp.where(lane_g // M2_HEADDIM == j, xd[:, gsl[g]], 0.0).astype(BF16))
        y[g] = y[g] + _dot(jnp.concatenate(ms, axis=1), jnp.concatenate(xs, axis=0))
    for g in groups:
        st_ref[g] = st_ref[g] * ea_e[C - 1:C, gsl[g]] + _dot(bg_t[g], xd_end[:, gsl[g]])
    for g in groups:
        yg = (y[g] + d_ref[:, gsl[g]] * xh[:, gsl[g]]) * _silu(z_ref[:, gsl[g]].astype(F32))
        yg = yg * lax.rsqrt(jnp.mean(yg * yg, axis=-1, keepdims=True) + M2_NORM_EPS)
        o_ref[:, gsl[g]] = (yg * nw_ref[:, gsl[g]]).astype(o_ref.dtype)


def _ssd(p, p2, cw, cb, dtb, a, d_e, nw):
    L = p.shape[0]
    e = (jnp.arange(LANE)[:, None] == (jnp.arange(W) // M2_HEADDIM)[None, :]).astype(BF16)
    e = jnp.tile(e, (2, 1))
    C = M2_CHUNK
    at = lambda off: pl.BlockSpec((pl.Element(C), pl.Element(W)), lambda t, off=off: (t * C, off))
    full = lambda a_: pl.BlockSpec(a_.shape, lambda t: (0,) * a_.ndim)
    return pl.pallas_call(
        _ssd_kernel,
        out_shape=jax.ShapeDtypeStruct((L, W), BF16),
        grid=(L // C,),
        in_specs=[at(OFF_XD), at(OFF_BCD), pl.BlockSpec((C, W), lambda t: (t, P2_ZD)),
                  pl.BlockSpec((C, LANE), lambda t: (t, OFF_DT // LANE)),
                  full(cw), full(cb), full(dtb), full(a), full(e), full(d_e), full(nw)],
        out_specs=pl.BlockSpec((C, W), lambda t: (t, 0)),
        scratch_shapes=[pltpu.VMEM((SUB + C + SUB, 2 * W), F32),
                        pltpu.VMEM((M2_GROUPS, M2_STATE, W // M2_GROUPS), F32)],
        compiler_params=_cparams(("arbitrary",)),
    )(p, p, p2, p, cw, cb, dtb, a, e, d_e, nw)


def _rwkv_kernel(r_ref, k_ref, v_ref, wa_ref, z_ref, mu_ref, muwa_ref, w0_ref, w2_ref, a0_ref, a2_ref,
                 kk_ref, ka_ref, rk_ref, lnw_ref, lnb_ref,
                 o_ref, xb_ref, wab_ref, st_ref):
    C = RWKV_CHUNK

    @pl.when(pl.program_id(0) == 0)
    def _():
        xb_ref[0:SUB, :] = jnp.zeros((SUB, 3 * W), F32)
        wab_ref[0:SUB, :] = jnp.zeros((SUB, LANE), F32)
        st_ref[...] = jnp.zeros_like(st_ref)

    xb_ref[SUB:SUB + C, 0:W] = r_ref[...]
    xb_ref[SUB:SUB + C, W:2 * W] = k_ref[...]
    xb_ref[SUB:SUB + C, 2 * W:3 * W] = v_ref[...]
    wab_ref[SUB:SUB + C, :] = wa_ref[...]
    cur = xb_ref[SUB:SUB + C, :]
    xs = cur + mu_ref[...] * (xb_ref[pl.ds(SUB - 1, C), :] - cur)
    wa_cur = wab_ref[SUB:SUB + C, :]
    wa = wa_cur + muwa_ref[...] * (wab_ref[pl.ds(SUB - 1, C), :] - wa_cur)
    xb_ref[0:SUB, :] = xb_ref[C:C + SUB, :]
    wab_ref[0:SUB, :] = wab_ref[C:C + SUB, :]

    logw_all = -RWKV_DECAY_SCALE * jax.nn.sigmoid(w0_ref[...] + _dot(jnp.tanh(wa), w2_ref[...]))
    iclr_all = jax.nn.sigmoid(a0_ref[...] + _dot(wa, a2_ref[...]))

    lane = lax.broadcasted_iota(jnp.int32, (C, LANE), 1)
    lo = lane < RWKV_HEAD

    def hsum(x):
        s_lo = jnp.sum(jnp.where(lo, x, 0.0), axis=-1, keepdims=True)
        s_hi = jnp.sum(jnp.where(lo, 0.0, x), axis=-1, keepdims=True)
        return jnp.where(lo, s_lo, s_hi)
    t_i = lax.broadcasted_iota(jnp.int32, (C, 2 * C), 0)
    s_i = lax.broadcasted_iota(jnp.int32, (C, 2 * C), 1) % C
    strict = t_i > s_i
    incl_uv = (lax.broadcasted_iota(jnp.int32, (C, 4 * C), 0)
               >= lax.broadcasted_iota(jnp.int32, (C, 4 * C), 1) % C)
    r2 = lax.broadcasted_iota(jnp.int32, (2 * C, 2 * C), 0)
    c2 = lax.broadcasted_iota(jnp.int32, (2 * C, 2 * C), 1)
    same_blk = (r2 // RWKV_SUB) == (c2 // RWKV_SUB)
    eye = (r2 == c2).astype(F32)

    def stack(x):
        return jnp.concatenate([jnp.where(lo, x, 0.0), jnp.where(lo, 0.0, x)], axis=0)

    pairs = list(range(W // LANE))
    sls = [slice(i * LANE, (i + 1) * LANE) for i in pairs]
    d = [dict() for _ in pairs]

    def p_keys(i):
        sl = sls[i]
        k = xs[:, W + i * LANE:W + (i + 1) * LANE]
        kk = k * kk_ref[:, sl]
        d[i].update(r=xs[:, sl], v=xs[:, 2 * W + i * LANE:2 * W + (i + 1) * LANE],
                    kk=kk, ss=hsum(kk * kk), k2=k * (1.0 + (iclr_all[:, sl] - 1.0) * ka_ref[:, sl]))

    def p_norm(i):
        d[i]['kk'] = d[i]['kk'] / jnp.maximum(jnp.sqrt(d[i]['ss']), 1e-12)

    def p_decay(i):
        cum = _cumsum_rows(logw_all[:, sls[i]])
        d[i].update(cum=cum, e_inv=jnp.exp(-cum))

    def p_ar(i):
        cum = d[i]['cum']
        d[i]['ar'] = jnp.concatenate([-d[i]['kk'] * jnp.exp(cum - logw_all[:, sls[i]]), d[i]['r'] * jnp.exp(cum)],
                                     axis=0).astype(BF16)

    def p_bk(i):
        e_inv = d[i]['e_inv']
        d[i]['bk'] = jnp.concatenate([stack(d[i]['kk'] * iclr_all[:, sls[i]] * e_inv), stack(d[i]['k2'] * e_inv)],
                                     axis=0).astype(BF16)
        d[i]['v_s'] = stack(d[i]['v']).astype(BF16)

    def s_scores(i):
        d[i]['sc'] = _dot_nt(d[i]['ar'], d[i]['bk'])

    def s_state(i):
        d[i]['ah'] = _dot_nt(d[i]['ar'], st_ref[i])

    def s_w(i):
        sc = d[i]['sc']
        d[i]['w'] = d[i]['ah'][0:C, :] + _dot(jnp.where(strict, sc[0:C, 2 * C:4 * C], 0.0), d[i]['v_s'])

    def s_n(i):
        n_p = stack(jnp.where(strict, d[i]['sc'][0:C, 0:2 * C], 0.0))
        n_d = jnp.where(same_blk, n_p, 0.0)
        d[i].update(n_p=n_p, n_d=n_d, pw=n_d, t_d=eye + n_d)

    def s_pow(i):
        d[i]['pw'] = _dot_solve(d[i]['pw'], d[i]['pw'])
        d[i]['t_d'] = d[i]['t_d'] + _dot_solve(d[i]['pw'], d[i]['t_d'])

    def s_m(i):
        d[i]['m'] = _dot_solve(d[i]['t_d'], d[i]['n_p'] - d[i]['n_d'])
        d[i]['u'] = _dot_solve(d[i]['t_d'], stack(d[i]['w']))

    def s_m2(i):
        d[i]['m2'] = _dot_solve(d[i]['m'], d[i]['m'])

    def s_u1(i):
        d[i]['u'] = d[i]['u'] + _dot_solve(d[i]['m2'], d[i]['u'])

    def s_u2(i):
        d[i]['u'] = d[i]['u'] + _dot_solve(d[i]['m'], d[i]['u'])

    def s_y(i):
        uv = jnp.concatenate([d[i]['u'].astype(BF16), d[i]['v_s']], axis=0)
        d[i]['uv'] = uv
        d[i]['y'] = d[i]['ah'][C:2 * C, :] + _dot(jnp.where(incl_uv, d[i]['sc'][C:2 * C, :], 0.0), uv)

    def s_carry(i):
        st_ref[i] = (st_ref[i] + _dot_tn(d[i]['uv'], d[i]['bk'])) * jnp.exp(d[i]['cum'][C - 1:C, :])

    def e_center(i):
        d[i]['yc'] = d[i]['y'] - hsum(d[i]['y']) * (1.0 / RWKV_HEAD)

    def e_var(i):
        d[i]['var'] = hsum(d[i]['yc'] * d[i]['yc']) * (1.0 / RWKV_HEAD)
        d[i]['bonus'] = hsum(d[i]['r'] * d[i]['k2'] * rk_ref[:, sls[i]]) * d[i]['v']

    def e_out(i):
        sl = sls[i]
        yn = d[i]['yc'] * lax.rsqrt(d[i]['var'] + RWKV_LN_EPS) * lnw_ref[:, sl] + lnb_ref[:, sl]
        o_ref[:, sl] = ((yn + d[i]['bonus']) * _silu(z_ref[:, sl])).astype(o_ref.dtype)

    for stage in (p_keys, p_norm, p_decay, p_ar, p_bk,
                  s_scores, s_state, s_w, s_n, s_pow, s_pow, s_pow, s_m, s_m2, s_u1, s_u2, s_y, s_carry,
                  e_center, e_var, e_out):
        for i in pairs:
            stage(i)


def _rwkv(p, mu, muwa, w0, w2p, a0, a2p, k_k, k_a, r_k, ln_w, ln_b):
    L = p.shape[0]
    C = RWKV_CHUNK
    wide = lambda c: pl.BlockSpec((C, W), lambda t, c=c: (t, c))
    full = lambda a_: pl.BlockSpec(a_.shape, lambda t: (0,) * a_.ndim)
    consts = (mu, muwa, w0, w2p, a0, a2p, k_k, k_a, r_k, ln_w, ln_b)
    return pl.pallas_call(
        _rwkv_kernel,
        out_shape=jax.ShapeDtypeStruct((L, W), BF16),
        grid=(L // C,),
        in_specs=[wide(COL_RC), wide(COL_KC), wide(COL_VC),
                  pl.BlockSpec((C, LANE), lambda t: (t, OFF_WA // LANE)),
                  pl.BlockSpec((pl.Element(C), pl.Element(W)), lambda t: (t * C, OFF_ZC))]
                 + [full(c) for c in consts],
        out_specs=pl.BlockSpec((C, W), lambda t: (t, 0)),
        scratch_shapes=[pltpu.VMEM((SUB + C + SUB, 3 * W), F32),
                        pltpu.VMEM((SUB + C + SUB, LANE), F32),
                        pltpu.VMEM((W // LANE, LANE, LANE), F32)],
        compiler_params=_cparams(("arbitrary",)),
    )(p, p, p, p, p, *consts)


def _s5_kernel(u_ref, perm_ref, m_ref, toe_ref, q_ref, l1_ref, l2_ref, l2s_ref, y_ref,
               xl_ref, xls_ref, xp_ref, car_ref):
    @pl.when(pl.program_id(1) == 0)
    def _():
        car_ref[...] = jnp.zeros_like(car_ref)

    tn = u_ref.shape[0]
    gw = S5_CHUNK * S5_GROUP
    perm = perm_ref[...]
    lhs = jnp.concatenate([u_ref[:, j, :].astype(BF16) for j in range(S5_CHUNK)], axis=1)
    lhs = jnp.dot(lhs, perm, preferred_element_type=F32).astype(BF16)
    lg = [lhs[:, g * gw:(g + 1) * gw] for g in range(S5_GBLK)]
    xl = jnp.concatenate([jnp.dot(lg[g], m_ref[g], preferred_element_type=F32) for g in range(S5_GBLK)],
                         axis=1)
    xl_ref[...] = xl
    xls_ref[...] = jnp.concatenate(
        [pltpu.roll(xl[:, g * LANE:(g + 1) * LANE], S5_STATE, 1) for g in range(S5_GBLK)], axis=1)
    l1, l2, l2s = l1_ref[...], l2_ref[...], l2s_ref[...]

    def step(i, carry):
        x, xs = carry
        row = pl.ds(i, 1)
        xp_ref[row, :] = x
        return (l1 * x + l2 * xs + xl_ref[row, :], l1 * xs + l2s * x + xls_ref[row, :])

    x, xs = lax.fori_loop(0, tn, step, (car_ref[0:1, :], car_ref[1:2, :]))
    car_ref[0:1, :] = x
    car_ref[1:2, :] = xs
    xp = xp_ref[...].astype(BF16)
    y = jnp.concatenate(
        [jnp.dot(lg[g], toe_ref[g], preferred_element_type=F32)
         + jnp.dot(xp[:, g * LANE:(g + 1) * LANE], q_ref[g], preferred_element_type=F32)
         for g in range(S5_GBLK)], axis=1)
    y = lax.dot_general(y.astype(BF16), perm, (((1,), (1,)), ((), ())),
                        preferred_element_type=F32)
    for t in range(S5_CHUNK):
        y_ref[:, t, :] = y[:, t * LANE:(t + 1) * LANE]


def _s5_scan(u3, m, toe, q, l1, l2, l2s):
    n = u3.shape[0]
    tn = min(512, n)
    nb = W // LANE
    sw = S5_GBLK * 2 * S5_STATE
    kw = S5_CHUNK * LANE
    r = jnp.arange(kw)
    dst = (r // S5_GROUP % S5_GBLK) * (S5_CHUNK * S5_GROUP) + (r // LANE) * S5_GROUP + r % S5_GROUP
    perm = (dst[:, None] == jnp.arange(kw)[None, :]).astype(BF16)
    per_b = lambda a_: pl.BlockSpec((S5_GBLK,) + a_.shape[1:], lambda b, i: (b, 0, 0))
    vec = pl.BlockSpec((None, 1, sw), lambda b, i: (b, 0, 0))
    return pl.pallas_call(
        _s5_kernel,
        out_shape=jax.ShapeDtypeStruct((n, S5_CHUNK, W), F32),
        grid=(nb, n // tn),
        in_specs=[pl.BlockSpec((tn, S5_CHUNK, LANE), lambda b, i: (i, 0, b)),
                  pl.BlockSpec((kw, kw), lambda b, i: (0, 0)),
                  per_b(m), per_b(toe), per_b(q), vec, vec, vec],
        out_specs=pl.BlockSpec((tn, S5_CHUNK, LANE), lambda b, i: (i, 0, b)),
        scratch_shapes=[pltpu.VMEM((tn, sw), F32)] * 3 + [pltpu.VMEM((SUB, sw), F32)],
        compiler_params=_cparams(("arbitrary", "arbitrary")),
    )(u3, perm, m, toe, q, l1, l2, l2s)


def _s5_out_kernel(y_ref, u_ref, z_ref, d_ref, wg_ref, bg_ref, o_ref):
    y = y_ref[...].reshape(u_ref.shape) + d_ref[...] * u_ref[...]
    g = jax.nn.gelu(y)
    o = g * jax.nn.sigmoid(_dot(g, wg_ref[...]) + bg_ref[...])
    o_ref[...] = (o * _silu(z_ref[...])).astype(o_ref.dtype)


def _s5_out(y3, p, d, wg, bg):
    L = p.shape[0]
    tm = min(512, L)
    vec = pl.BlockSpec((1, W), lambda t: (0, 0))
    return pl.pallas_call(
        _s5_out_kernel,
        out_shape=jax.ShapeDtypeStruct((L, W), BF16),
        grid=(L // tm,),
        in_specs=[pl.BlockSpec((tm // S5_CHUNK, S5_CHUNK, W), lambda t: (t, 0, 0)),
                  pl.BlockSpec((tm, W), lambda t: (t, COL_UB)),
                  pl.BlockSpec((tm, W), lambda t: (t, COL_ZB)),
                  vec, pl.BlockSpec((W, W), lambda t: (0, 0)), vec],
        out_specs=pl.BlockSpec((tm, W), lambda t: (t, 0)),
        compiler_params=_cparams(("parallel",)),
    )(y3, p, p, d, wg, bg)


def _s5_tables(a_re, a_im, log_dt, b_re, b_im, c_re, c_im):
    J = S5_CHUNK
    lam_re = jnp.minimum(a_re, S5_MAX_RE)
    lam_im = a_im
    dt = jnp.exp(log_dt)[:, None]
    dl_re, dl_im = lam_re * dt, lam_im * dt
    mag = jnp.exp(dl_re)
    num_re = mag * jnp.cos(dl_im) - 1.0
    num_im = mag * jnp.sin(dl_im)
    den = lam_re * lam_re + lam_im * lam_im
    coef_re = (num_re * lam_re + num_im * lam_im) / den
    coef_im = (num_im * lam_re - num_re * lam_im) / den
    bb_re = coef_re[..., None] * b_re - coef_im[..., None] * b_im
    bb_im = coef_re[..., None] * b_im + coef_im[..., None] * b_re
    tau = jnp.arange(J + 1, dtype=F32)[:, None, None]
    pm = jnp.exp(tau * dl_re)
    pw_re, pw_im = pm * jnp.cos(tau * dl_im), pm * jnp.sin(tau * dl_im)
    pb_re = pw_re[..., None] * bb_re - pw_im[..., None] * bb_im
    pb_im = pw_re[..., None] * bb_im + pw_im[..., None] * bb_re
    kern = (jnp.einsum('gop,tgph->tgoh', c_re, pb_re[:J], precision=HI)
            - jnp.einsum('gop,tgph->tgoh', c_im, pb_im[:J], precision=HI))
    t_idx = jnp.arange(J)
    lag = t_idx[None, :] - t_idx[:, None]
    onehot = (lag[:, :, None] == t_idx[None, None, :]).astype(F32)
    toe = jnp.einsum('stu,ugoh->stgoh', onehot, kern, precision=HI)
    nb, gb = S5_GROUPS // S5_GBLK, S5_GBLK
    toe = toe.transpose(2, 0, 4, 1, 3).reshape(S5_GROUPS, J * S5_GROUP, J * S5_GROUP)
    rev_re, rev_im = pb_re[J - 1 - t_idx], pb_im[J - 1 - t_idx]
    to_m = lambda x: x.transpose(1, 0, 3, 2).reshape(S5_GROUPS, J * S5_GROUP, S5_STATE)
    m = jnp.concatenate([to_m(rev_re), to_m(rev_im)], axis=-1)
    q_re = (c_re[None] * pw_re[1:, :, None, :] - c_im[None] * pw_im[1:, :, None, :])
    q_im = -(c_re[None] * pw_im[1:, :, None, :] + c_im[None] * pw_re[1:, :, None, :])
    to_q = lambda x: x.transpose(1, 3, 0, 2).reshape(S5_GROUPS, S5_STATE, J * S5_GROUP)
    q = jnp.concatenate([to_q(q_re), to_q(q_im)], axis=1)
    lj_re, lj_im = pw_re[J], pw_im[J]
    rs = lambda parts: jnp.concatenate(parts, axis=-1).reshape(nb, 1, gb * 2 * S5_STATE)
    return (m.astype(BF16), toe.astype(BF16), q.astype(BF16),
            rs([lj_re, lj_re]), rs([-lj_im, lj_im]), rs([lj_im, -lj_im]))


def _s5(u3, p, tables, d, wg, bg):
    return _s5_out(_s5_scan(u3, *tables), p, d, wg, bg)


def _up_kernel(oa_ref, ob_ref, oc_ref, od_ref, ga_ref, gb_ref, gc_ref, gd_ref, w_ref, bias_ref, o_ref):
    acc = None
    for b, (o_b, g_b) in enumerate(((oa_ref, ga_ref), (ob_ref, gb_ref), (oc_ref, gc_ref), (od_ref, gd_ref))):
        gate = jax.nn.sigmoid(g_b[...] + bias_ref[b:b + 1, :])
        contrib = gate * jnp.dot(o_b[...], w_ref[b], preferred_element_type=F32)
        acc = contrib if acc is None else acc + contrib
    o_ref[...] = acc.astype(o_ref.dtype)


def _up(outs, p, w_up, bias):
    L = p.shape[0]
    tm = min(1024, L)
    tn = 512
    nb = D_MODEL // tn
    o_spec = pl.BlockSpec((tm, W), lambda i, j: (i, 0))
    g_spec = lambda b: pl.BlockSpec((tm, tn), lambda i, j, b=b: (i, (P2_GATE * W + b * D_MODEL) // tn + j))
    return pl.pallas_call(
        _up_kernel,
        out_shape=jax.ShapeDtypeStruct((L, D_MODEL), BF16),
        grid=(L // tm, nb),
        in_specs=[o_spec] * 4 + [g_spec(b) for b in range(N_BRANCH)]
                 + [pl.BlockSpec((N_BRANCH, W, tn), lambda i, j: (0, 0, j)),
                    pl.BlockSpec((N_BRANCH, tn), lambda i, j: (0, j))],
        out_specs=pl.BlockSpec((tm, tn), lambda i, j: (i, j)),
        compiler_params=_cparams(("parallel", "arbitrary")),
    )(*outs, p, p, p, p, w_up, bias)


def _out_kernel(m_ref, w_ref, g_ref, res_ref, o_ref):
    y = jnp.dot(m_ref[...], w_ref[...], preferred_element_type=F32)
    o_ref[...] = res_ref[...] + (y * lax.rsqrt(jnp.mean(y * y, axis=-1, keepdims=True) + NORM_EPS)
                                 * g_ref[...])


def _out(merged, w_out, g, res):
    L = res.shape[1]
    tm = min(512, L)
    return pl.pallas_call(
        _out_kernel,
        out_shape=jax.ShapeDtypeStruct((1, L, D_MODEL), F32),
        grid=(L // tm,),
        in_specs=[pl.BlockSpec((tm, D_MODEL), lambda i: (i, 0)),
                  pl.BlockSpec((D_MODEL, D_MODEL), lambda i: (0, 0)),
                  pl.BlockSpec((1, D_MODEL), lambda i: (0, 0)),
                  pl.BlockSpec((None, tm, D_MODEL), lambda i: (0, i, 0))],
        out_specs=pl.BlockSpec((None, tm, D_MODEL), lambda i: (0, i, 0)),
        compiler_params=_cparams(("parallel",)),
    )(merged, w_out, g.reshape(1, D_MODEL), res)


def _pack_w_in(w):
    wb = w.astype(BF16)
    return wb, wb[:, OFF_P2:OFF_P2 + P2_COLS]


def _row(v):
    return v.astype(F32).reshape(1, -1)


def kernel(x, norm_pre, norm_post, w_in, gate_bias, w_up, w_out, hgrn_lb_logits, hgrn_norm, s5_a_re, s5_a_im, s5_log_dt, s5_b_re, s5_b_im, s5_c_re, s5_c_im, s5_d, s5_w_glu, s5_b_glu, rwkv_mu, rwkv_w0, rwkv_w2, rwkv_a0, rwkv_a2, rwkv_k_k, rwkv_k_a, rwkv_r_k, rwkv_ln_w, rwkv_ln_b, m2_conv_w, m2_conv_b, m2_dt_bias, m2_a_log, m2_d, m2_norm):
    depth = w_in.shape[0]
    L = x.shape[1]
    lb_all = jnp.cumsum(jax.nn.softmax(hgrn_lb_logits.astype(F32), axis=0), axis=0)
    lb_all = lb_all - lb_all[0:1]

    zpad = jnp.zeros((RWKV_LORA, W), F32)
    hpad = jnp.zeros((1, LANE - M2_HEADS), F32)

    res = x.astype(F32)
    for l in range(depth):
        h = _prenorm(res, norm_pre[l])
        w1, w2 = _pack_w_in(w_in[l])
        p, u3 = _inproj(h, w1)
        p2 = _matmul(h, w2, P2_TN, BF16)

        lb = lb_all[l]
        o_a = _hgrn(p, _row(jnp.log(lb)), _row(jnp.log1p(-lb)), _row(1.0 - lb), _row(hgrn_norm[l]))

        tables = _s5_tables(s5_a_re[l].astype(F32), s5_a_im[l].astype(F32), s5_log_dt[l].astype(F32),
                            s5_b_re[l].astype(F32), s5_b_im[l].astype(F32),
                            s5_c_re[l].astype(F32), s5_c_im[l].astype(F32))
        o_b = _s5(u3, p, tables, _row(s5_d[l]), s5_w_glu[l].astype(BF16), _row(s5_b_glu[l]))

        mu = rwkv_mu[l].astype(F32)
        o_c = _rwkv(p, _row(mu[:3 * W]), _row(mu[3 * W:]), _row(rwkv_w0[l]),
                    jnp.concatenate([rwkv_w2[l].astype(F32), zpad], axis=0).astype(BF16),
                    _row(rwkv_a0[l]),
                    jnp.concatenate([zpad, rwkv_a2[l].astype(F32)], axis=0).astype(BF16),
                    _row(rwkv_k_k[l]), _row(rwkv_k_a[l]), _row(rwkv_r_k[l]),
                    _row(rwkv_ln_w[l]), _row(rwkv_ln_b[l]))

        o_d = _ssd(p, p2, m2_conv_w[l].astype(F32), _row(m2_conv_b[l]),
                   jnp.concatenate([_row(m2_dt_bias[l]), hpad], axis=1),
                   jnp.concatenate([-jnp.exp(_row(m2_a_log[l])), hpad], axis=1),
                   _row(jnp.repeat(m2_d[l].astype(F32), M2_HEADDIM)), _row(m2_norm[l]))

        merged = _up((o_a, o_b, o_c, o_d), p2, w_up[l].astype(BF16), gate_bias[l].astype(F32))
        res = _out(merged, w_out[l].astype(BF16), norm_post[l].astype(F32), res)
    return res.astype(x.dtype)
```

```python
import functools

import jax
import jax.numpy as jnp
from jax import lax
from jax.experimental import pallas as pl
from jax.experimental.pallas import tpu as pltpu

F32 = jnp.float32
BF16 = jnp.bfloat16
HI = lax.Precision.HIGHEST
LOG2E = 1.4426950408889634

D_MODEL = 2048
W = D_MODEL // 2
N_BRANCH = 4
NORM_EPS = 1e-6

HGRN_DK = 128
HGRN_CHUNK = 64
HGRN_SUB = 8
HGRN_HPS = 8

S5_GROUP = 16
S5_GROUPS = 64
S5_STATE = 64
S5_CHUNK = 16
S5_MAX_RE = -1e-4
S5_GBLK = 8

RWKV_HEAD = 64
RWKV_LORA = 64
RWKV_CHUNK = 64
RWKV_SUB = 16
RWKV_DECAY_SCALE = 0.606531
RWKV_LN_EPS = 64e-5
RWKV_KNORM_FLOOR = 1e-12

M2_HEADS = 16
M2_HEADDIM = 64
M2_GROUPS = 4
M2_STATE = 128
M2_CONV = 4
M2_CHUNK = 128
M2_NORM_EPS = 1e-5

LANE = 128
SUB = 8
VMEM_LIMIT = 56 * 1024 * 1024

COL_QA, COL_FA, COL_IA, COL_ZA = 0, 1, 2, 3
COL_UB, COL_ZB = 4, 5
COL_RC, COL_KC, COL_VC = 6, 7, 8
OFF_WA = 9 * W
OFF_ZC = OFF_WA + 2 * RWKV_LORA
OFF_XD = OFF_ZC + W
OFF_BCD = OFF_XD + W
OFF_DT = OFF_BCD + 2 * M2_GROUPS * M2_STATE
P1_COLS = OFF_DT + LANE
OFF_P2 = OFF_DT + M2_HEADS
P2_ZD, P2_GATE = 0, 1
P2_COLS = W + N_BRANCH * D_MODEL
P1_TN = 1792
P2_TN = 2304
U_TILE = (COL_UB * W) // P1_TN
U_OFF = COL_UB * W - U_TILE * P1_TN


def _cparams(sem):
    return pltpu.CompilerParams(dimension_semantics=sem, vmem_limit_bytes=VMEM_LIMIT)


def _dot(a, b):
    return jnp.dot(a.astype(BF16), b.astype(BF16), preferred_element_type=F32)


def _dot_nt(a, b):
    return lax.dot_general(a.astype(BF16), b.astype(BF16), (((1,), (1,)), ((), ())),
                           preferred_element_type=F32)


def _dot_tn(a, b):
    return lax.dot_general(a.astype(BF16), b.astype(BF16), (((0,), (0,)), ((), ())),
                           preferred_element_type=F32)


def _split(x, n):
    parts = []
    for _ in range(n):
        part = x.astype(BF16)
        parts.append(part)
        x = x - part.astype(F32)
    return parts


def _dot_sel(x, sel_tiled, n):
    return jnp.dot(jnp.concatenate(_split(x, n), axis=1), sel_tiled, preferred_element_type=F32)


def _silu(x):
    return x * jax.nn.sigmoid(x)


def _cumsum_rows(x):
    n = x.shape[0]
    row = lax.broadcasted_iota(jnp.int32, x.shape, 0)
    s = 1
    while s < n:
        x = x + jnp.where(row >= s, pltpu.roll(x, s, 0), 0.0)
        s *= 2
    return x


def _prenorm_kernel(x_ref, g_ref, o_ref):
    x = x_ref[...]
    o_ref[...] = (x * lax.rsqrt(jnp.mean(x * x, axis=-1, keepdims=True) + NORM_EPS)
                  * g_ref[...]).astype(o_ref.dtype)


def _prenorm(x, g):
    L = x.shape[1]
    tm = min(512, L)
    return pl.pallas_call(
        _prenorm_kernel,
        out_shape=jax.ShapeDtypeStruct((L, D_MODEL), BF16),
        grid=(L // tm,),
        in_specs=[pl.BlockSpec((None, tm, D_MODEL), lambda i: (0, i, 0)),
                  pl.BlockSpec((1, D_MODEL), lambda i: (0, 0))],
        out_specs=pl.BlockSpec((tm, D_MODEL), lambda i: (i, 0)),
        compiler_params=_cparams(("parallel",)),
    )(x, g.reshape(1, D_MODEL))


def _inproj_kernel(a_ref, w_ref, o_ref, u3_ref):
    acc = jnp.dot(a_ref[...], w_ref[...], preferred_element_type=F32)
    o_ref[...] = acc

    @pl.when(pl.program_id(1) == U_TILE)
    def _():
        u3_ref[...] = acc[:, U_OFF:U_OFF + W].reshape(u3_ref.shape)


def _inproj(a, w):
    M, K = a.shape
    tm = min(1024, M)
    return pl.pallas_call(
        _inproj_kernel,
        out_shape=(jax.ShapeDtypeStruct((M, P1_COLS), F32),
                   jax.ShapeDtypeStruct((M // S5_CHUNK, S5_CHUNK, W), F32)),
        grid=(M // tm, P1_COLS // P1_TN),
        in_specs=[pl.BlockSpec((tm, K), lambda i, j: (i, 0)),
                  pl.BlockSpec((K, P1_TN), lambda i, j: (0, j))],
        out_specs=(pl.BlockSpec((tm, P1_TN), lambda i, j: (i, j)),
                   pl.BlockSpec((tm // S5_CHUNK, S5_CHUNK, W), lambda i, j: (i, 0, 0))),
        compiler_params=_cparams(("parallel", "arbitrary")),
    )(a, w)


def _mm_kernel(a_ref, w_ref, o_ref):
    o_ref[...] = jnp.dot(a_ref[...], w_ref[...], preferred_element_type=F32).astype(o_ref.dtype)


def _matmul(a, w, tn, out_dtype):
    M, K = a.shape
    N = w.shape[1]
    tm = min(1024, M)
    return pl.pallas_call(
        _mm_kernel,
        out_shape=jax.ShapeDtypeStruct((M, N), out_dtype),
        grid=(M // tm, N // tn),
        in_specs=[pl.BlockSpec((tm, K), lambda i, j: (i, 0)),
                  pl.BlockSpec((K, tn), lambda i, j: (0, j))],
        out_specs=pl.BlockSpec((tm, tn), lambda i, j: (i, j)),
        compiler_params=_cparams(("parallel", "parallel")),
    )(a, w)


def _hgrn_kernel(q_ref, f_ref, i_ref, z_ref, loglb_ref, log1mlb_ref, omlb_ref, nw_ref,
                 o_ref, st_ref, *, n_chunks):
    @pl.when(pl.program_id(1) == 0)
    def _():
        st_ref[...] = jnp.zeros_like(st_ref)

    C, SB, DK = HGRN_CHUNK, HGRN_SUB, HGRN_DK
    NB = C // SB
    heads = range(HGRN_HPS)
    hs = [slice(j * DK, (j + 1) * DK) for j in heads]
    loglb, log1mlb, omlb, nw = loglb_ref[...], log1mlb_ref[...], omlb_ref[...], nw_ref[...]
    trow = lax.broadcasted_iota(jnp.int32, (SB, DK), 0)
    col_c = lax.broadcasted_iota(jnp.int32, (SB, C), 1)
    ones = jnp.ones((DK, DK), BF16)

    def chunk(c, carry):
        rows = pl.ds(pl.multiple_of(c * C, C), C)
        f = f_ref[rows, :]
        log_sig = jnp.minimum(f, 0.0) - jnp.log(1.0 + jnp.exp(-jnp.abs(f)))
        b = log1mlb + log_sig
        logf = jnp.maximum(loglb, b) + jnp.log(1.0 + jnp.exp(-jnp.abs(loglb - b)))
        cum_all = _cumsum_rows(logf) * LOG2E
        k_all = omlb * jax.nn.sigmoid(-f)
        q_all = _silu(q_ref[rows, :]) * (DK ** -0.5)
        cum = [cum_all[:, s_] for s_ in hs]
        k = [k_all[:, s_] for s_ in hs]
        q = [q_all[:, s_] for s_ in hs]
        v = [i_ref[rows, s_].astype(BF16) for s_ in hs]
        st = [st_ref[j] for j in heads]
        o = [_dot_nt(q[j] * jnp.exp2(cum[j]), st[j]) for j in heads]

        sc = [[None] * NB for _ in heads]
        for bi in range(1, NB):
            r0 = bi * SB
            for j in heads:
                ref = cum[j][r0:r0 + 1, :]
                kmod = k[j][0:r0, :] * jnp.exp2(ref - cum[j][0:r0, :])
                kmod = jnp.concatenate([kmod, jnp.zeros((C - r0, DK), F32)], axis=0)
                sc[j][bi] = _dot_nt(q[j][r0:r0 + SB, :] * jnp.exp2(cum[j][r0:r0 + SB, :] - ref), kmod)
        psum = []
        for j in heads:
            ps = []
            for bi in range(NB):
                r0 = bi * SB
                for s in range(SB):
                    d = jnp.where(trow >= s, cum[j][r0:r0 + SB, :] - cum[j][r0 + s:r0 + s + 1, :], -jnp.inf)
                    ps.append((q[j][r0:r0 + SB, :] * k[j][r0 + s:r0 + s + 1, :] * jnp.exp2(d)).astype(BF16))
            psum.append(jnp.dot(jnp.concatenate(ps, axis=0), ones, preferred_element_type=F32))
        for j in heads:
            blocks = []
            for bi in range(NB):
                acc = jnp.zeros((SB, C), F32) if bi == 0 else sc[j][bi]
                for s in range(SB):
                    r1 = (bi * SB + s) * SB
                    acc = jnp.where(col_c == bi * SB + s, psum[j][r1:r1 + SB, 0:C], acc)
                blocks.append(acc)
            o[j] = o[j] + _dot(jnp.concatenate(blocks, axis=0), v[j])

        for j in heads:
            last = cum[j][C - 1:C, :]
            st_ref[j] = st[j] * jnp.exp2(last) + _dot_tn(v[j], k[j] * jnp.exp2(last - cum[j]))
        for j in heads:
            oj = o[j] * lax.rsqrt(jnp.mean(o[j] * o[j], axis=-1, keepdims=True) + NORM_EPS) * nw[:, hs[j]]
            o_ref[rows, hs[j]] = (oj * _silu(z_ref[rows, hs[j]])).astype(o_ref.dtype)
        return carry

    lax.fori_loop(0, n_chunks, chunk, 0)


def _hgrn(p, loglb, log1mlb, omlb, nw):
    L = p.shape[0]
    T = min(512, L)
    nt = L // T
    gw = HGRN_HPS * HGRN_DK
    ng = W // gw
    blk = lambda c: pl.BlockSpec((T, gw), lambda h, t, c=c: (t, c * ng + h))
    vec = pl.BlockSpec((1, gw), lambda h, t: (0, h))
    return pl.pallas_call(
        functools.partial(_hgrn_kernel, n_chunks=T // HGRN_CHUNK),
        out_shape=jax.ShapeDtypeStruct((L, W), BF16),
        grid=(ng, nt),
        in_specs=[blk(COL_QA), blk(COL_FA), blk(COL_IA), blk(COL_ZA), vec, vec, vec, vec],
        out_specs=pl.BlockSpec((T, gw), lambda h, t: (t, h)),
        scratch_shapes=[pltpu.VMEM((HGRN_HPS, HGRN_DK, HGRN_DK), F32)],
        compiler_params=_cparams(("arbitrary", "arbitrary")),
    )(p, p, p, p, loglb, log1mlb, omlb, nw)


def _ssd_kernel(x_ref, bc_ref, z_ref, dt_ref, cw_ref, cb_ref, dtb_ref, a_ref, e_ref, d_ref, nw_ref,
                o_ref, xb_ref, st_ref):
    C = M2_CHUNK
    CH = 2 * W

    @pl.when(pl.program_id(0) == 0)
    def _():
        xb_ref[0:SUB, :] = jnp.zeros((SUB, CH), F32)
        st_ref[...] = jnp.zeros_like(st_ref)

    xb_ref[SUB:SUB + C, 0:W] = x_ref[...]
    xb_ref[SUB:SUB + C, W:CH] = bc_ref[...]
    acc = cb_ref[...] + cw_ref[0:1, :] * xb_ref[pl.ds(SUB - 3, C), :]
    for j in range(1, M2_CONV):
        acc = acc + cw_ref[j:j + 1, :] * xb_ref[pl.ds(SUB - 3 + j, C), :]
    xb_ref[0:SUB, :] = xb_ref[C:C + SUB, :]
    xc = _silu(acc)
    xh = xc[:, 0:W]

    dt = jax.nn.softplus(dt_ref[...] + dtb_ref[...])
    da = dt * a_ref[...]
    r_i = lax.broadcasted_iota(jnp.int32, (C, C), 0)
    c_i = lax.broadcasted_iota(jnp.int32, (C, C), 1)
    causal = r_i >= c_i
    causal3 = (lax.broadcasted_iota(jnp.int32, (C, 3 * C), 0)
               >= lax.broadcasted_iota(jnp.int32, (C, 3 * C), 1) % C).astype(BF16)
    a_cs = jnp.dot(causal3, jnp.concatenate(_split(da, 3), axis=0),
                   preferred_element_type=F32)
    a_cs_t = a_cs.T
    e = e_ref[...]
    ea = jnp.exp(a_cs)
    dt_e = _dot_sel(dt, e, 2)
    ea_e = _dot_sel(ea, e, 2)
    te_e = _dot_sel(jnp.exp(a_cs[C - 1:C, :] - a_cs), e, 2)
    xd = xh * dt_e
    xd_end = xd * te_e

    GW = W // M2_GROUPS
    lane_g = lax.broadcasted_iota(jnp.int32, (C, GW), 1)
    groups = range(M2_GROUPS)
    hpg = M2_HEADS // M2_GROUPS
    gsl = [slice(g * GW, (g + 1) * GW) for g in groups]
    c0 = W + M2_GROUPS * M2_STATE
    bg_t = [xc[:, W + g * M2_STATE:W + (g + 1) * M2_STATE].T.astype(BF16) for g in groups]
    cg = [xc[:, c0 + g * M2_STATE:c0 + (g + 1) * M2_STATE].astype(BF16) for g in groups]
    cb = [_dot(cg[g], bg_t[g]) for g in groups]
    y = [_dot(cg[g], st_ref[g]) * ea_e[:, gsl[g]] for g in groups]
    for g in groups:
        ms, xs = [], []
        for j in range(hpg):
            h = g * hpg + j
            seg = a_cs[:, h:h + 1] - a_cs_t[h:h + 1, :]
            ms.append((cb[g] * jnp.exp(jnp.where(causal, seg, -jnp.inf))).astype(BF16))
            xs.append(jnp.where(lane_g // M2_HEADDIM == j, xd[:, gsl[g]], 0.0).astype(BF16))
        y[g] = y[g] + _dot(jnp.concatenate(ms, axis=1), jnp.concatenate(xs, axis=0))
    for g in groups:
        st_ref[g] = st_ref[g] * ea_e[C - 1:C, gsl[g]] + _dot(bg_t[g], xd_end[:, gsl[g]])
    for g in groups:
        yg = (y[g] + d_ref[:, gsl[g]] * xh[:, gsl[g]]) * _silu(z_ref[:, gsl[g]].astype(F32))
        yg = yg * lax.rsqrt(jnp.mean(yg * yg, axis=-1, keepdims=True) + M2_NORM_EPS)
        o_ref[:, gsl[g]] = (yg * nw_ref[:, gsl[g]]).astype(o_ref.dtype)


def _ssd(p, p2, cw, cb, dtb, a, d_e, nw):
    L = p.shape[0]
    e = (jnp.arange(LANE)[:, None] == (jnp.arange(W) // M2_HEADDIM)[None, :]).astype(BF16)
    e = jnp.tile(e, (2, 1))
    C = M2_CHUNK
    at = lambda off: pl.BlockSpec((pl.Element(C), pl.Element(W)), lambda t, off=off: (t * C, off))
    full = lambda a_: pl.BlockSpec(a_.shape, lambda t: (0,) * a_.ndim)
    return pl.pallas_call(
        _ssd_kernel,
        out_shape=jax.ShapeDtypeStruct((L, W), BF16),
        grid=(L // C,),
        in_specs=[at(OFF_XD), at(OFF_BCD), pl.BlockSpec((C, W), lambda t: (t, P2_ZD)),
                  pl.BlockSpec((C, LANE), lambda t: (t, OFF_DT // LANE)),
                  full(cw), full(cb), full(dtb), full(a), full(e), full(d_e), full(nw)],
        out_specs=pl.BlockSpec((C, W), lambda t: (t, 0)),
        scratch_shapes=[pltpu.VMEM((SUB + C, 2 * W), F32),
                        pltpu.VMEM((M2_GROUPS, M2_STATE, W // M2_GROUPS), F32)],
        compiler_params=_cparams(("arbitrary",)),
    )(p, p, p2, p, cw, cb, dtb, a, e, d_e, nw)


def _rwkv_kernel(r_ref, k_ref, v_ref, wa_ref, z_ref, mu_ref, muwa_ref, w0_ref, w2_ref, a0_ref, a2_ref,
                 kk_ref, ka_ref, rk_ref, lnw_ref, lnb_ref,
                 o_ref, xb_ref, wab_ref, st_ref):
    C = RWKV_CHUNK

    @pl.when(pl.program_id(0) == 0)
    def _():
        xb_ref[0:SUB, :] = jnp.zeros((SUB, 3 * W), F32)
        wab_ref[0:SUB, :] = jnp.zeros((SUB, LANE), F32)
        st_ref[...] = jnp.zeros_like(st_ref)

    xb_ref[SUB:SUB + C, 0:W] = r_ref[...]
    xb_ref[SUB:SUB + C, W:2 * W] = k_ref[...]
    xb_ref[SUB:SUB + C, 2 * W:3 * W] = v_ref[...]
    wab_ref[SUB:SUB + C, :] = wa_ref[...]
    cur = xb_ref[SUB:SUB + C, :]
    xs = cur + mu_ref[...] * (xb_ref[pl.ds(SUB - 1, C), :] - cur)
    wa_cur = wab_ref[SUB:SUB + C, :]
    wa = wa_cur + muwa_ref[...] * (wab_ref[pl.ds(SUB - 1, C), :] - wa_cur)
    xb_ref[0:SUB, :] = xb_ref[C:C + SUB, :]
    wab_ref[0:SUB, :] = wab_ref[C:C + SUB, :]

    logw_all = -RWKV_DECAY_SCALE * jax.nn.sigmoid(w0_ref[...] + _dot(jnp.tanh(wa), w2_ref[...]))
    iclr_all = jax.nn.sigmoid(a0_ref[...] + _dot(wa, a2_ref[...]))

    lane = lax.broadcasted_iota(jnp.int32, (C, LANE), 1)
    lo = lane < RWKV_HEAD

    def hsum(x):
        s_lo = jnp.sum(jnp.where(lo, x, 0.0), axis=-1, keepdims=True)
        s_hi = jnp.sum(jnp.where(lo, 0.0, x), axis=-1, keepdims=True)
        return jnp.where(lo, s_lo, s_hi)
    t_i = lax.broadcasted_iota(jnp.int32, (C, 2 * C), 0)
    s_i = lax.broadcasted_iota(jnp.int32, (C, 2 * C), 1) % C
    strict = t_i > s_i
    incl_uv = (lax.broadcasted_iota(jnp.int32, (C, 4 * C), 0)
               >= lax.broadcasted_iota(jnp.int32, (C, 4 * C), 1) % C)
    r2 = lax.broadcasted_iota(jnp.int32, (2 * C, 2 * C), 0)
    c2 = lax.broadcasted_iota(jnp.int32, (2 * C, 2 * C), 1)
    same_blk = (r2 // RWKV_SUB) == (c2 // RWKV_SUB)
    eye = (r2 == c2).astype(F32)

    def stack(x):
        return jnp.concatenate([jnp.where(lo, x, 0.0), jnp.where(lo, 0.0, x)], axis=0)

    pairs = list(range(W // LANE))
    sls = [slice(i * LANE, (i + 1) * LANE) for i in pairs]
    d = [dict() for _ in pairs]

    def p_keys(i):
        sl = sls[i]
        k = xs[:, W + i * LANE:W + (i + 1) * LANE]
        kk = k * kk_ref[:, sl]
        d[i].update(r=xs[:, sl], v=xs[:, 2 * W + i * LANE:2 * W + (i + 1) * LANE],
                    kk=kk, ss=hsum(kk * kk), k2=k * (1.0 + (iclr_all[:, sl] - 1.0) * ka_ref[:, sl]))

    def p_norm(i):
        d[i]['kk'] = d[i]['kk'] / jnp.maximum(jnp.sqrt(d[i]['ss']), RWKV_KNORM_FLOOR)

    def p_decay(i):
        cum = _cumsum_rows(logw_all[:, sls[i]])
        d[i].update(cum=cum, e_inv=jnp.exp(-cum))

    def p_ar(i):
        cum = d[i]['cum']
        d[i]['ar'] = jnp.concatenate([-d[i]['kk'] * jnp.exp(cum - logw_all[:, sls[i]]), d[i]['r'] * jnp.exp(cum)],
                                     axis=0).astype(BF16)

    def p_bk(i):
        e_inv = d[i]['e_inv']
        d[i]['bk'] = jnp.concatenate([stack(d[i]['kk'] * iclr_all[:, sls[i]] * e_inv), stack(d[i]['k2'] * e_inv)],
                                     axis=0).astype(BF16)
        d[i]['v_s'] = stack(d[i]['v']).astype(BF16)

    def s_scores(i):
        d[i]['sc'] = _dot_nt(d[i]['ar'], d[i]['bk'])

    def s_state(i):
        d[i]['ah'] = _dot_nt(d[i]['ar'], st_ref[i])

    def s_w(i):
        sc = d[i]['sc']
        d[i]['w'] = d[i]['ah'][0:C, :] + _dot(jnp.where(strict, sc[0:C, 2 * C:4 * C], 0.0), d[i]['v_s'])

    def s_n(i):
        n_p = stack(jnp.where(strict, d[i]['sc'][0:C, 0:2 * C], 0.0))
        n_d = jnp.where(same_blk, n_p, 0.0)
        d[i].update(n_p=n_p, n_d=n_d, pw=n_d, t_d=eye + n_d)

    def s_pow(i):
        d[i]['pw'] = _dot(d[i]['pw'], d[i]['pw'])
        d[i]['t_d'] = d[i]['t_d'] + _dot(d[i]['pw'], d[i]['t_d'])

    def s_m(i):
        d[i]['m'] = _dot(d[i]['t_d'], d[i]['n_p'] - d[i]['n_d'])
        d[i]['u'] = _dot(d[i]['t_d'], stack(d[i]['w']))

    def s_m2(i):
        d[i]['m2'] = _dot(d[i]['m'], d[i]['m'])

    def s_u1(i):
        d[i]['u'] = d[i]['u'] + _dot(d[i]['m2'], d[i]['u'])

    def s_u2(i):
        d[i]['u'] = d[i]['u'] + _dot(d[i]['m'], d[i]['u'])

    def s_y(i):
        uv = jnp.concatenate([d[i]['u'].astype(BF16), d[i]['v_s']], axis=0)
        d[i]['uv'] = uv
        d[i]['y'] = d[i]['ah'][C:2 * C, :] + _dot(jnp.where(incl_uv, d[i]['sc'][C:2 * C, :], 0.0), uv)

    def s_carry(i):
        st_ref[i] = (st_ref[i] + _dot_tn(d[i]['uv'], d[i]['bk'])) * jnp.exp(d[i]['cum'][C - 1:C, :])

    def e_center(i):
        d[i]['yc'] = d[i]['y'] - hsum(d[i]['y']) * (1.0 / RWKV_HEAD)

    def e_var(i):
        d[i]['var'] = hsum(d[i]['yc'] * d[i]['yc']) * (1.0 / RWKV_HEAD)
        d[i]['bonus'] = hsum(d[i]['r'] * d[i]['k2'] * rk_ref[:, sls[i]]) * d[i]['v']

    def e_out(i):
        sl = sls[i]
        yn = d[i]['yc'] * lax.rsqrt(d[i]['var'] + RWKV_LN_EPS) * lnw_ref[:, sl] + lnb_ref[:, sl]
        o_ref[:, sl] = ((yn + d[i]['bonus']) * _silu(z_ref[:, sl])).astype(o_ref.dtype)

    for stage in (p_keys, p_norm, p_decay, p_ar, p_bk,
                  s_scores, s_state, s_w, s_n, s_pow, s_pow, s_pow, s_m, s_m2, s_u1, s_u2, s_y, s_carry,
                  e_center, e_var, e_out):
        for i in pairs:
            stage(i)


def _rwkv(p, mu, muwa, w0, w2p, a0, a2p, k_k, k_a, r_k, ln_w, ln_b):
    L = p.shape[0]
    C = RWKV_CHUNK
    wide = lambda c: pl.BlockSpec((C, W), lambda t, c=c: (t, c))
    full = lambda a_: pl.BlockSpec(a_.shape, lambda t: (0,) * a_.ndim)
    consts = (mu, muwa, w0, w2p, a0, a2p, k_k, k_a, r_k, ln_w, ln_b)
    return pl.pallas_call(
        _rwkv_kernel,
        out_shape=jax.ShapeDtypeStruct((L, W), BF16),
        grid=(L // C,),
        in_specs=[wide(COL_RC), wide(COL_KC), wide(COL_VC),
                  pl.BlockSpec((C, LANE), lambda t: (t, OFF_WA // LANE)),
                  pl.BlockSpec((pl.Element(C), pl.Element(W)), lambda t: (t * C, OFF_ZC))]
                 + [full(c) for c in consts],
        out_specs=pl.BlockSpec((C, W), lambda t: (t, 0)),
        scratch_shapes=[pltpu.VMEM((SUB + C, 3 * W), F32),
                        pltpu.VMEM((SUB + C, LANE), F32),
                        pltpu.VMEM((W // LANE, LANE, LANE), F32)],
        compiler_params=_cparams(("arbitrary",)),
    )(p, p, p, p, p, *consts)


def _s5_kernel(u_ref, perm_ref, m_ref, toe_ref, q_ref, l1_ref, l2_ref, l2s_ref, y_ref,
               xl_ref, xls_ref, xp_ref, car_ref):
    @pl.when(pl.program_id(1) == 0)
    def _():
        car_ref[...] = jnp.zeros_like(car_ref)

    tn = u_ref.shape[0]
    gw = S5_CHUNK * S5_GROUP
    perm = perm_ref[...]
    lhs = jnp.concatenate([u_ref[:, j, :].astype(BF16) for j in range(S5_CHUNK)], axis=1)
    lhs = jnp.dot(lhs, perm, preferred_element_type=F32).astype(BF16)
    lg = [lhs[:, g * gw:(g + 1) * gw] for g in range(S5_GBLK)]
    xl = jnp.concatenate([jnp.dot(lg[g], m_ref[g], preferred_element_type=F32) for g in range(S5_GBLK)],
                         axis=1)
    xl_ref[...] = xl
    xls_ref[...] = jnp.concatenate(
        [pltpu.roll(xl[:, g * LANE:(g + 1) * LANE], S5_STATE, 1) for g in range(S5_GBLK)], axis=1)
    l1, l2, l2s = l1_ref[...], l2_ref[...], l2s_ref[...]

    def step(i, carry):
        x, xs = carry
        row = pl.ds(i, 1)
        xp_ref[row, :] = x
        return (l1 * x + l2 * xs + xl_ref[row, :], l1 * xs + l2s * x + xls_ref[row, :])

    x, xs = lax.fori_loop(0, tn, step, (car_ref[0:1, :], car_ref[1:2, :]))
    car_ref[0:1, :] = x
    car_ref[1:2, :] = xs
    xp = xp_ref[...].astype(BF16)
    y = jnp.concatenate(
        [jnp.dot(lg[g], toe_ref[g], preferred_element_type=F32)
         + jnp.dot(xp[:, g * LANE:(g + 1) * LANE], q_ref[g], preferred_element_type=F32)
         for g in range(S5_GBLK)], axis=1)
    y = lax.dot_general(y.astype(BF16), perm, (((1,), (1,)), ((), ())),
                        preferred_element_type=F32)
    for t in range(S5_CHUNK):
        y_ref[:, t, :] = y[:, t * LANE:(t + 1) * LANE]


def _s5_scan(u3, m, toe, q, l1, l2, l2s):
    n = u3.shape[0]
    tn = min(512, n)
    nb = W // LANE
    sw = S5_GBLK * 2 * S5_STATE
    kw = S5_CHUNK * LANE
    r = jnp.arange(kw)
    dst = (r // S5_GROUP % S5_GBLK) * (S5_CHUNK * S5_GROUP) + (r // LANE) * S5_GROUP + r % S5_GROUP
    perm = (dst[:, None] == jnp.arange(kw)[None, :]).astype(BF16)
    per_b = lambda a_: pl.BlockSpec((S5_GBLK,) + a_.shape[1:], lambda b, i: (b, 0, 0))
    vec = pl.BlockSpec((None, 1, sw), lambda b, i: (b, 0, 0))
    return pl.pallas_call(
        _s5_kernel,
        out_shape=jax.ShapeDtypeStruct((n, S5_CHUNK, W), F32),
        grid=(nb, n // tn),
        in_specs=[pl.BlockSpec((tn, S5_CHUNK, LANE), lambda b, i: (i, 0, b)),
                  pl.BlockSpec((kw, kw), lambda b, i: (0, 0)),
                  per_b(m), per_b(toe), per_b(q), vec, vec, vec],
        out_specs=pl.BlockSpec((tn, S5_CHUNK, LANE), lambda b, i: (i, 0, b)),
        scratch_shapes=[pltpu.VMEM((tn, sw), F32)] * 3 + [pltpu.VMEM((SUB, sw), F32)],
        compiler_params=_cparams(("arbitrary", "arbitrary")),
    )(u3, perm, m, toe, q, l1, l2, l2s)


def _s5_out_kernel(y_ref, u_ref, z_ref, d_ref, wg_ref, bg_ref, o_ref):
    y = y_ref[...].reshape(u_ref.shape) + d_ref[...] * u_ref[...]
    g = jax.nn.gelu(y)
    o = g * jax.nn.sigmoid(_dot(g, wg_ref[...]) + bg_ref[...])
    o_ref[...] = (o * _silu(z_ref[...])).astype(o_ref.dtype)


def _s5_out(y3, p, d, wg, bg):
    L = p.shape[0]
    tm = min(512, L)
    vec = pl.BlockSpec((1, W), lambda t: (0, 0))
    return pl.pallas_call(
        _s5_out_kernel,
        out_shape=jax.ShapeDtypeStruct((L, W), BF16),
        grid=(L // tm,),
        in_specs=[pl.BlockSpec((tm // S5_CHUNK, S5_CHUNK, W), lambda t: (t, 0, 0)),
                  pl.BlockSpec((tm, W), lambda t: (t, COL_UB)),
                  pl.BlockSpec((tm, W), lambda t: (t, COL_ZB)),
                  vec, pl.BlockSpec((W, W), lambda t: (0, 0)), vec],
        out_specs=pl.BlockSpec((tm, W), lambda t: (t, 0)),
        compiler_params=_cparams(("parallel",)),
    )(y3, p, p, d, wg, bg)


def _s5_tables(a_re, a_im, log_dt, b_re, b_im, c_re, c_im):
    J = S5_CHUNK
    lam_re = jnp.minimum(a_re, S5_MAX_RE)
    lam_im = a_im
    dt = jnp.exp(log_dt)[:, None]
    dl_re, dl_im = lam_re * dt, lam_im * dt
    mag = jnp.exp(dl_re)
    num_re = mag * jnp.cos(dl_im) - 1.0
    num_im = mag * jnp.sin(dl_im)
    den = lam_re * lam_re + lam_im * lam_im
    coef_re = (num_re * lam_re + num_im * lam_im) / den
    coef_im = (num_im * lam_re - num_re * lam_im) / den
    bb_re = coef_re[..., None] * b_re - coef_im[..., None] * b_im
    bb_im = coef_re[..., None] * b_im + coef_im[..., None] * b_re
    tau = jnp.arange(J + 1, dtype=F32)[:, None, None]
    pm = jnp.exp(tau * dl_re)
    pw_re, pw_im = pm * jnp.cos(tau * dl_im), pm * jnp.sin(tau * dl_im)
    pb_re = pw_re[..., None] * bb_re - pw_im[..., None] * bb_im
    pb_im = pw_re[..., None] * bb_im + pw_im[..., None] * bb_re
    kern = (jnp.einsum('gop,tgph->tgoh', c_re, pb_re[:J], precision=HI)
            - jnp.einsum('gop,tgph->tgoh', c_im, pb_im[:J], precision=HI))
    t_idx = jnp.arange(J)
    lag = t_idx[None, :] - t_idx[:, None]
    onehot = (lag[:, :, None] == t_idx[None, None, :]).astype(F32)
    toe = jnp.einsum('stu,ugoh->stgoh', onehot, kern, precision=HI)
    nb, gb = S5_GROUPS // S5_GBLK, S5_GBLK
    toe = toe.transpose(2, 0, 4, 1, 3).reshape(S5_GROUPS, J * S5_GROUP, J * S5_GROUP)
    rev_re, rev_im = pb_re[J - 1 - t_idx], pb_im[J - 1 - t_idx]
    to_m = lambda x: x.transpose(1, 0, 3, 2).reshape(S5_GROUPS, J * S5_GROUP, S5_STATE)
    m = jnp.concatenate([to_m(rev_re), to_m(rev_im)], axis=-1)
    q_re = (c_re[None] * pw_re[1:, :, None, :] - c_im[None] * pw_im[1:, :, None, :])
    q_im = -(c_re[None] * pw_im[1:, :, None, :] + c_im[None] * pw_re[1:, :, None, :])
    to_q = lambda x: x.transpose(1, 3, 0, 2).reshape(S5_GROUPS, S5_STATE, J * S5_GROUP)
    q = jnp.concatenate([to_q(q_re), to_q(q_im)], axis=1)
    lj_re, lj_im = pw_re[J], pw_im[J]
    rs = lambda parts: jnp.concatenate(parts, axis=-1).reshape(nb, 1, gb * 2 * S5_STATE)
    return (m.astype(BF16), toe.astype(BF16), q.astype(BF16),
            rs([lj_re, lj_re]), rs([-lj_im, lj_im]), rs([lj_im, -lj_im]))


def _s5(u3, p, tables, d, wg, bg):
    return _s5_out(_s5_scan(u3, *tables), p, d, wg, bg)


def _up_kernel(oa_ref, ob_ref, oc_ref, od_ref, ga_ref, gb_ref, gc_ref, gd_ref, w_ref, bias_ref, o_ref):
    acc = None
    for b, (o_b, g_b) in enumerate(((oa_ref, ga_ref), (ob_ref, gb_ref), (oc_ref, gc_ref), (od_ref, gd_ref))):
        gate = jax.nn.sigmoid(g_b[...] + bias_ref[b:b + 1, :])
        contrib = gate * jnp.dot(o_b[...], w_ref[b], preferred_element_type=F32)
        acc = contrib if acc is None else acc + contrib
    o_ref[...] = acc.astype(o_ref.dtype)


def _up(outs, p, w_up, bias):
    L = p.shape[0]
    tm = min(1024, L)
    tn = 512
    nb = D_MODEL // tn
    o_spec = pl.BlockSpec((tm, W), lambda i, j: (i, 0))
    g_spec = lambda b: pl.BlockSpec((tm, tn), lambda i, j, b=b: (i, (P2_GATE * W + b * D_MODEL) // tn + j))
    return pl.pallas_call(
        _up_kernel,
        out_shape=jax.ShapeDtypeStruct((L, D_MODEL), BF16),
        grid=(L // tm, nb),
        in_specs=[o_spec] * 4 + [g_spec(b) for b in range(N_BRANCH)]
                 + [pl.BlockSpec((N_BRANCH, W, tn), lambda i, j: (0, 0, j)),
                    pl.BlockSpec((N_BRANCH, tn), lambda i, j: (0, j))],
        out_specs=pl.BlockSpec((tm, tn), lambda i, j: (i, j)),
        compiler_params=_cparams(("parallel", "arbitrary")),
    )(*outs, p, p, p, p, w_up, bias)


def _out_kernel(m_ref, w_ref, g_ref, res_ref, o_ref):
    y = jnp.dot(m_ref[...], w_ref[...], preferred_element_type=F32)
    o_ref[...] = res_ref[...] + (y * lax.rsqrt(jnp.mean(y * y, axis=-1, keepdims=True) + NORM_EPS)
                                 * g_ref[...])


def _out(merged, w_out, g, res):
    L = res.shape[1]
    tm = min(512, L)
    return pl.pallas_call(
        _out_kernel,
        out_shape=jax.ShapeDtypeStruct((1, L, D_MODEL), F32),
        grid=(L // tm,),
        in_specs=[pl.BlockSpec((tm, D_MODEL), lambda i: (i, 0)),
                  pl.BlockSpec((D_MODEL, D_MODEL), lambda i: (0, 0)),
                  pl.BlockSpec((1, D_MODEL), lambda i: (0, 0)),
                  pl.BlockSpec((None, tm, D_MODEL), lambda i: (0, i, 0))],
        out_specs=pl.BlockSpec((None, tm, D_MODEL), lambda i: (0, i, 0)),
        compiler_params=_cparams(("parallel",)),
    )(merged, w_out, g.reshape(1, D_MODEL), res)


def _pack_w_in(w):
    wb = w.astype(BF16)
    return wb, wb[:, OFF_P2:OFF_P2 + P2_COLS]


def _row(v):
    return v.astype(F32).reshape(1, -1)


def kernel(x, norm_pre, norm_post, w_in, gate_bias, w_up, w_out, hgrn_lb_logits, hgrn_norm, s5_a_re, s5_a_im, s5_log_dt, s5_b_re, s5_b_im, s5_c_re, s5_c_im, s5_d, s5_w_glu, s5_b_glu, rwkv_mu, rwkv_w0, rwkv_w2, rwkv_a0, rwkv_a2, rwkv_k_k, rwkv_k_a, rwkv_r_k, rwkv_ln_w, rwkv_ln_b, m2_conv_w, m2_conv_b, m2_dt_bias, m2_a_log, m2_d, m2_norm):
    depth = w_in.shape[0]
    L = x.shape[1]
    lb_all = jnp.cumsum(jax.nn.softmax(hgrn_lb_logits.astype(F32), axis=0), axis=0)
    lb_all = lb_all - lb_all[0:1]

    zpad = jnp.zeros((RWKV_LORA, W), F32)
    hpad = jnp.zeros((1, LANE - M2_HEADS), F32)

    res = x.astype(F32)
    for l in range(depth):
        h = _prenorm(res, norm_pre[l])
        w1, w2 = _pack_w_in(w_in[l])
        p, u3 = _inproj(h, w1)
        p2 = _matmul(h, w2, P2_TN, BF16)

        lb = lb_all[l]
        o_a = _hgrn(p, _row(jnp.log(lb)), _row(jnp.log1p(-lb)), _row(1.0 - lb), _row(hgrn_norm[l]))

        tables = _s5_tables(s5_a_re[l].astype(F32), s5_a_im[l].astype(F32), s5_log_dt[l].astype(F32),
                            s5_b_re[l].astype(F32), s5_b_im[l].astype(F32),
                            s5_c_re[l].astype(F32), s5_c_im[l].astype(F32))
        o_b = _s5(u3, p, tables, _row(s5_d[l]), s5_w_glu[l].astype(BF16), _row(s5_b_glu[l]))

        mu = rwkv_mu[l].astype(F32)
        o_c = _rwkv(p, _row(mu[:3 * W]), _row(mu[3 * W:]), _row(rwkv_w0[l]),
                    jnp.concatenate([rwkv_w2[l].astype(F32), zpad], axis=0).astype(BF16),
                    _row(rwkv_a0[l]),
                    jnp.concatenate([zpad, rwkv_a2[l].astype(F32)], axis=0).astype(BF16),
                    _row(rwkv_k_k[l]), _row(rwkv_k_a[l]), _row(rwkv_r_k[l]),
                    _row(rwkv_ln_w[l]), _row(rwkv_ln_b[l]))

        o_d = _ssd(p, p2, m2_conv_w[l].astype(F32), _row(m2_conv_b[l]),
                   jnp.concatenate([_row(m2_dt_bias[l]), hpad], axis=1),
                   jnp.concatenate([-jnp.exp(_row(m2_a_log[l])), hpad], axis=1),
                   _row(jnp.repeat(m2_d[l].astype(F32), M2_HEADDIM)), _row(m2_norm[l]))

        merged = _up((o_a, o_b, o_c, o_d), p2, w_up[l].astype(BF16), gate_bias[l].astype(F32))
        res = _out(merged, w_out[l].astype(BF16), norm_post[l].astype(F32), res)
    return res.astype(x.dtype)
```

```python
import functools

import jax
import jax.numpy as jnp
from jax import lax
from jax.experimental import pallas as pl
from jax.experimental.pallas import tpu as pltpu

F32 = jnp.float32
BF16 = jnp.bfloat16
HI = lax.Precision.HIGHEST
LOG2E = 1.4426950408889634

D_MODEL = 2048
W = D_MODEL // 2
N_BRANCH = 4
NORM_EPS = 1e-6

HGRN_DK = 128
HGRN_CHUNK = 64
HGRN_SUB = 8
HGRN_HPS = 8

S5_GROUP = 16
S5_GROUPS = 64
S5_STATE = 64
S5_CHUNK = 16
S5_MAX_RE = -1e-4
S5_GBLK = 8

RWKV_HEAD = 64
RWKV_LORA = 64
RWKV_CHUNK = 64
RWKV_TILE = 4 * RWKV_CHUNK
RWKV_SUB = 16
RWKV_DECAY_SCALE = 0.606531
RWKV_LN_EPS = 64e-5
RWKV_KNORM_FLOOR = 1e-12

M2_HEADS = 16
M2_HEADDIM = 64
M2_GROUPS = 4
M2_STATE = 128
M2_CONV = 4
M2_CHUNK = 128
M2_NORM_EPS = 1e-5

LANE = 128
SUB = 8
VMEM_LIMIT = 56 * 1024 * 1024

COL_QA, COL_FA, COL_IA, COL_ZA = 0, 1, 2, 3
COL_UB, COL_ZB = 4, 5
COL_RC, COL_KC, COL_VC = 6, 7, 8
OFF_WA = 9 * W
OFF_ZC = OFF_WA + 2 * RWKV_LORA
OFF_XD = OFF_ZC + W
OFF_BCD = OFF_XD + W
OFF_DT = OFF_BCD + 2 * M2_GROUPS * M2_STATE
P1_COLS = OFF_DT + LANE
OFF_P2 = OFF_DT + M2_HEADS
P2_ZD, P2_GATE = 0, 1
P2_COLS = W + N_BRANCH * D_MODEL
P1_TN = 1792
P2_TN = 2304
U_TILE = (COL_UB * W) // P1_TN
U_OFF = COL_UB * W - U_TILE * P1_TN


def _cparams(sem):
    return pltpu.CompilerParams(dimension_semantics=sem, vmem_limit_bytes=VMEM_LIMIT)


def _dot(a, b):
    return jnp.dot(a.astype(BF16), b.astype(BF16), preferred_element_type=F32)


def _dot_nt(a, b):
    return lax.dot_general(a.astype(BF16), b.astype(BF16), (((1,), (1,)), ((), ())),
                           preferred_element_type=F32)


def _dot_tn(a, b):
    return lax.dot_general(a.astype(BF16), b.astype(BF16), (((0,), (0,)), ((), ())),
                           preferred_element_type=F32)


def _split(x, n):
    parts = []
    for _ in range(n):
        part = x.astype(BF16)
        parts.append(part)
        x = x - part.astype(F32)
    return parts


def _dot_sel(x, sel_tiled, n):
    return jnp.dot(jnp.concatenate(_split(x, n), axis=1), sel_tiled, preferred_element_type=F32)


def _silu(x):
    return x * jax.nn.sigmoid(x)


def _cumsum_rows(x):
    n = x.shape[0]
    row = lax.broadcasted_iota(jnp.int32, x.shape, 0)
    s = 1
    while s < n:
        x = x + jnp.where(row >= s, pltpu.roll(x, s, 0), 0.0)
        s *= 2
    return x


def _prenorm_kernel(x_ref, g_ref, o_ref):
    x = x_ref[...]
    o_ref[...] = (x * lax.rsqrt(jnp.mean(x * x, axis=-1, keepdims=True) + NORM_EPS)
                  * g_ref[...]).astype(o_ref.dtype)


def _prenorm(x, g):
    L = x.shape[1]
    tm = min(512, L)
    return pl.pallas_call(
        _prenorm_kernel,
        out_shape=jax.ShapeDtypeStruct((L, D_MODEL), BF16),
        grid=(L // tm,),
        in_specs=[pl.BlockSpec((None, tm, D_MODEL), lambda i: (0, i, 0)),
                  pl.BlockSpec((1, D_MODEL), lambda i: (0, 0))],
        out_specs=pl.BlockSpec((tm, D_MODEL), lambda i: (i, 0)),
        compiler_params=_cparams(("parallel",)),
    )(x, g.reshape(1, D_MODEL))


def _inproj_kernel(a_ref, w_ref, o_ref, u3_ref):
    acc = jnp.dot(a_ref[...], w_ref[...], preferred_element_type=F32)
    o_ref[...] = acc

    @pl.when(pl.program_id(1) == U_TILE)
    def _():
        u3_ref[...] = acc[:, U_OFF:U_OFF + W].reshape(u3_ref.shape)


def _inproj(a, w):
    M, K = a.shape
    tm = min(1024, M)
    return pl.pallas_call(
        _inproj_kernel,
        out_shape=(jax.ShapeDtypeStruct((M, P1_COLS), F32),
                   jax.ShapeDtypeStruct((M // S5_CHUNK, S5_CHUNK, W), F32)),
        grid=(M // tm, P1_COLS // P1_TN),
        in_specs=[pl.BlockSpec((tm, K), lambda i, j: (i, 0)),
                  pl.BlockSpec((K, P1_TN), lambda i, j: (0, j))],
        out_specs=(pl.BlockSpec((tm, P1_TN), lambda i, j: (i, j)),
                   pl.BlockSpec((tm // S5_CHUNK, S5_CHUNK, W), lambda i, j: (i, 0, 0))),
        compiler_params=_cparams(("parallel", "arbitrary")),
    )(a, w)


def _mm_kernel(a_ref, w_ref, o_ref):
    o_ref[...] = jnp.dot(a_ref[...], w_ref[...], preferred_element_type=F32).astype(o_ref.dtype)


def _matmul(a, w, tn, out_dtype):
    M, K = a.shape
    N = w.shape[1]
    tm = min(1024, M)
    return pl.pallas_call(
        _mm_kernel,
        out_shape=jax.ShapeDtypeStruct((M, N), out_dtype),
        grid=(M // tm, N // tn),
        in_specs=[pl.BlockSpec((tm, K), lambda i, j: (i, 0)),
                  pl.BlockSpec((K, tn), lambda i, j: (0, j))],
        out_specs=pl.BlockSpec((tm, tn), lambda i, j: (i, j)),
        compiler_params=_cparams(("parallel", "parallel")),
    )(a, w)


def _hgrn_kernel(q_ref, f_ref, i_ref, z_ref, loglb_ref, log1mlb_ref, omlb_ref, nw_ref,
                 o_ref, st_ref, *, n_chunks):
    @pl.when(pl.program_id(1) == 0)
    def _():
        st_ref[...] = jnp.zeros_like(st_ref)

    C, SB, DK = HGRN_CHUNK, HGRN_SUB, HGRN_DK
    NB = C // SB
    heads = range(HGRN_HPS)
    hs = [slice(j * DK, (j + 1) * DK) for j in heads]
    loglb, log1mlb, omlb, nw = loglb_ref[...], log1mlb_ref[...], omlb_ref[...], nw_ref[...]
    trow = lax.broadcasted_iota(jnp.int32, (SB, DK), 0)
    col_c = lax.broadcasted_iota(jnp.int32, (SB, C), 1)
    ones = jnp.ones((DK, DK), BF16)

    def chunk(c, carry):
        rows = pl.ds(pl.multiple_of(c * C, C), C)
        f = f_ref[rows, :]
        log_sig = jnp.minimum(f, 0.0) - jnp.log(1.0 + jnp.exp(-jnp.abs(f)))
        b = log1mlb + log_sig
        logf = jnp.maximum(loglb, b) + jnp.log(1.0 + jnp.exp(-jnp.abs(loglb - b)))
        cum_all = _cumsum_rows(logf) * LOG2E
        k_all = omlb * jax.nn.sigmoid(-f)
        q_all = _silu(q_ref[rows, :]) * (DK ** -0.5)
        cum = [cum_all[:, s_] for s_ in hs]
        k = [k_all[:, s_] for s_ in hs]
        q = [q_all[:, s_] for s_ in hs]
        v = [i_ref[rows, s_].astype(BF16) for s_ in hs]
        st = [st_ref[j] for j in heads]
        o = [_dot_nt(q[j] * jnp.exp2(cum[j]), st[j]) for j in heads]

        sc = [[None] * NB for _ in heads]
        for bi in range(1, NB):
            r0 = bi * SB
            for j in heads:
                ref = cum[j][r0:r0 + 1, :]
                kmod = k[j][0:r0, :] * jnp.exp2(ref - cum[j][0:r0, :])
                kmod = jnp.concatenate([kmod, jnp.zeros((C - r0, DK), F32)], axis=0)
                sc[j][bi] = _dot_nt(q[j][r0:r0 + SB, :] * jnp.exp2(cum[j][r0:r0 + SB, :] - ref), kmod)
        psum = []
        for j in heads:
            ps = []
            for bi in range(NB):
                r0 = bi * SB
                for s in range(SB):
                    d = jnp.where(trow >= s, cum[j][r0:r0 + SB, :] - cum[j][r0 + s:r0 + s + 1, :], -jnp.inf)
                    ps.append((q[j][r0:r0 + SB, :] * k[j][r0 + s:r0 + s + 1, :] * jnp.exp2(d)).astype(BF16))
            psum.append(jnp.dot(jnp.concatenate(ps, axis=0), ones, preferred_element_type=F32))
        for j in heads:
            blocks = []
            for bi in range(NB):
                acc = jnp.zeros((SB, C), F32) if bi == 0 else sc[j][bi]
                for s in range(SB):
                    r1 = (bi * SB + s) * SB
                    acc = jnp.where(col_c == bi * SB + s, psum[j][r1:r1 + SB, 0:C], acc)
                blocks.append(acc)
            o[j] = o[j] + _dot(jnp.concatenate(blocks, axis=0), v[j])

        for j in heads:
            last = cum[j][C - 1:C, :]
            st_ref[j] = st[j] * jnp.exp2(last) + _dot_tn(v[j], k[j] * jnp.exp2(last - cum[j]))
        for j in heads:
            oj = o[j] * lax.rsqrt(jnp.mean(o[j] * o[j], axis=-1, keepdims=True) + NORM_EPS) * nw[:, hs[j]]
            o_ref[rows, hs[j]] = (oj * _silu(z_ref[rows, hs[j]])).astype(o_ref.dtype)
        return carry

    lax.fori_loop(0, n_chunks, chunk, 0)


def _hgrn(p, loglb, log1mlb, omlb, nw):
    L = p.shape[0]
    T = min(512, L)
    nt = L // T
    gw = HGRN_HPS * HGRN_DK
    ng = W // gw
    blk = lambda c: pl.BlockSpec((T, gw), lambda h, t, c=c: (t, c * ng + h))
    vec = pl.BlockSpec((1, gw), lambda h, t: (0, h))
    return pl.pallas_call(
        functools.partial(_hgrn_kernel, n_chunks=T // HGRN_CHUNK),
        out_shape=jax.ShapeDtypeStruct((L, W), BF16),
        grid=(ng, nt),
        in_specs=[blk(COL_QA), blk(COL_FA), blk(COL_IA), blk(COL_ZA), vec, vec, vec, vec],
        out_specs=pl.BlockSpec((T, gw), lambda h, t: (t, h)),
        scratch_shapes=[pltpu.VMEM((HGRN_HPS, HGRN_DK, HGRN_DK), F32)],
        compiler_params=_cparams(("arbitrary", "arbitrary")),
    )(p, p, p, p, loglb, log1mlb, omlb, nw)


def _ssd_kernel(x_ref, bc_ref, z_ref, dt_ref, cw_ref, cb_ref, dtb_ref, a_ref, e_ref, d_ref, nw_ref,
                o_ref, xb_ref, st_ref):
    C = M2_CHUNK
    CH = 2 * W

    @pl.when(pl.program_id(0) == 0)
    def _():
        xb_ref[0:SUB, :] = jnp.zeros((SUB, CH), F32)
        st_ref[...] = jnp.zeros_like(st_ref)

    xb_ref[SUB:SUB + C, 0:W] = x_ref[...]
    xb_ref[SUB:SUB + C, W:CH] = bc_ref[...]
    acc = cb_ref[...] + cw_ref[0:1, :] * xb_ref[pl.ds(SUB - 3, C), :]
    for j in range(1, M2_CONV):
        acc = acc + cw_ref[j:j + 1, :] * xb_ref[pl.ds(SUB - 3 + j, C), :]
    xb_ref[0:SUB, :] = xb_ref[C:C + SUB, :]
    xc = _silu(acc)
    xh = xc[:, 0:W]

    dt = jax.nn.softplus(dt_ref[...] + dtb_ref[...])
    da = dt * a_ref[...]
    r_i = lax.broadcasted_iota(jnp.int32, (C, C), 0)
    c_i = lax.broadcasted_iota(jnp.int32, (C, C), 1)
    causal = r_i >= c_i
    causal3 = (lax.broadcasted_iota(jnp.int32, (C, 3 * C), 0)
               >= lax.broadcasted_iota(jnp.int32, (C, 3 * C), 1) % C).astype(BF16)
    a_cs = jnp.dot(causal3, jnp.concatenate(_split(da, 3), axis=0),
                   preferred_element_type=F32)
    a_cs_t = a_cs.T
    e = e_ref[...]
    ea = jnp.exp(a_cs)
    dt_e = _dot_sel(dt, e, 2)
    ea_e = _dot_sel(ea, e, 2)
    te_e = _dot_sel(jnp.exp(a_cs[C - 1:C, :] - a_cs), e, 2)
    xd = xh * dt_e
    xd_end = xd * te_e

    GW = W // M2_GROUPS
    lane_g = lax.broadcasted_iota(jnp.int32, (C, GW), 1)
    groups = range(M2_GROUPS)
    hpg = M2_HEADS // M2_GROUPS
    gsl = [slice(g * GW, (g + 1) * GW) for g in groups]
    c0 = W + M2_GROUPS * M2_STATE
    bg_t = [xc[:, W + g * M2_STATE:W + (g + 1) * M2_STATE].T.astype(BF16) for g in groups]
    cg = [xc[:, c0 + g * M2_STATE:c0 + (g + 1) * M2_STATE].astype(BF16) for g in groups]
    cb = [_dot(cg[g], bg_t[g]) for g in groups]
    y = [_dot(cg[g], st_ref[g]) * ea_e[:, gsl[g]] for g in groups]
    for g in groups:
        ms, xs = [], []
        for j in range(hpg):
            h = g * hpg + j
            seg = a_cs[:, h:h + 1] - a_cs_t[h:h + 1, :]
            ms.append((cb[g] * jnp.exp(jnp.where(causal, seg, -jnp.inf))).astype(BF16))
            xs.append(jnp.where(lane_g // M2_HEADDIM == j, xd[:, gsl[g]], 0.0).astype(BF16))
        y[g] = y[g] + _dot(jnp.concatenate(ms, axis=1), jnp.concatenate(xs, axis=0))
    for g in groups:
        st_ref[g] = st_ref[g] * ea_e[C - 1:C, gsl[g]] + _dot(bg_t[g], xd_end[:, gsl[g]])
    for g in groups:
        yg = (y[g] + d_ref[:, gsl[g]] * xh[:, gsl[g]]) * _silu(z_ref[:, gsl[g]].astype(F32))
        yg = yg * lax.rsqrt(jnp.mean(yg * yg, axis=-1, keepdims=True) + M2_NORM_EPS)
        o_ref[:, gsl[g]] = (yg * nw_ref[:, gsl[g]]).astype(o_ref.dtype)


def _ssd(p, p2, cw, cb, dtb, a, d_e, nw):
    L = p.shape[0]
    e = (jnp.arange(LANE)[:, None] == (jnp.arange(W) // M2_HEADDIM)[None, :]).astype(BF16)
    e = jnp.tile(e, (2, 1))
    C = M2_CHUNK
    at = lambda off: pl.BlockSpec((pl.Element(C), pl.Element(W)), lambda t, off=off: (t * C, off))
    full = lambda a_: pl.BlockSpec(a_.shape, lambda t: (0,) * a_.ndim)
    return pl.pallas_call(
        _ssd_kernel,
        out_shape=jax.ShapeDtypeStruct((L, W), BF16),
        grid=(L // C,),
        in_specs=[at(OFF_XD), at(OFF_BCD), pl.BlockSpec((C, W), lambda t: (t, P2_ZD)),
                  pl.BlockSpec((C, LANE), lambda t: (t, OFF_DT // LANE)),
                  full(cw), full(cb), full(dtb), full(a), full(e), full(d_e), full(nw)],
        out_specs=pl.BlockSpec((C, W), lambda t: (t, 0)),
        scratch_shapes=[pltpu.VMEM((SUB + C, 2 * W), F32),
                        pltpu.VMEM((M2_GROUPS, M2_STATE, W // M2_GROUPS), F32)],
        compiler_params=_cparams(("arbitrary",)),
    )(p, p, p2, p, cw, cb, dtb, a, e, d_e, nw)


def _rwkv_kernel(r_ref, k_ref, v_ref, wa_ref, z_ref, mu_ref, muwa_ref, w0_ref, w2_ref, a0_ref, a2_ref,
                 kk_ref, ka_ref, rk_ref, lnw_ref, lnb_ref,
                 o_ref, xb_ref, wab_ref, st_ref):
    C = RWKV_CHUNK
    T = o_ref.shape[0]

    @pl.when(pl.program_id(0) == 0)
    def _():
        xb_ref[0:SUB, :] = jnp.zeros((SUB, 3 * W), F32)
        wab_ref[0:SUB, :] = jnp.zeros((SUB, LANE), F32)
        st_ref[...] = jnp.zeros_like(st_ref)

    xb_ref[SUB:SUB + T, 0:W] = r_ref[...]
    xb_ref[SUB:SUB + T, W:2 * W] = k_ref[...]
    xb_ref[SUB:SUB + T, 2 * W:3 * W] = v_ref[...]
    wab_ref[SUB:SUB + T, :] = wa_ref[...]
    cur = xb_ref[SUB:SUB + T, :]
    xs = cur + mu_ref[...] * (xb_ref[pl.ds(SUB - 1, T), :] - cur)
    wa_cur = wab_ref[SUB:SUB + T, :]
    wa = wa_cur + muwa_ref[...] * (wab_ref[pl.ds(SUB - 1, T), :] - wa_cur)
    xb_ref[0:SUB, :] = xb_ref[T:T + SUB, :]
    wab_ref[0:SUB, :] = wab_ref[T:T + SUB, :]

    logw_all = -RWKV_DECAY_SCALE * jax.nn.sigmoid(w0_ref[...] + _dot(jnp.tanh(wa), w2_ref[...]))
    iclr_all = jax.nn.sigmoid(a0_ref[...] + _dot(wa, a2_ref[...]))

    lane = lax.broadcasted_iota(jnp.int32, (C, LANE), 1)
    lo = lane < RWKV_HEAD

    def hsum(x):
        s_lo = jnp.sum(jnp.where(lo, x, 0.0), axis=-1, keepdims=True)
        s_hi = jnp.sum(jnp.where(lo, 0.0, x), axis=-1, keepdims=True)
        return jnp.where(lo, s_lo, s_hi)
    t_i = lax.broadcasted_iota(jnp.int32, (C, 2 * C), 0)
    s_i = lax.broadcasted_iota(jnp.int32, (C, 2 * C), 1) % C
    strict = t_i > s_i
    incl_uv = (lax.broadcasted_iota(jnp.int32, (C, 4 * C), 0)
               >= lax.broadcasted_iota(jnp.int32, (C, 4 * C), 1) % C)
    r2 = lax.broadcasted_iota(jnp.int32, (2 * C, 2 * C), 0)
    c2 = lax.broadcasted_iota(jnp.int32, (2 * C, 2 * C), 1)
    same_blk = (r2 // RWKV_SUB) == (c2 // RWKV_SUB)
    eye = (r2 == c2).astype(F32)

    def stack(x):
        return jnp.concatenate([jnp.where(lo, x, 0.0), jnp.where(lo, 0.0, x)], axis=0)

    pairs = list(range(W // LANE))
    chunks = list(range(T // C))
    sls = [slice(i * LANE, (i + 1) * LANE) for i in pairs]
    rws = [slice(c * C, (c + 1) * C) for c in chunks]
    ds = [[dict() for _ in pairs] for _ in chunks]

    def p_keys(c, i):
        sl, rw, d = sls[i], rws[c], ds[c]
        k = xs[rw, W + i * LANE:W + (i + 1) * LANE]
        kk = k * kk_ref[:, sl]
        d[i].update(r=xs[rw, sl], v=xs[rw, 2 * W + i * LANE:2 * W + (i + 1) * LANE],
                    kk=kk, ss=hsum(kk * kk), k2=k * (1.0 + (iclr_all[rw, sl] - 1.0) * ka_ref[:, sl]))

    def p_norm(c, i):
        d = ds[c]
        d[i]['kk'] = d[i]['kk'] / jnp.maximum(jnp.sqrt(d[i]['ss']), RWKV_KNORM_FLOOR)

    def p_decay(c, i):
        cum = _cumsum_rows(logw_all[rws[c], sls[i]])
        ds[c][i].update(cum=cum, e_inv=jnp.exp(-cum))

    def p_ar(c, i):
        d = ds[c]
        cum = d[i]['cum']
        d[i]['ar'] = jnp.concatenate([-d[i]['kk'] * jnp.exp(cum - logw_all[rws[c], sls[i]]),
                                      d[i]['r'] * jnp.exp(cum)], axis=0).astype(BF16)

    def p_bk(c, i):
        d = ds[c]
        e_inv = d[i]['e_inv']
        d[i]['bk'] = jnp.concatenate([stack(d[i]['kk'] * iclr_all[rws[c], sls[i]] * e_inv),
                                      stack(d[i]['k2'] * e_inv)], axis=0).astype(BF16)
        d[i]['v_s'] = stack(d[i]['v']).astype(BF16)

    def s_scores(c, i):
        d = ds[c]
        d[i]['sc'] = _dot_nt(d[i]['ar'], d[i]['bk'])

    def s_state(c, i):
        d = ds[c]
        d[i]['ah'] = _dot_nt(d[i]['ar'], st_ref[i])

    def s_w(c, i):
        d = ds[c]
        sc = d[i]['sc']
        d[i]['w'] = d[i]['ah'][0:C, :] + _dot(jnp.where(strict, sc[0:C, 2 * C:4 * C], 0.0), d[i]['v_s'])

    def s_n(c, i):
        d = ds[c]
        n_p = stack(jnp.where(strict, d[i]['sc'][0:C, 0:2 * C], 0.0))
        n_d = jnp.where(same_blk, n_p, 0.0)
        d[i].update(n_p=n_p, n_d=n_d, pw=n_d, t_d=eye + n_d)

    def s_pow(c, i):
        d = ds[c]
        d[i]['pw'] = _dot(d[i]['pw'], d[i]['pw'])
        d[i]['t_d'] = d[i]['t_d'] + _dot(d[i]['pw'], d[i]['t_d'])

    def s_m(c, i):
        d = ds[c]
        d[i]['m'] = _dot(d[i]['t_d'], d[i]['n_p'] - d[i]['n_d'])
        d[i]['u'] = _dot(d[i]['t_d'], stack(d[i]['w']))

    def s_m2(c, i):
        d = ds[c]
        d[i]['m2'] = _dot(d[i]['m'], d[i]['m'])

    def s_u1(c, i):
        d = ds[c]
        d[i]['u'] = d[i]['u'] + _dot(d[i]['m2'], d[i]['u'])

    def s_u2(c, i):
        d = ds[c]
        d[i]['u'] = d[i]['u'] + _dot(d[i]['m'], d[i]['u'])

    def s_y(c, i):
        d = ds[c]
        uv = jnp.concatenate([d[i]['u'].astype(BF16), d[i]['v_s']], axis=0)
        d[i]['uv'] = uv
        d[i]['y'] = d[i]['ah'][C:2 * C, :] + _dot(jnp.where(incl_uv, d[i]['sc'][C:2 * C, :], 0.0), uv)

    def s_carry(c, i):
        d = ds[c]
        st_ref[i] = (st_ref[i] + _dot_tn(d[i]['uv'], d[i]['bk'])) * jnp.exp(d[i]['cum'][C - 1:C, :])

    def e_center(c, i):
        d = ds[c]
        d[i]['yc'] = d[i]['y'] - hsum(d[i]['y']) * (1.0 / RWKV_HEAD)

    def e_var(c, i):
        d = ds[c]
        d[i]['var'] = hsum(d[i]['yc'] * d[i]['yc']) * (1.0 / RWKV_HEAD)
        d[i]['bonus'] = hsum(d[i]['r'] * d[i]['k2'] * rk_ref[:, sls[i]]) * d[i]['v']

    def e_out(c, i):
        sl, rw, d = sls[i], rws[c], ds[c]
        yn = d[i]['yc'] * lax.rsqrt(d[i]['var'] + RWKV_LN_EPS) * lnw_ref[:, sl] + lnb_ref[:, sl]
        o_ref[rw, sl] = ((yn + d[i]['bonus']) * _silu(z_ref[rw, sl])).astype(o_ref.dtype)

    prep = (p_keys, p_norm, p_decay, p_ar, p_bk)
    solve = (s_scores, s_state, s_w, s_n, s_pow, s_pow, s_pow, s_m, s_m2, s_u1, s_u2, s_y, s_carry)
    finish = (e_center, e_var, e_out)

    def run(stage, c):
        for i in pairs:
            stage(c, i)

    for stage in prep:
        run(stage, 0)
    for c in chunks:
        fill = ([(s_, c + 1) for s_ in prep] if c + 1 < len(chunks) else []) \
            + ([(s_, c - 1) for s_ in finish] if c > 0 else [])
        for n, stage in enumerate(solve):
            run(stage, c)
            if n < len(fill):
                run(*fill[n])
        for extra in fill[len(solve):]:
            run(*extra)
    for stage in finish:
        run(stage, chunks[-1])


def _rwkv(p, mu, muwa, w0, w2p, a0, a2p, k_k, k_a, r_k, ln_w, ln_b):
    L = p.shape[0]
    C = min(RWKV_TILE, L)
    wide = lambda c: pl.BlockSpec((C, W), lambda t, c=c: (t, c))
    full = lambda a_: pl.BlockSpec(a_.shape, lambda t: (0,) * a_.ndim)
    consts = (mu, muwa, w0, w2p, a0, a2p, k_k, k_a, r_k, ln_w, ln_b)
    return pl.pallas_call(
        _rwkv_kernel,
        out_shape=jax.ShapeDtypeStruct((L, W), BF16),
        grid=(L // C,),
        in_specs=[wide(COL_RC), wide(COL_KC), wide(COL_VC),
                  pl.BlockSpec((C, LANE), lambda t: (t, OFF_WA // LANE)),
                  pl.BlockSpec((pl.Element(C), pl.Element(W)), lambda t: (t * C, OFF_ZC))]
                 + [full(c) for c in consts],
        out_specs=pl.BlockSpec((C, W), lambda t: (t, 0)),
        scratch_shapes=[pltpu.VMEM((SUB + C, 3 * W), F32),
                        pltpu.VMEM((SUB + C, LANE), F32),
                        pltpu.VMEM((W // LANE, LANE, LANE), F32)],
        compiler_params=_cparams(("arbitrary",)),
    )(p, p, p, p, p, *consts)


def _s5_kernel(u_ref, perm_ref, m_ref, toe_ref, q_ref, l1_ref, l2_ref, l2s_ref, y_ref,
               xl_ref, xls_ref, xp_ref, car_ref):
    @pl.when(pl.program_id(1) == 0)
    def _():
        car_ref[...] = jnp.zeros_like(car_ref)

    tn = u_ref.shape[0]
    gw = S5_CHUNK * S5_GROUP
    perm = perm_ref[...]
    lhs = jnp.concatenate([u_ref[:, j, :].astype(BF16) for j in range(S5_CHUNK)], axis=1)
    lhs = jnp.dot(lhs, perm, preferred_element_type=F32).astype(BF16)
    lg = [lhs[:, g * gw:(g + 1) * gw] for g in range(S5_GBLK)]
    xl = jnp.concatenate([jnp.dot(lg[g], m_ref[g], preferred_element_type=F32) for g in range(S5_GBLK)],
                         axis=1)
    xl_ref[...] = xl
    xls_ref[...] = jnp.concatenate(
        [pltpu.roll(xl[:, g * LANE:(g + 1) * LANE], S5_STATE, 1) for g in range(S5_GBLK)], axis=1)
    l1, l2, l2s = l1_ref[...], l2_ref[...], l2s_ref[...]

    def step(i, carry):
        x, xs = carry
        row = pl.ds(i, 1)
        xp_ref[row, :] = x
        return (l1 * x + l2 * xs + xl_ref[row, :], l1 * xs + l2s * x + xls_ref[row, :])

    x, xs = lax.fori_loop(0, tn, step, (car_ref[0:1, :], car_ref[1:2, :]))
    car_ref[0:1, :] = x
    car_ref[1:2, :] = xs
    xp = xp_ref[...].astype(BF16)
    y = jnp.concatenate(
        [jnp.dot(lg[g], toe_ref[g], preferred_element_type=F32)
         + jnp.dot(xp[:, g * LANE:(g + 1) * LANE], q_ref[g], preferred_element_type=F32)
         for g in range(S5_GBLK)], axis=1)
    y = lax.dot_general(y.astype(BF16), perm, (((1,), (1,)), ((), ())),
                        preferred_element_type=F32)
    for t in range(S5_CHUNK):
        y_ref[:, t, :] = y[:, t * LANE:(t + 1) * LANE]


def _s5_scan(u3, m, toe, q, l1, l2, l2s):
    n = u3.shape[0]
    tn = min(512, n)
    nb = W // LANE
    sw = S5_GBLK * 2 * S5_STATE
    kw = S5_CHUNK * LANE
    r = jnp.arange(kw)
    dst = (r // S5_GROUP % S5_GBLK) * (S5_CHUNK * S5_GROUP) + (r // LANE) * S5_GROUP + r % S5_GROUP
    perm = (dst[:, None] == jnp.arange(kw)[None, :]).astype(BF16)
    per_b = lambda a_: pl.BlockSpec((S5_GBLK,) + a_.shape[1:], lambda b, i: (b, 0, 0))
    vec = pl.BlockSpec((None, 1, sw), lambda b, i: (b, 0, 0))
    return pl.pallas_call(
        _s5_kernel,
        out_shape=jax.ShapeDtypeStruct((n, S5_CHUNK, W), F32),
        grid=(nb, n // tn),
        in_specs=[pl.BlockSpec((tn, S5_CHUNK, LANE), lambda b, i: (i, 0, b)),
                  pl.BlockSpec((kw, kw), lambda b, i: (0, 0)),
                  per_b(m), per_b(toe), per_b(q), vec, vec, vec],
        out_specs=pl.BlockSpec((tn, S5_CHUNK, LANE), lambda b, i: (i, 0, b)),
        scratch_shapes=[pltpu.VMEM((tn, sw), F32)] * 3 + [pltpu.VMEM((SUB, sw), F32)],
        compiler_params=_cparams(("arbitrary", "arbitrary")),
    )(u3, perm, m, toe, q, l1, l2, l2s)


def _s5_out_kernel(y_ref, u_ref, z_ref, d_ref, wg_ref, bg_ref, o_ref):
    y = y_ref[...].reshape(u_ref.shape) + d_ref[...] * u_ref[...]
    g = jax.nn.gelu(y)
    o = g * jax.nn.sigmoid(_dot(g, wg_ref[...]) + bg_ref[...])
    o_ref[...] = (o * _silu(z_ref[...])).astype(o_ref.dtype)


def _s5_out(y3, p, d, wg, bg):
    L = p.shape[0]
    tm = min(512, L)
    vec = pl.BlockSpec((1, W), lambda t: (0, 0))
    return pl.pallas_call(
        _s5_out_kernel,
        out_shape=jax.ShapeDtypeStruct((L, W), BF16),
        grid=(L // tm,),
        in_specs=[pl.BlockSpec((tm // S5_CHUNK, S5_CHUNK, W), lambda t: (t, 0, 0)),
                  pl.BlockSpec((tm, W), lambda t: (t, COL_UB)),
                  pl.BlockSpec((tm, W), lambda t: (t, COL_ZB)),
                  vec, pl.BlockSpec((W, W), lambda t: (0, 0)), vec],
        out_specs=pl.BlockSpec((tm, W), lambda t: (t, 0)),
        compiler_params=_cparams(("parallel",)),
    )(y3, p, p, d, wg, bg)


def _s5_tables(a_re, a_im, log_dt, b_re, b_im, c_re, c_im):
    J = S5_CHUNK
    lam_re = jnp.minimum(a_re, S5_MAX_RE)
    lam_im = a_im
    dt = jnp.exp(log_dt)[:, None]
    dl_re, dl_im = lam_re * dt, lam_im * dt
    mag = jnp.exp(dl_re)
    num_re = mag * jnp.cos(dl_im) - 1.0
    num_im = mag * jnp.sin(dl_im)
    den = lam_re * lam_re + lam_im * lam_im
    coef_re = (num_re * lam_re + num_im * lam_im) / den
    coef_im = (num_im * lam_re - num_re * lam_im) / den
    bb_re = coef_re[..., None] * b_re - coef_im[..., None] * b_im
    bb_im = coef_re[..., None] * b_im + coef_im[..., None] * b_re
    tau = jnp.arange(J + 1, dtype=F32)[:, None, None]
    pm = jnp.exp(tau * dl_re)
    pw_re, pw_im = pm * jnp.cos(tau * dl_im), pm * jnp.sin(tau * dl_im)
    pb_re = pw_re[..., None] * bb_re - pw_im[..., None] * bb_im
    pb_im = pw_re[..., None] * bb_im + pw_im[..., None] * bb_re
    kern = (jnp.einsum('gop,tgph->tgoh', c_re, pb_re[:J], precision=HI)
            - jnp.einsum('gop,tgph->tgoh', c_im, pb_im[:J], precision=HI))
    t_idx = jnp.arange(J)
    lag = t_idx[None, :] - t_idx[:, None]
    onehot = (lag[:, :, None] == t_idx[None, None, :]).astype(F32)
    toe = jnp.einsum('stu,ugoh->stgoh', onehot, kern, precision=HI)
    nb, gb = S5_GROUPS // S5_GBLK, S5_GBLK
    toe = toe.transpose(2, 0, 4, 1, 3).reshape(S5_GROUPS, J * S5_GROUP, J * S5_GROUP)
    rev_re, rev_im = pb_re[J - 1 - t_idx], pb_im[J - 1 - t_idx]
    to_m = lambda x: x.transpose(1, 0, 3, 2).reshape(S5_GROUPS, J * S5_GROUP, S5_STATE)
    m = jnp.concatenate([to_m(rev_re), to_m(rev_im)], axis=-1)
    q_re = (c_re[None] * pw_re[1:, :, None, :] - c_im[None] * pw_im[1:, :, None, :])
    q_im = -(c_re[None] * pw_im[1:, :, None, :] + c_im[None] * pw_re[1:, :, None, :])
    to_q = lambda x: x.transpose(1, 3, 0, 2).reshape(S5_GROUPS, S5_STATE, J * S5_GROUP)
    q = jnp.concatenate([to_q(q_re), to_q(q_im)], axis=1)
    lj_re, lj_im = pw_re[J], pw_im[J]
    rs = lambda parts: jnp.concatenate(parts, axis=-1).reshape(nb, 1, gb * 2 * S5_STATE)
    return (m.astype(BF16), toe.astype(BF16), q.astype(BF16),
            rs([lj_re, lj_re]), rs([-lj_im, lj_im]), rs([lj_im, -lj_im]))


def _s5(u3, p, tables, d, wg, bg):
    return _s5_out(_s5_scan(u3, *tables), p, d, wg, bg)


def _up_kernel(oa_ref, ob_ref, oc_ref, od_ref, ga_ref, gb_ref, gc_ref, gd_ref, w_ref, bias_ref, o_ref):
    acc = None
    for b, (o_b, g_b) in enumerate(((oa_ref, ga_ref), (ob_ref, gb_ref), (oc_ref, gc_ref), (od_ref, gd_ref))):
        gate = jax.nn.sigmoid(g_b[...] + bias_ref[b:b + 1, :])
        contrib = gate * jnp.dot(o_b[...], w_ref[b], preferred_element_type=F32)
        acc = contrib if acc is None else acc + contrib
    o_ref[...] = acc.astype(o_ref.dtype)


def _up(outs, p, w_up, bias):
    L = p.shape[0]
    tm = min(1024, L)
    tn = 512
    nb = D_MODEL // tn
    o_spec = pl.BlockSpec((tm, W), lambda i, j: (i, 0))
    g_spec = lambda b: pl.BlockSpec((tm, tn), lambda i, j, b=b: (i, (P2_GATE * W + b * D_MODEL) // tn + j))
    return pl.pallas_call(
        _up_kernel,
        out_shape=jax.ShapeDtypeStruct((L, D_MODEL), BF16),
        grid=(L // tm, nb),
        in_specs=[o_spec] * 4 + [g_spec(b) for b in range(N_BRANCH)]
                 + [pl.BlockSpec((N_BRANCH, W, tn), lambda i, j: (0, 0, j)),
                    pl.BlockSpec((N_BRANCH, tn), lambda i, j: (0, j))],
        out_specs=pl.BlockSpec((tm, tn), lambda i, j: (i, j)),
        compiler_params=_cparams(("parallel", "arbitrary")),
    )(*outs, p, p, p, p, w_up, bias)


def _out_kernel(m_ref, w_ref, g_ref, res_ref, o_ref):
    y = jnp.dot(m_ref[...], w_ref[...], preferred_element_type=F32)
    o_ref[...] = res_ref[...] + (y * lax.rsqrt(jnp.mean(y * y, axis=-1, keepdims=True) + NORM_EPS)
                                 * g_ref[...])


def _out(merged, w_out, g, res):
    L = res.shape[1]
    tm = min(512, L)
    return pl.pallas_call(
        _out_kernel,
        out_shape=jax.ShapeDtypeStruct((1, L, D_MODEL), F32),
        grid=(L // tm,),
        in_specs=[pl.BlockSpec((tm, D_MODEL), lambda i: (i, 0)),
                  pl.BlockSpec((D_MODEL, D_MODEL), lambda i: (0, 0)),
                  pl.BlockSpec((1, D_MODEL), lambda i: (0, 0)),
                  pl.BlockSpec((None, tm, D_MODEL), lambda i: (0, i, 0))],
        out_specs=pl.BlockSpec((None, tm, D_MODEL), lambda i: (0, i, 0)),
        compiler_params=_cparams(("parallel",)),
    )(merged, w_out, g.reshape(1, D_MODEL), res)


def _pack_w_in(w):
    wb = w.astype(BF16)
    return wb, wb[:, OFF_P2:OFF_P2 + P2_COLS]


def _row(v):
    return v.astype(F32).reshape(1, -1)


def kernel(x, norm_pre, norm_post, w_in, gate_bias, w_up, w_out, hgrn_lb_logits, hgrn_norm, s5_a_re, s5_a_im, s5_log_dt, s5_b_re, s5_b_im, s5_c_re, s5_c_im, s5_d, s5_w_glu, s5_b_glu, rwkv_mu, rwkv_w0, rwkv_w2, rwkv_a0, rwkv_a2, rwkv_k_k, rwkv_k_a, rwkv_r_k, rwkv_ln_w, rwkv_ln_b, m2_conv_w, m2_conv_b, m2_dt_bias, m2_a_log, m2_d, m2_norm):
    depth = w_in.shape[0]
    L = x.shape[1]
    lb_all = jnp.cumsum(jax.nn.softmax(hgrn_lb_logits.astype(F32), axis=0), axis=0)
    lb_all = lb_all - lb_all[0:1]

    zpad = jnp.zeros((RWKV_LORA, W), F32)
    hpad = jnp.zeros((1, LANE - M2_HEADS), F32)

    res = x.astype(F32)
    for l in range(depth):
        h = _prenorm(res, norm_pre[l])
        w1, w2 = _pack_w_in(w_in[l])
        p, u3 = _inproj(h, w1)
        p2 = _matmul(h, w2, P2_TN, BF16)

        lb = lb_all[l]
        o_a = _hgrn(p, _row(jnp.log(lb)), _row(jnp.log1p(-lb)), _row(1.0 - lb), _row(hgrn_norm[l]))

        tables = _s5_tables(s5_a_re[l].astype(F32), s5_a_im[l].astype(F32), s5_log_dt[l].astype(F32),
                            s5_b_re[l].astype(F32), s5_b_im[l].astype(F32),
                            s5_c_re[l].astype(F32), s5_c_im[l].astype(F32))
        o_b = _s5(u3, p, tables, _row(s5_d[l]), s5_w_glu[l].astype(BF16), _row(s5_b_glu[l]))

        mu = rwkv_mu[l].astype(F32)
        o_c = _rwkv(p, _row(mu[:3 * W]), _row(mu[3 * W:]), _row(rwkv_w0[l]),
                    jnp.concatenate([rwkv_w2[l].astype(F32), zpad], axis=0).astype(BF16),
                    _row(rwkv_a0[l]),
                    jnp.concatenate([zpad, rwkv_a2[l].astype(F32)], axis=0).astype(BF16),
                    _row(rwkv_k_k[l]), _row(rwkv_k_a[l]), _row(rwkv_r_k[l]),
                    _row(rwkv_ln_w[l]), _row(rwkv_ln_b[l]))

        o_d = _ssd(p, p2, m2_conv_w[l].astype(F32), _row(m2_conv_b[l]),
                   jnp.concatenate([_row(m2_dt_bias[l]), hpad], axis=1),
                   jnp.concatenate([-jnp.exp(_row(m2_a_log[l])), hpad], axis=1),
                   _row(jnp.repeat(m2_d[l].astype(F32), M2_HEADDIM)), _row(m2_norm[l]))

        merged = _up((o_a, o_b, o_c, o_d), p2, w_up[l].astype(BF16), gate_bias[l].astype(F32))
        res = _out(merged, w_out[l].astype(BF16), norm_post[l].astype(F32), res)
    return res.astype(x.dtype)
```

```python
import functools

import jax
import jax.numpy as jnp
from jax import lax
from jax.experimental import pallas as pl
from jax.experimental.pallas import tpu as pltpu

F32 = jnp.float32
BF16 = jnp.bfloat16
HI = lax.Precision.HIGHEST
LOG2E = 1.4426950408889634

D_MODEL = 2048
W = D_MODEL // 2
N_BRANCH = 4
NORM_EPS = 1e-6

HGRN_DK = 128
HGRN_CHUNK = 64
HGRN_SUB = 8
HGRN_HPS = 8

S5_GROUP = 16
S5_GROUPS = 64
S5_STATE = 64
S5_CHUNK = 16
S5_MAX_RE = -1e-4
S5_GBLK = 8

RWKV_HEAD = 64
RWKV_LORA = 64
RWKV_CHUNK = 64
RWKV_TILE = 4 * RWKV_CHUNK
RWKV_SUB = 16
RWKV_DECAY_SCALE = 0.606531
RWKV_LN_EPS = 64e-5
RWKV_KNORM_FLOOR = 1e-12

M2_HEADS = 16
M2_HEADDIM = 64
M2_GROUPS = 4
M2_STATE = 128
M2_CONV = 4
M2_CHUNK = 128
M2_NORM_EPS = 1e-5

LANE = 128
SUB = 8
VMEM_LIMIT = 56 * 1024 * 1024

COL_QA, COL_FA, COL_IA, COL_ZA = 0, 1, 2, 3
COL_UB, COL_ZB = 4, 5
COL_RC, COL_KC, COL_VC = 6, 7, 8
OFF_WA = 9 * W
OFF_ZC = OFF_WA + 2 * RWKV_LORA
OFF_XD = OFF_ZC + W
OFF_BCD = OFF_XD + W
OFF_DT = OFF_BCD + 2 * M2_GROUPS * M2_STATE
P1_COLS = OFF_DT + LANE
OFF_P2 = OFF_DT + M2_HEADS
P2_ZD, P2_GATE = 0, 1
P2_COLS = W + N_BRANCH * D_MODEL
P1_TN = 1792
P2_TN = 2304
U_TILE = (COL_UB * W) // P1_TN
U_OFF = COL_UB * W - U_TILE * P1_TN


def _cparams(sem):
    return pltpu.CompilerParams(dimension_semantics=sem, vmem_limit_bytes=VMEM_LIMIT)


def _dot(a, b):
    return jnp.dot(a.astype(BF16), b.astype(BF16), preferred_element_type=F32)


def _dot_nt(a, b):
    return lax.dot_general(a.astype(BF16), b.astype(BF16), (((1,), (1,)), ((), ())),
                           preferred_element_type=F32)


def _dot_tn(a, b):
    return lax.dot_general(a.astype(BF16), b.astype(BF16), (((0,), (0,)), ((), ())),
                           preferred_element_type=F32)


def _split(x, n):
    parts = []
    for _ in range(n):
        part = x.astype(BF16)
        parts.append(part)
        x = x - part.astype(F32)
    return parts


def _dot_sel(x, sel_tiled, n):
    return jnp.dot(jnp.concatenate(_split(x, n), axis=1), sel_tiled, preferred_element_type=F32)


def _silu(x):
    return x * jax.nn.sigmoid(x)


def _cumsum_rows(x):
    n = x.shape[0]
    row = lax.broadcasted_iota(jnp.int32, x.shape, 0)
    s = 1
    while s < n:
        x = x + jnp.where(row >= s, pltpu.roll(x, s, 0), 0.0)
        s *= 2
    return x


def _prenorm_kernel(x_ref, g_ref, o_ref):
    x = x_ref[...]
    o_ref[...] = (x * lax.rsqrt(jnp.mean(x * x, axis=-1, keepdims=True) + NORM_EPS)
                  * g_ref[...]).astype(o_ref.dtype)


def _prenorm(x, g):
    L = x.shape[1]
    tm = min(512, L)
    return pl.pallas_call(
        _prenorm_kernel,
        out_shape=jax.ShapeDtypeStruct((L, D_MODEL), BF16),
        grid=(L // tm,),
        in_specs=[pl.BlockSpec((None, tm, D_MODEL), lambda i: (0, i, 0)),
                  pl.BlockSpec((1, D_MODEL), lambda i: (0, 0))],
        out_specs=pl.BlockSpec((tm, D_MODEL), lambda i: (i, 0)),
        compiler_params=_cparams(("parallel",)),
    )(x, g.reshape(1, D_MODEL))


def _inproj_kernel(a_ref, w_ref, o_ref, u3_ref):
    acc = jnp.dot(a_ref[...], w_ref[...], preferred_element_type=F32)
    o_ref[...] = acc

    @pl.when(pl.program_id(1) == U_TILE)
    def _():
        u3_ref[...] = acc[:, U_OFF:U_OFF + W].reshape(u3_ref.shape)


def _inproj(a, w):
    M, K = a.shape
    tm = min(1024, M)
    return pl.pallas_call(
        _inproj_kernel,
        out_shape=(jax.ShapeDtypeStruct((M, P1_COLS), F32),
                   jax.ShapeDtypeStruct((M // S5_CHUNK, S5_CHUNK, W), F32)),
        grid=(M // tm, P1_COLS // P1_TN),
        in_specs=[pl.BlockSpec((tm, K), lambda i, j: (i, 0)),
                  pl.BlockSpec((K, P1_TN), lambda i, j: (0, j))],
        out_specs=(pl.BlockSpec((tm, P1_TN), lambda i, j: (i, j)),
                   pl.BlockSpec((tm // S5_CHUNK, S5_CHUNK, W), lambda i, j: (i, 0, 0))),
        compiler_params=_cparams(("parallel", "arbitrary")),
    )(a, w)


def _mm_kernel(a_ref, w_ref, o_ref):
    o_ref[...] = jnp.dot(a_ref[...], w_ref[...], preferred_element_type=F32).astype(o_ref.dtype)


def _matmul(a, w, tn, out_dtype):
    M, K = a.shape
    N = w.shape[1]
    tm = min(1024, M)
    return pl.pallas_call(
        _mm_kernel,
        out_shape=jax.ShapeDtypeStruct((M, N), out_dtype),
        grid=(M // tm, N // tn),
        in_specs=[pl.BlockSpec((tm, K), lambda i, j: (i, 0)),
                  pl.BlockSpec((K, tn), lambda i, j: (0, j))],
        out_specs=pl.BlockSpec((tm, tn), lambda i, j: (i, j)),
        compiler_params=_cparams(("parallel", "parallel")),
    )(a, w)


def _hgrn_kernel(q_ref, f_ref, i_ref, z_ref, loglb_ref, log1mlb_ref, omlb_ref, nw_ref,
                 o_ref, st_ref, *, n_chunks):
    @pl.when(pl.program_id(1) == 0)
    def _():
        st_ref[...] = jnp.zeros_like(st_ref)

    C, SB, DK = HGRN_CHUNK, HGRN_SUB, HGRN_DK
    NB = C // SB
    heads = range(HGRN_HPS)
    hs = [slice(j * DK, (j + 1) * DK) for j in heads]
    loglb, log1mlb, omlb, nw = loglb_ref[...], log1mlb_ref[...], omlb_ref[...], nw_ref[...]
    trow = lax.broadcasted_iota(jnp.int32, (SB, DK), 0)
    col_c = lax.broadcasted_iota(jnp.int32, (SB, C), 1)
    ones = jnp.ones((DK, DK), BF16)

    def chunk(c, carry):
        rows = pl.ds(pl.multiple_of(c * C, C), C)
        f = f_ref[rows, :]
        log_sig = jnp.minimum(f, 0.0) - jnp.log(1.0 + jnp.exp(-jnp.abs(f)))
        b = log1mlb + log_sig
        logf = jnp.maximum(loglb, b) + jnp.log(1.0 + jnp.exp(-jnp.abs(loglb - b)))
        cum_all = _cumsum_rows(logf) * LOG2E
        k_all = omlb * jax.nn.sigmoid(-f)
        q_all = _silu(q_ref[rows, :]) * (DK ** -0.5)
        cum = [cum_all[:, s_] for s_ in hs]
        k = [k_all[:, s_] for s_ in hs]
        q = [q_all[:, s_] for s_ in hs]
        v = [i_ref[rows, s_].astype(BF16) for s_ in hs]
        st = [st_ref[j] for j in heads]
        o = [_dot_nt(q[j] * jnp.exp2(cum[j]), st[j]) for j in heads]

        sc = [[None] * NB for _ in heads]
        for bi in range(1, NB):
            r0 = bi * SB
            for j in heads:
                ref = cum[j][r0:r0 + 1, :]
                kmod = k[j][0:r0, :] * jnp.exp2(ref - cum[j][0:r0, :])
                kmod = jnp.concatenate([kmod, jnp.zeros((C - r0, DK), F32)], axis=0)
                sc[j][bi] = _dot_nt(q[j][r0:r0 + SB, :] * jnp.exp2(cum[j][r0:r0 + SB, :] - ref), kmod)
        psum = []
        for j in heads:
            ps = []
            for bi in range(NB):
                r0 = bi * SB
                for s in range(SB):
                    d = jnp.where(trow >= s, cum[j][r0:r0 + SB, :] - cum[j][r0 + s:r0 + s + 1, :], -jnp.inf)
                    ps.append((q[j][r0:r0 + SB, :] * k[j][r0 + s:r0 + s + 1, :] * jnp.exp2(d)).astype(BF16))
            psum.append(jnp.dot(jnp.concatenate(ps, axis=0), ones, preferred_element_type=F32))
        for j in heads:
            blocks = []
            for bi in range(NB):
                acc = jnp.zeros((SB, C), F32) if bi == 0 else sc[j][bi]
                for s in range(SB):
                    r1 = (bi * SB + s) * SB
                    acc = jnp.where(col_c == bi * SB + s, psum[j][r1:r1 + SB, 0:C], acc)
                blocks.append(acc)
            o[j] = o[j] + _dot(jnp.concatenate(blocks, axis=0), v[j])

        for j in heads:
            last = cum[j][C - 1:C, :]
            st_ref[j] = st[j] * jnp.exp2(last) + _dot_tn(v[j], k[j] * jnp.exp2(last - cum[j]))
        for j in heads:
            oj = o[j] * lax.rsqrt(jnp.mean(o[j] * o[j], axis=-1, keepdims=True) + NORM_EPS) * nw[:, hs[j]]
            o_ref[rows, hs[j]] = (oj * _silu(z_ref[rows, hs[j]])).astype(o_ref.dtype)
        return carry

    lax.fori_loop(0, n_chunks, chunk, 0)


def _hgrn(p, loglb, log1mlb, omlb, nw):
    L = p.shape[0]
    T = min(512, L)
    nt = L // T
    gw = HGRN_HPS * HGRN_DK
    ng = W // gw
    blk = lambda c: pl.BlockSpec((T, gw), lambda h, t, c=c: (t, c * ng + h))
    vec = pl.BlockSpec((1, gw), lambda h, t: (0, h))
    return pl.pallas_call(
        functools.partial(_hgrn_kernel, n_chunks=T // HGRN_CHUNK),
        out_shape=jax.ShapeDtypeStruct((L, W), BF16),
        grid=(ng, nt),
        in_specs=[blk(COL_QA), blk(COL_FA), blk(COL_IA), blk(COL_ZA), vec, vec, vec, vec],
        out_specs=pl.BlockSpec((T, gw), lambda h, t: (t, h)),
        scratch_shapes=[pltpu.VMEM((HGRN_HPS, HGRN_DK, HGRN_DK), F32)],
        compiler_params=_cparams(("arbitrary", "arbitrary")),
    )(p, p, p, p, loglb, log1mlb, omlb, nw)


def _ssd_kernel(x_ref, bc_ref, z_ref, dt_ref, cw_ref, cb_ref, dtb_ref, a_ref, e_ref, d_ref, nw_ref,
                o_ref, xb_ref, st_ref):
    C = M2_CHUNK
    CH = 2 * W

    @pl.when(pl.program_id(0) == 0)
    def _():
        xb_ref[0:SUB, :] = jnp.zeros((SUB, CH), F32)
        st_ref[...] = jnp.zeros_like(st_ref)

    xb_ref[SUB:SUB + C, 0:W] = x_ref[...]
    xb_ref[SUB:SUB + C, W:CH] = bc_ref[...]
    acc = cb_ref[...] + cw_ref[0:1, :] * xb_ref[pl.ds(SUB - 3, C), :]
    for j in range(1, M2_CONV):
        acc = acc + cw_ref[j:j + 1, :] * xb_ref[pl.ds(SUB - 3 + j, C), :]
    xb_ref[0:SUB, :] = xb_ref[C:C + SUB, :]
    xc = _silu(acc)
    xh = xc[:, 0:W]

    dt = jax.nn.softplus(dt_ref[...] + dtb_ref[...])
    da = dt * a_ref[...]
    r_i = lax.broadcasted_iota(jnp.int32, (C, C), 0)
    c_i = lax.broadcasted_iota(jnp.int32, (C, C), 1)
    causal = r_i >= c_i
    causal3 = (lax.broadcasted_iota(jnp.int32, (C, 3 * C), 0)
               >= lax.broadcasted_iota(jnp.int32, (C, 3 * C), 1) % C).astype(BF16)
    a_cs = jnp.dot(causal3, jnp.concatenate(_split(da, 3), axis=0),
                   preferred_element_type=F32)
    a_cs_t = a_cs.T
    e = e_ref[...]
    ea = jnp.exp(a_cs)
    dt_e = _dot_sel(dt, e, 2)
    ea_e = _dot_sel(ea, e, 2)
    te_e = _dot_sel(jnp.exp(a_cs[C - 1:C, :] - a_cs), e, 2)
    xd = xh * dt_e
    xd_end = xd * te_e

    GW = W // M2_GROUPS
    lane_g = lax.broadcasted_iota(jnp.int32, (C, GW), 1)
    groups = range(M2_GROUPS)
    hpg = M2_HEADS // M2_GROUPS
    gsl = [slice(g * GW, (g + 1) * GW) for g in groups]
    c0 = W + M2_GROUPS * M2_STATE
    bg_t = [xc[:, W + g * M2_STATE:W + (g + 1) * M2_STATE].T.astype(BF16) for g in groups]
    cg = [xc[:, c0 + g * M2_STATE:c0 + (g + 1) * M2_STATE].astype(BF16) for g in groups]
    cb = [_dot(cg[g], bg_t[g]) for g in groups]
    y = [_dot(cg[g], st_ref[g]) * ea_e[:, gsl[g]] for g in groups]
    for g in groups:
        ms, xs = [], []
        for j in range(hpg):
            h = g * hpg + j
            seg = a_cs[:, h:h + 1] - a_cs_t[h:h + 1, :]
            ms.append((cb[g] * jnp.exp(jnp.where(causal, seg, -jnp.inf))).astype(BF16))
            xs.append(jnp.where(lane_g // M2_HEADDIM == j, xd[:, gsl[g]], 0.0).astype(BF16))
        y[g] = y[g] + _dot(jnp.concatenate(ms, axis=1), jnp.concatenate(xs, axis=0))
    for g in groups:
        st_ref[g] = st_ref[g] * ea_e[C - 1:C, gsl[g]] + _dot(bg_t[g], xd_end[:, gsl[g]])
    for g in groups:
        yg = (y[g] + d_ref[:, gsl[g]] * xh[:, gsl[g]]) * _silu(z_ref[:, gsl[g]].astype(F32))
        yg = yg * lax.rsqrt(jnp.mean(yg * yg, axis=-1, keepdims=True) + M2_NORM_EPS)
        o_ref[:, gsl[g]] = (yg * nw_ref[:, gsl[g]]).astype(o_ref.dtype)


def _ssd(p, p2, cw, cb, dtb, a, d_e, nw):
    L = p.shape[0]
    e = (jnp.arange(LANE)[:, None] == (jnp.arange(W) // M2_HEADDIM)[None, :]).astype(BF16)
    e = jnp.tile(e, (2, 1))
    C = M2_CHUNK
    at = lambda off: pl.BlockSpec((pl.Element(C), pl.Element(W)), lambda t, off=off: (t * C, off))
    full = lambda a_: pl.BlockSpec(a_.shape, lambda t: (0,) * a_.ndim)
    return pl.pallas_call(
        _ssd_kernel,
        out_shape=jax.ShapeDtypeStruct((L, W), BF16),
        grid=(L // C,),
        in_specs=[at(OFF_XD), at(OFF_BCD), pl.BlockSpec((C, W), lambda t: (t, P2_ZD)),
                  pl.BlockSpec((C, LANE), lambda t: (t, OFF_DT // LANE)),
                  full(cw), full(cb), full(dtb), full(a), full(e), full(d_e), full(nw)],
        out_specs=pl.BlockSpec((C, W), lambda t: (t, 0)),
        scratch_shapes=[pltpu.VMEM((SUB + C, 2 * W), F32),
                        pltpu.VMEM((M2_GROUPS, M2_STATE, W // M2_GROUPS), F32)],
        compiler_params=_cparams(("arbitrary",)),
    )(p, p, p2, p, cw, cb, dtb, a, e, d_e, nw)


def _rwkv_kernel(r_ref, k_ref, v_ref, wa_ref, z_ref, mu_ref, muwa_ref, w0_ref, w2_ref, a0_ref, a2_ref,
                 kk_ref, ka_ref, rk_ref, lnw_ref, lnb_ref,
                 o_ref, xb_ref, wab_ref, st_ref):
    C = RWKV_CHUNK
    T = o_ref.shape[0]

    @pl.when(pl.program_id(0) == 0)
    def _():
        xb_ref[0:SUB, :] = jnp.zeros((SUB, 3 * W), F32)
        wab_ref[0:SUB, :] = jnp.zeros((SUB, LANE), F32)
        st_ref[...] = jnp.zeros_like(st_ref)

    xb_ref[SUB:SUB + T, 0:W] = r_ref[...]
    xb_ref[SUB:SUB + T, W:2 * W] = k_ref[...]
    xb_ref[SUB:SUB + T, 2 * W:3 * W] = v_ref[...]
    wab_ref[SUB:SUB + T, :] = wa_ref[...]
    cur = xb_ref[SUB:SUB + T, :]
    xs = cur + mu_ref[...] * (xb_ref[pl.ds(SUB - 1, T), :] - cur)
    wa_cur = wab_ref[SUB:SUB + T, :]
    wa = wa_cur + muwa_ref[...] * (wab_ref[pl.ds(SUB - 1, T), :] - wa_cur)
    xb_ref[0:SUB, :] = xb_ref[T:T + SUB, :]
    wab_ref[0:SUB, :] = wab_ref[T:T + SUB, :]

    logw_all = -RWKV_DECAY_SCALE * jax.nn.sigmoid(w0_ref[...] + _dot(jnp.tanh(wa), w2_ref[...]))
    iclr_all = jax.nn.sigmoid(a0_ref[...] + _dot(wa, a2_ref[...]))

    lane = lax.broadcasted_iota(jnp.int32, (C, LANE), 1)
    lo = lane < RWKV_HEAD

    def hsum(x):
        s_lo = jnp.sum(jnp.where(lo, x, 0.0), axis=-1, keepdims=True)
        s_hi = jnp.sum(jnp.where(lo, 0.0, x), axis=-1, keepdims=True)
        return jnp.where(lo, s_lo, s_hi)
    t_i = lax.broadcasted_iota(jnp.int32, (C, 2 * C), 0)
    s_i = lax.broadcasted_iota(jnp.int32, (C, 2 * C), 1) % C
    strict = t_i > s_i
    incl_uv = (lax.broadcasted_iota(jnp.int32, (C, 4 * C), 0)
               >= lax.broadcasted_iota(jnp.int32, (C, 4 * C), 1) % C)
    r2 = lax.broadcasted_iota(jnp.int32, (2 * C, 2 * C), 0)
    c2 = lax.broadcasted_iota(jnp.int32, (2 * C, 2 * C), 1)
    same_blk = (r2 // RWKV_SUB) == (c2 // RWKV_SUB)
    eye = (r2 == c2).astype(F32)

    def stack(x):
        return jnp.concatenate([jnp.where(lo, x, 0.0), jnp.where(lo, 0.0, x)], axis=0)

    pairs = list(range(W // LANE))
    chunks = list(range(T // C))
    sls = [slice(i * LANE, (i + 1) * LANE) for i in pairs]
    rws = [slice(c * C, (c + 1) * C) for c in chunks]
    ds = [[dict() for _ in pairs] for _ in chunks]

    def p_keys(c, i):
        sl, rw, d = sls[i], rws[c], ds[c]
        k = xs[rw, W + i * LANE:W + (i + 1) * LANE]
        kk = k * kk_ref[:, sl]
        d[i].update(r=xs[rw, sl], v=xs[rw, 2 * W + i * LANE:2 * W + (i + 1) * LANE],
                    kk=kk, ss=hsum(kk * kk), k2=k * (1.0 + (iclr_all[rw, sl] - 1.0) * ka_ref[:, sl]))

    def p_norm(c, i):
        d = ds[c]
        d[i]['kk'] = d[i]['kk'] / jnp.maximum(jnp.sqrt(d[i]['ss']), RWKV_KNORM_FLOOR)

    def p_decay(c, i):
        cum = _cumsum_rows(logw_all[rws[c], sls[i]])
        ds[c][i].update(cum=cum, e_inv=jnp.exp(-cum))

    def p_ar(c, i):
        d = ds[c]
        cum = d[i]['cum']
        d[i]['ar'] = jnp.concatenate([-d[i]['kk'] * jnp.exp(cum - logw_all[rws[c], sls[i]]),
                                      d[i]['r'] * jnp.exp(cum)], axis=0).astype(BF16)

    def p_bk(c, i):
        d = ds[c]
        e_inv = d[i]['e_inv']
        d[i]['bk'] = jnp.concatenate([stack(d[i]['kk'] * iclr_all[rws[c], sls[i]] * e_inv),
                                      stack(d[i]['k2'] * e_inv)], axis=0).astype(BF16)
        d[i]['v_s'] = stack(d[i]['v']).astype(BF16)

    def s_scores(c, i):
        d = ds[c]
        d[i]['sc'] = _dot_nt(d[i]['ar'], d[i]['bk'])

    def s_state(c, i):
        d = ds[c]
        d[i]['ah'] = _dot_nt(d[i]['ar'], st_ref[i])

    def s_w(c, i):
        d = ds[c]
        sc = d[i]['sc']
        d[i]['w'] = d[i]['ah'][0:C, :] + _dot(jnp.where(strict, sc[0:C, 2 * C:4 * C], 0.0), d[i]['v_s'])

    def s_n(c, i):
        d = ds[c]
        n_p = stack(jnp.where(strict, d[i]['sc'][0:C, 0:2 * C], 0.0))
        n_d = jnp.where(same_blk, n_p, 0.0)
        d[i].update(n_p=n_p, n_d=n_d, pw=n_d, t_d=eye + n_d)

    def s_pow(c, i):
        d = ds[c]
        d[i]['pw'] = _dot(d[i]['pw'], d[i]['pw'])
        d[i]['t_d'] = d[i]['t_d'] + _dot(d[i]['pw'], d[i]['t_d'])

    def s_m(c, i):
        d = ds[c]
        d[i]['m'] = _dot(d[i]['t_d'], d[i]['n_p'] - d[i]['n_d'])
        d[i]['u'] = _dot(d[i]['t_d'], stack(d[i]['w']))

    def s_m2(c, i):
        d = ds[c]
        d[i]['m2'] = _dot(d[i]['m'], d[i]['m'])

    def s_u1(c, i):
        d = ds[c]
        d[i]['u'] = d[i]['u'] + _dot(d[i]['m2'], d[i]['u'])

    def s_u2(c, i):
        d = ds[c]
        d[i]['u'] = d[i]['u'] + _dot(d[i]['m'], d[i]['u'])

    def s_y(c, i):
        d = ds[c]
        uv = jnp.concatenate([d[i]['u'].astype(BF16), d[i]['v_s']], axis=0)
        d[i]['uv'] = uv
        d[i]['y'] = d[i]['ah'][C:2 * C, :] + _dot(jnp.where(incl_uv, d[i]['sc'][C:2 * C, :], 0.0), uv)

    def s_carry(c, i):
        d = ds[c]
        st_ref[i] = (st_ref[i] + _dot_tn(d[i]['uv'], d[i]['bk'])) * jnp.exp(d[i]['cum'][C - 1:C, :])

    def e_center(c, i):
        d = ds[c]
        d[i]['yc'] = d[i]['y'] - hsum(d[i]['y']) * (1.0 / RWKV_HEAD)

    def e_var(c, i):
        d = ds[c]
        d[i]['var'] = hsum(d[i]['yc'] * d[i]['yc']) * (1.0 / RWKV_HEAD)
        d[i]['bonus'] = hsum(d[i]['r'] * d[i]['k2'] * rk_ref[:, sls[i]]) * d[i]['v']

    def e_out(c, i):
        sl, rw, d = sls[i], rws[c], ds[c]
        yn = d[i]['yc'] * lax.rsqrt(d[i]['var'] + RWKV_LN_EPS) * lnw_ref[:, sl] + lnb_ref[:, sl]
        o_ref[rw, sl] = ((yn + d[i]['bonus']) * _silu(z_ref[rw, sl])).astype(o_ref.dtype)

    prep = (p_keys, p_norm, p_decay, p_ar, p_bk)
    solve = (s_scores, s_state, s_w, s_n, s_pow, s_pow, s_pow, s_m, s_m2, s_u1, s_u2, s_y, s_carry)
    finish = (e_center, e_var, e_out)

    def run(stage, c):
        for i in pairs:
            stage(c, i)

    for stage in prep:
        run(stage, 0)
    for c in chunks:
        fill = ([(s_, c + 1) for s_ in prep] if c + 1 < len(chunks) else []) \
            + ([(s_, c - 1) for s_ in finish] if c > 0 else [])
        for n, stage in enumerate(solve):
            run(stage, c)
            if n < len(fill):
                run(*fill[n])
        for extra in fill[len(solve):]:
            run(*extra)
    for stage in finish:
        run(stage, chunks[-1])


def _rwkv(p, mu, muwa, w0, w2p, a0, a2p, k_k, k_a, r_k, ln_w, ln_b):
    L = p.shape[0]
    C = min(RWKV_TILE, L)
    wide = lambda c: pl.BlockSpec((C, W), lambda t, c=c: (t, c))
    full = lambda a_: pl.BlockSpec(a_.shape, lambda t: (0,) * a_.ndim)
    consts = (mu, muwa, w0, w2p, a0, a2p, k_k, k_a, r_k, ln_w, ln_b)
    return pl.pallas_call(
        _rwkv_kernel,
        out_shape=jax.ShapeDtypeStruct((L, W), BF16),
        grid=(L // C,),
        in_specs=[wide(COL_RC), wide(COL_KC), wide(COL_VC),
                  pl.BlockSpec((C, LANE), lambda t: (t, OFF_WA // LANE)),
                  pl.BlockSpec((pl.Element(C), pl.Element(W)), lambda t: (t * C, OFF_ZC))]
                 + [full(c) for c in consts],
        out_specs=pl.BlockSpec((C, W), lambda t: (t, 0)),
        scratch_shapes=[pltpu.VMEM((SUB + C, 3 * W), F32),
                        pltpu.VMEM((SUB + C, LANE), F32),
                        pltpu.VMEM((W // LANE, LANE, LANE), F32)],
        compiler_params=_cparams(("arbitrary",)),
    )(p, p, p, p, p, *consts)


def _piece_transpose(vs):
    vs = list(vs)
    piece = lax.broadcasted_iota(jnp.int32, vs[0].shape, 1) // S5_GROUP
    for d in (4, 2, 1):
        keep_lo = (piece & d) == 0
        nxt = list(vs)
        for a in range(len(vs)):
            if a & d == 0:
                lo, hi = vs[a], vs[a + d]
                nxt[a] = jnp.where(keep_lo, lo, pltpu.roll(hi, d * S5_GROUP, 1))
                nxt[a + d] = jnp.where(keep_lo, pltpu.roll(lo, LANE - d * S5_GROUP, 1), hi)
        vs = nxt
    return vs


def _s5_kernel(u_ref, m_ref, toe_ref, q_ref, l1_ref, l2_ref, l2s_ref, y_ref,
               xl_ref, xls_ref, xp_ref, car_ref):
    @pl.when(pl.program_id(1) == 0)
    def _():
        car_ref[...] = jnp.zeros_like(car_ref)

    tn = u_ref.shape[0]
    half = S5_CHUNK // 2
    halves = [_piece_transpose([u_ref[:, hf * half + p, :] for p in range(half)]) for hf in range(2)]
    lg = [jnp.concatenate([halves[0][g], halves[1][g]], axis=1).astype(BF16) for g in range(S5_GBLK)]
    xl = jnp.concatenate([jnp.dot(lg[g], m_ref[g], preferred_element_type=F32) for g in range(S5_GBLK)],
                         axis=1)
    xl_ref[...] = xl
    xls_ref[...] = jnp.concatenate(
        [pltpu.roll(xl[:, g * LANE:(g + 1) * LANE], S5_STATE, 1) for g in range(S5_GBLK)], axis=1)
    l1, l2, l2s = l1_ref[...], l2_ref[...], l2s_ref[...]

    def step(i, carry):
        x, xs = carry
        row = pl.ds(i, 1)
        xp_ref[row, :] = x
        return (l1 * x + l2 * xs + xl_ref[row, :], l1 * xs + l2s * x + xls_ref[row, :])

    x, xs = lax.fori_loop(0, tn, step, (car_ref[0:1, :], car_ref[1:2, :]))
    car_ref[0:1, :] = x
    car_ref[1:2, :] = xs
    xp = xp_ref[...].astype(BF16)
    y = [jnp.dot(lg[g], toe_ref[g], preferred_element_type=F32)
         + jnp.dot(xp[:, g * LANE:(g + 1) * LANE], q_ref[g], preferred_element_type=F32)
         for g in range(S5_GBLK)]
    for hf in range(2):
        out = _piece_transpose([y[g][:, hf * LANE:(hf + 1) * LANE] for g in range(S5_GBLK)])
        for p in range(half):
            y_ref[:, hf * half + p, :] = out[p]


def _s5_scan(u3, m, toe, q, l1, l2, l2s):
    n = u3.shape[0]
    tn = min(512, n)
    nb = W // LANE
    sw = S5_GBLK * 2 * S5_STATE
    per_b = lambda a_: pl.BlockSpec((S5_GBLK,) + a_.shape[1:], lambda b, i: (b, 0, 0))
    vec = pl.BlockSpec((None, 1, sw), lambda b, i: (b, 0, 0))
    return pl.pallas_call(
        _s5_kernel,
        out_shape=jax.ShapeDtypeStruct((n, S5_CHUNK, W), F32),
        grid=(nb, n // tn),
        in_specs=[pl.BlockSpec((tn, S5_CHUNK, LANE), lambda b, i: (i, 0, b)),
                  per_b(m), per_b(toe), per_b(q), vec, vec, vec],
        out_specs=pl.BlockSpec((tn, S5_CHUNK, LANE), lambda b, i: (i, 0, b)),
        scratch_shapes=[pltpu.VMEM((tn, sw), F32)] * 3 + [pltpu.VMEM((SUB, sw), F32)],
        compiler_params=_cparams(("arbitrary", "arbitrary")),
    )(u3, m, toe, q, l1, l2, l2s)


def _s5_out_kernel(y_ref, u_ref, z_ref, d_ref, wg_ref, bg_ref, o_ref):
    y = y_ref[...].reshape(u_ref.shape) + d_ref[...] * u_ref[...]
    g = jax.nn.gelu(y)
    o = g * jax.nn.sigmoid(_dot(g, wg_ref[...]) + bg_ref[...])
    o_ref[...] = (o * _silu(z_ref[...])).astype(o_ref.dtype)


def _s5_out(y3, p, d, wg, bg):
    L = p.shape[0]
    tm = min(512, L)
    vec = pl.BlockSpec((1, W), lambda t: (0, 0))
    return pl.pallas_call(
        _s5_out_kernel,
        out_shape=jax.ShapeDtypeStruct((L, W), BF16),
        grid=(L // tm,),
        in_specs=[pl.BlockSpec((tm // S5_CHUNK, S5_CHUNK, W), lambda t: (t, 0, 0)),
                  pl.BlockSpec((tm, W), lambda t: (t, COL_UB)),
                  pl.BlockSpec((tm, W), lambda t: (t, COL_ZB)),
                  vec, pl.BlockSpec((W, W), lambda t: (0, 0)), vec],
        out_specs=pl.BlockSpec((tm, W), lambda t: (t, 0)),
        compiler_params=_cparams(("parallel",)),
    )(y3, p, p, d, wg, bg)


def _s5_tables(a_re, a_im, log_dt, b_re, b_im, c_re, c_im):
    J = S5_CHUNK
    lam_re = jnp.minimum(a_re, S5_MAX_RE)
    lam_im = a_im
    dt = jnp.exp(log_dt)[:, None]
    dl_re, dl_im = lam_re * dt, lam_im * dt
    mag = jnp.exp(dl_re)
    num_re = mag * jnp.cos(dl_im) - 1.0
    num_im = mag * jnp.sin(dl_im)
    den = lam_re * lam_re + lam_im * lam_im
    coef_re = (num_re * lam_re + num_im * lam_im) / den
    coef_im = (num_im * lam_re - num_re * lam_im) / den
    bb_re = coef_re[..., None] * b_re - coef_im[..., None] * b_im
    bb_im = coef_re[..., None] * b_im + coef_im[..., None] * b_re
    tau = jnp.arange(J + 1, dtype=F32)[:, None, None]
    pm = jnp.exp(tau * dl_re)
    pw_re, pw_im = pm * jnp.cos(tau * dl_im), pm * jnp.sin(tau * dl_im)
    pb_re = pw_re[..., None] * bb_re - pw_im[..., None] * bb_im
    pb_im = pw_re[..., None] * bb_im + pw_im[..., None] * bb_re
    kern = (jnp.einsum('gop,tgph->tgoh', c_re, pb_re[:J], precision=HI)
            - jnp.einsum('gop,tgph->tgoh', c_im, pb_im[:J], precision=HI))
    t_idx = jnp.arange(J)
    lag = t_idx[None, :] - t_idx[:, None]
    onehot = (lag[:, :, None] == t_idx[None, None, :]).astype(F32)
    toe = jnp.einsum('stu,ugoh->stgoh', onehot, kern, precision=HI)
    nb, gb = S5_GROUPS // S5_GBLK, S5_GBLK
    toe = toe.transpose(2, 0, 4, 1, 3).reshape(S5_GROUPS, J * S5_GROUP, J * S5_GROUP)
    rev_re, rev_im = pb_re[J - 1 - t_idx], pb_im[J - 1 - t_idx]
    to_m = lambda x: x.transpose(1, 0, 3, 2).reshape(S5_GROUPS, J * S5_GROUP, S5_STATE)
    m = jnp.concatenate([to_m(rev_re), to_m(rev_im)], axis=-1)
    q_re = (c_re[None] * pw_re[1:, :, None, :] - c_im[None] * pw_im[1:, :, None, :])
    q_im = -(c_re[None] * pw_im[1:, :, None, :] + c_im[None] * pw_re[1:, :, None, :])
    to_q = lambda x: x.transpose(1, 3, 0, 2).reshape(S5_GROUPS, S5_STATE, J * S5_GROUP)
    q = jnp.concatenate([to_q(q_re), to_q(q_im)], axis=1)
    lj_re, lj_im = pw_re[J], pw_im[J]
    rs = lambda parts: jnp.concatenate(parts, axis=-1).reshape(nb, 1, gb * 2 * S5_STATE)
    return (m.astype(BF16), toe.astype(BF16), q.astype(BF16),
            rs([lj_re, lj_re]), rs([-lj_im, lj_im]), rs([lj_im, -lj_im]))


def _s5(u3, p, tables, d, wg, bg):
    return _s5_out(_s5_scan(u3, *tables), p, d, wg, bg)


def _up_kernel(oa_ref, ob_ref, oc_ref, od_ref, ga_ref, gb_ref, gc_ref, gd_ref, w_ref, bias_ref, o_ref):
    acc = None
    for b, (o_b, g_b) in enumerate(((oa_ref, ga_ref), (ob_ref, gb_ref), (oc_ref, gc_ref), (od_ref, gd_ref))):
        gate = jax.nn.sigmoid(g_b[...] + bias_ref[b:b + 1, :])
        contrib = gate * jnp.dot(o_b[...], w_ref[b], preferred_element_type=F32)
        acc = contrib if acc is None else acc + contrib
    o_ref[...] = acc.astype(o_ref.dtype)


def _up(outs, p, w_up, bias):
    L = p.shape[0]
    tm = min(1024, L)
    tn = 512
    nb = D_MODEL // tn
    o_spec = pl.BlockSpec((tm, W), lambda i, j: (i, 0))
    g_spec = lambda b: pl.BlockSpec((tm, tn), lambda i, j, b=b: (i, (P2_GATE * W + b * D_MODEL) // tn + j))
    return pl.pallas_call(
        _up_kernel,
        out_shape=jax.ShapeDtypeStruct((L, D_MODEL), BF16),
        grid=(L // tm, nb),
        in_specs=[o_spec] * 4 + [g_spec(b) for b in range(N_BRANCH)]
                 + [pl.BlockSpec((N_BRANCH, W, tn), lambda i, j: (0, 0, j)),
                    pl.BlockSpec((N_BRANCH, tn), lambda i, j: (0, j))],
        out_specs=pl.BlockSpec((tm, tn), lambda i, j: (i, j)),
        compiler_params=_cparams(("parallel", "arbitrary")),
    )(*outs, p, p, p, p, w_up, bias)


def _out_kernel(m_ref, w_ref, g_ref, res_ref, o_ref):
    y = jnp.dot(m_ref[...], w_ref[...], preferred_element_type=F32)
    o_ref[...] = res_ref[...] + (y * lax.rsqrt(jnp.mean(y * y, axis=-1, keepdims=True) + NORM_EPS)
                                 * g_ref[...])


def _out(merged, w_out, g, res):
    L = res.shape[1]
    tm = min(512, L)
    return pl.pallas_call(
        _out_kernel,
        out_shape=jax.ShapeDtypeStruct((1, L, D_MODEL), F32),
        grid=(L // tm,),
        in_specs=[pl.BlockSpec((tm, D_MODEL), lambda i: (i, 0)),
                  pl.BlockSpec((D_MODEL, D_MODEL), lambda i: (0, 0)),
                  pl.BlockSpec((1, D_MODEL), lambda i: (0, 0)),
                  pl.BlockSpec((None, tm, D_MODEL), lambda i: (0, i, 0))],
        out_specs=pl.BlockSpec((None, tm, D_MODEL), lambda i: (0, i, 0)),
        compiler_params=_cparams(("parallel",)),
    )(merged, w_out, g.reshape(1, D_MODEL), res)


def _pack_w_in(w):
    wb = w.astype(BF16)
    return wb, wb[:, OFF_P2:OFF_P2 + P2_COLS]


def _row(v):
    return v.astype(F32).reshape(1, -1)


def kernel(x, norm_pre, norm_post, w_in, gate_bias, w_up, w_out, hgrn_lb_logits, hgrn_norm, s5_a_re, s5_a_im, s5_log_dt, s5_b_re, s5_b_im, s5_c_re, s5_c_im, s5_d, s5_w_glu, s5_b_glu, rwkv_mu, rwkv_w0, rwkv_w2, rwkv_a0, rwkv_a2, rwkv_k_k, rwkv_k_a, rwkv_r_k, rwkv_ln_w, rwkv_ln_b, m2_conv_w, m2_conv_b, m2_dt_bias, m2_a_log, m2_d, m2_norm):
    depth = w_in.shape[0]
    L = x.shape[1]
    lb_all = jnp.cumsum(jax.nn.softmax(hgrn_lb_logits.astype(F32), axis=0), axis=0)
    lb_all = lb_all - lb_all[0:1]

    zpad = jnp.zeros((RWKV_LORA, W), F32)
    hpad = jnp.zeros((1, LANE - M2_HEADS), F32)

    res = x.astype(F32)
    for l in range(depth):
        h = _prenorm(res, norm_pre[l])
        w1, w2 = _pack_w_in(w_in[l])
        p, u3 = _inproj(h, w1)
        p2 = _matmul(h, w2, P2_TN, BF16)

        lb = lb_all[l]
        o_a = _hgrn(p, _row(jnp.log(lb)), _row(jnp.log1p(-lb)), _row(1.0 - lb), _row(hgrn_norm[l]))

        tables = _s5_tables(s5_a_re[l].astype(F32), s5_a_im[l].astype(F32), s5_log_dt[l].astype(F32),
                            s5_b_re[l].astype(F32), s5_b_im[l].astype(F32),
                            s5_c_re[l].astype(F32), s5_c_im[l].astype(F32))
        o_b = _s5(u3, p, tables, _row(s5_d[l]), s5_w_glu[l].astype(BF16), _row(s5_b_glu[l]))

        mu = rwkv_mu[l].astype(F32)
        o_c = _rwkv(p, _row(mu[:3 * W]), _row(mu[3 * W:]), _row(rwkv_w0[l]),
                    jnp.concatenate([rwkv_w2[l].astype(F32), zpad], axis=0).astype(BF16),
                    _row(rwkv_a0[l]),
                    jnp.concatenate([zpad, rwkv_a2[l].astype(F32)], axis=0).astype(BF16),
                    _row(rwkv_k_k[l]), _row(rwkv_k_a[l]), _row(rwkv_r_k[l]),
                    _row(rwkv_ln_w[l]), _row(rwkv_ln_b[l]))

        o_d = _ssd(p, p2, m2_conv_w[l].astype(F32), _row(m2_conv_b[l]),
                   jnp.concatenate([_row(m2_dt_bias[l]), hpad], axis=1),
                   jnp.concatenate([-jnp.exp(_row(m2_a_log[l])), hpad], axis=1),
                   _row(jnp.repeat(m2_d[l].astype(F32), M2_HEADDIM)), _row(m2_norm[l]))

        merged = _up((o_a, o_b, o_c, o_d), p2, w_up[l].astype(BF16), gate_bias[l].astype(F32))
        res = _out(merged, w_out[l].astype(BF16), norm_post[l].astype(F32), res)
    return res.astype(x.dtype)
```
